```python
import jax, jax.numpy as jnp
from jax import lax
import numpy as np

D_MODEL = 1024
BATCH = 4
SEQ = 4096
DEPTH = 1
DEC_BATCH = 128
DEC_SEQ = 8
PAST_LEN = 8192
PAGE_SIZE = 128

HD = 64
N_ATT_HEADS = D_MODEL // 128
ATT_WIDTH = N_ATT_HEADS * HD
MLSTM_HEADS = 4
MLSTM_WIDTH = D_MODEL - ATT_WIDTH
MLSTM_HD = MLSTM_WIDTH // MLSTM_HEADS
ROT_DIM = HD // 4
ROPE_THETA = 500000.0
DILATED = ((128, 1), (512, 4), (2048, 16))
MAX_WINDOW = 2048
BLOCK = 128
CONV_W = 4
CHUNK = 128
D_FF = 4 * D_MODEL
EPS = 1e-6
IN_SIZES = (ATT_WIDTH, ATT_WIDTH, ATT_WIDTH, 2 * MLSTM_WIDTH, MLSTM_WIDTH, MLSTM_WIDTH, 2 * MLSTM_HEADS)
IN_COLS = sum(IN_SIZES)
IN_SPLITS = tuple(int(s) for s in np.cumsum(IN_SIZES)[:-1])

kernel_name = 'dilated_swa_mlstm_hybrid_step'


def rms_norm(x, g):
    xf = x.astype(jnp.float32)
    y = xf * lax.rsqrt(jnp.mean(xf * xf, axis=-1, keepdims=True) + EPS)
    return (y * g.astype(jnp.float32)).astype(x.dtype)


def rotary(x, pos):
    half = ROT_DIM // 2
    inv_freq = ROPE_THETA ** (-jnp.arange(half, dtype=jnp.float32) / half)
    ang = pos.astype(jnp.float32)[:, None] * inv_freq[None, :]
    cos = jnp.cos(ang)[None, :, None, :]
    sin = jnp.sin(ang)[None, :, None, :]
    x1 = x[..., :half].astype(jnp.float32)
    x2 = x[..., half:ROT_DIM].astype(jnp.float32)
    rot = jnp.concatenate([x1 * cos - x2 * sin, x2 * cos + x1 * sin], axis=-1).astype(x.dtype)
    return jnp.concatenate([rot, x[..., ROT_DIM:]], axis=-1)


def from_blocks(x, B, S, dil):
    L = S // dil
    rest = x.shape[3:]
    x = x.reshape((B, dil, -1) + rest)[:, :, :L]
    perm = (0, 2, 1) + tuple(range(3, x.ndim))
    return x.transpose(perm).reshape((B, S) + rest)


def banded_residue_attention(q, k, v, window, dil):
    B, S, H, Dh = q.shape
    L = S // dil
    nb = -(-L // BLOCK)
    Lp = nb * BLOCK
    sub_w = window // dil

    def to_blocks(x):
        x = x.reshape(B, L, dil, H, Dh).transpose(0, 2, 1, 3, 4).reshape(B * dil, L, H, Dh)
        x = jnp.pad(x, ((0, 0), (0, Lp - L), (0, 0), (0, 0)))
        return x.reshape(B * dil, nb, BLOCK, H, Dh)

    def with_prev(x):
        prev = jnp.pad(x[:, :-1], ((0, 0), (1, 0), (0, 0), (0, 0), (0, 0)))
        return jnp.concatenate([prev, x], axis=2)

    qb = to_blocks(q)
    kk = with_prev(to_blocks(k))
    vv = with_prev(to_blocks(v))
    s = jnp.einsum('bnqhd,bnkhd->bnhqk', qb, kk, preferred_element_type=jnp.float32)
    qi = jnp.arange(BLOCK)
    ki = jnp.arange(2 * BLOCK) - BLOCK
    rel = qi[:, None] - ki[None, :]
    start = jnp.arange(nb) * BLOCK
    valid = ((rel >= 0) & (rel <= sub_w))[None] & ((start[:, None, None] + ki[None, None, :]) >= 0)
    s = jnp.where(valid[None, :, None], s, -jnp.inf)
    mx = jnp.max(s, axis=-1, keepdims=True)
    p = jnp.exp(s - mx)
    den = jnp.sum(p, axis=-1)
    o = jnp.einsum('bnhqk,bnkhd->bnqhd', p.astype(v.dtype), vv, preferred_element_type=jnp.float32)
    o = o / jnp.swapaxes(den, 2, 3)[..., None]
    lse = jnp.swapaxes(mx[..., 0] + jnp.log(den), 2, 3)
    return from_blocks(o, B, S, dil), from_blocks(lse, B, S, dil)


def combine_by_denominator(outs, lses):
    w = jax.nn.softmax(jnp.stack(lses, 0), axis=0)
    return jnp.einsum('gbth,gbthd->bthd', w, jnp.stack(outs, 0))


def dilated_attention_prompt(q, k, v):
    outs, lses = [], []
    for window, dil in DILATED:
        o, lse = banded_residue_attention(q, k, v, window, dil)
        outs.append(o)
        lses.append(lse)
    return combine_by_denominator(outs, lses)


def dilated_attention_sample(q, kc, vc):
    B, T, H, Dh = q.shape
    W = kc.shape[1] - T
    ti = jnp.arange(T)
    outs, lses = [], []
    for window, dil in DILATED:
        j = jnp.arange(window // dil + 1)
        idx = W + ti[:, None] - dil * j[None, :]
        valid = idx >= 0
        idx = jnp.maximum(idx, 0)
        kg = kc[:, idx]
        vg = vc[:, idx]
        s = jnp.einsum('bthd,btjhd->bthj', q, kg, preferred_element_type=jnp.float32)
        s = jnp.where(valid[None, :, None, :], s, -jnp.inf)
        mx = jnp.max(s, axis=-1, keepdims=True)
        p = jnp.exp(s - mx)
        den = jnp.sum(p, axis=-1)
        o = jnp.einsum('bthj,btjhd->bthd', p.astype(vc.dtype), vg, preferred_element_type=jnp.float32)
        outs.append(o / den[..., None])
        lses.append(mx[..., 0] + jnp.log(den))
    return combine_by_denominator(outs, lses)


def mlstm_chunkwise(q, k, v, ig, lf, C0, n0, m0, chunk):
    B, T, H, D = q.shape
    nc = T // chunk
    to_c4 = lambda x: x.reshape(B, nc, chunk, H, D).transpose(1, 0, 3, 2, 4)
    to_c3 = lambda x: x.reshape(B, nc, chunk, H).transpose(1, 0, 3, 2)
    causal = jnp.tril(jnp.ones((chunk, chunk), dtype=bool))

    def step(carry, inp):
        C, n, m = carry
        qc, kc, vc, igc, lfc = inp
        b = jnp.cumsum(lfc, axis=-1)
        a = b + m[..., None]
        Dm = b[..., :, None] - b[..., None, :] + igc[..., None, :]
        Dm = jnp.where(causal, Dm, -jnp.inf)
        mt = jnp.maximum(a, jnp.max(Dm, axis=-1))
        w_inter = jnp.exp(a - mt)
        Wm = jnp.exp(Dm - mt[..., None])
        sc = jnp.einsum('bhtd,bhsd->bhts', qc, kc) * Wm
        num = w_inter[..., None] * jnp.einsum('bhtd,bhde->bhte', qc, C) + jnp.einsum('bhts,bhse->bhte', sc, vc)
        den = w_inter * jnp.einsum('bhtd,bhd->bht', qc, n) + jnp.sum(sc, axis=-1)
        h = num / jnp.maximum(jnp.abs(den), jnp.exp(-mt))[..., None]
        mT = mt[..., -1]
        wT_inter = jnp.exp(a[..., -1] - mT)
        wT = jnp.exp(b[..., -1:] - b + igc - mT[..., None])
        C_new = wT_inter[..., None, None] * C + jnp.einsum('bhs,bhsd,bhse->bhde', wT, kc, vc)
        n_new = wT_inter[..., None] * n + jnp.einsum('bhs,bhsd->bhd', wT, kc)
        return (C_new, n_new, mT), h

    (C1, n1, m1), hs = lax.scan(step, (C0, n0, m0), (to_c4(q), to_c4(k), to_c4(v), to_c3(ig), to_c3(lf)))
    h = hs.transpose(1, 0, 3, 2, 4).reshape(B, T, H, D)
    return h, C1, n1, m1


def mixer_block(h, pos, past, w_in, b_gate, w_conv, b_conv, mh_norm_g, w_out):
    B, T, _ = h.shape
    f32 = jnp.float32
    proj = jnp.einsum('btd,dc->btc', h, w_in)
    qa, ka, va, qk_raw, vb, ob, gates = jnp.split(proj, IN_SPLITS, axis=-1)
    qa = rotary(qa.reshape(B, T, N_ATT_HEADS, HD), pos) * (HD ** -0.5)
    ka = rotary(ka.reshape(B, T, N_ATT_HEADS, HD), pos)
    va = va.reshape(B, T, N_ATT_HEADS, HD)
    if past is None:
        att = dilated_attention_prompt(qa, ka, va)
        n_keep = min(MAX_WINDOW, T)
        new_k, new_v = ka[:, T - n_keep:], va[:, T - n_keep:]
        conv_buf = jnp.zeros((B, CONV_W - 1, 2 * MLSTM_WIDTH), h.dtype)
        C0 = jnp.zeros((B, MLSTM_HEADS, MLSTM_HD, MLSTM_HD), f32)
        n0 = jnp.zeros((B, MLSTM_HEADS, MLSTM_HD), f32)
        m0 = jnp.zeros((B, MLSTM_HEADS), f32)
        chunk = CHUNK if T % CHUNK == 0 else T
    else:
        win_k, win_v, conv_buf, C0, n0, m0 = past
        kc = jnp.concatenate([win_k.astype(ka.dtype), ka], axis=1)
        vc = jnp.concatenate([win_v.astype(va.dtype), va], axis=1)
        att = dilated_attention_sample(qa, kc, vc)
        n_keep = win_k.shape[1]
        new_k, new_v = kc[:, -n_keep:], vc[:, -n_keep:]
        C0, n0, m0 = C0.astype(f32), n0.astype(f32), m0.astype(f32)
        chunk = T
    xp = jnp.concatenate([conv_buf.astype(h.dtype), qk_raw], axis=1)
    conv = b_conv + sum(xp[:, w:w + T] * w_conv[w] for w in range(CONV_W))
    new_conv = xp[:, T:]
    qb, kb = jnp.split(jax.nn.silu(conv), 2, axis=-1)
    qb = qb.reshape(B, T, MLSTM_HEADS, MLSTM_HD).astype(f32)
    kb = kb.reshape(B, T, MLSTM_HEADS, MLSTM_HD).astype(f32) * (MLSTM_HD ** -0.5)
    vb = vb.reshape(B, T, MLSTM_HEADS, MLSTM_HD).astype(f32)
    ig, fg = jnp.split((gates + b_gate).astype(f32), 2, axis=-1)
    lf = jax.nn.log_sigmoid(fg)
    hb, C1, n1, m1 = mlstm_chunkwise(qb, kb, vb, ig, lf, C0, n0, m0, chunk)
    hb = hb * lax.rsqrt(jnp.mean(hb * hb, axis=-1, keepdims=True) + EPS)
    hb = hb * mh_norm_g.reshape(MLSTM_HEADS, MLSTM_HD).astype(f32)
    hb = hb * jax.nn.sigmoid(ob.reshape(B, T, MLSTM_HEADS, MLSTM_HD).astype(f32))
    mix = jnp.concatenate([att.reshape(B, T, ATT_WIDTH).astype(h.dtype), hb.reshape(B, T, MLSTM_WIDTH).astype(h.dtype)], axis=-1)
    out = jnp.einsum('btc,cd->btd', mix, w_out)
    dt = h.dtype
    return out, (new_k, new_v, new_conv, C1.astype(dt), n1.astype(dt), m1.astype(dt))


def sqrelu_mlp(h, w_up, w_down):
    u = jnp.einsum('btd,df->btf', h, w_up)
    return jnp.einsum('btf,fd->btd', jnp.square(jax.nn.relu(u)), w_down)


def setup_inputs(seed: int = 0) -> dict:
    key = jax.random.key(seed)
    ks = jax.random.split(key, 20)
    nrm = jax.random.normal
    wb = min(MAX_WINDOW, PAST_LEN)
    f_bias = jnp.linspace(3.0, 6.0, MLSTM_HEADS, dtype=jnp.float32)[None, :] + 0.1 * nrm(ks[9], (DEPTH, MLSTM_HEADS))
    i_bias = 0.1 * nrm(ks[10], (DEPTH, MLSTM_HEADS))
    return {
        'x_prompt': nrm(ks[0], (BATCH, SEQ, D_MODEL), jnp.float32),
        'x_sample': nrm(ks[1], (DEC_BATCH, DEC_SEQ, D_MODEL), jnp.float32),
        'cache_win_k': nrm(ks[2], (DEPTH, DEC_BATCH, wb, N_ATT_HEADS, HD), jnp.float32),
        'cache_win_v': nrm(ks[3], (DEPTH, DEC_BATCH, wb, N_ATT_HEADS, HD), jnp.float32),
        'state_conv': nrm(ks[4], (DEPTH, DEC_BATCH, CONV_W - 1, 2 * MLSTM_WIDTH), jnp.float32),
        'state_C': 0.2 * nrm(ks[5], (DEPTH, DEC_BATCH, MLSTM_HEADS, MLSTM_HD, MLSTM_HD), jnp.float32),
        'state_n': 0.2 * nrm(ks[6], (DEPTH, DEC_BATCH, MLSTM_HEADS, MLSTM_HD), jnp.float32),
        'state_m': jax.random.uniform(ks[7], (DEPTH, DEC_BATCH, MLSTM_HEADS), jnp.float32, 0.0, 4.0),
        'norm1_g': 1.0 + 0.01 * nrm(ks[8], (DEPTH, D_MODEL)),
        'w_in': nrm(ks[11], (DEPTH, D_MODEL, IN_COLS)) * D_MODEL ** -0.5,
        'b_gate': jnp.concatenate([i_bias, f_bias], axis=-1),
        'w_conv': nrm(ks[12], (DEPTH, CONV_W, 2 * MLSTM_WIDTH)) * CONV_W ** -0.5,
        'b_conv': 0.01 * nrm(ks[13], (DEPTH, 2 * MLSTM_WIDTH)),
        'mh_norm_g': 1.0 + 0.01 * nrm(ks[14], (DEPTH, MLSTM_WIDTH)),
        'w_out': nrm(ks[15], (DEPTH, D_MODEL, D_MODEL)) * D_MODEL ** -0.5,
        'norm2_g': 1.0 + 0.01 * nrm(ks[16], (DEPTH, D_MODEL)),
        'w_up': nrm(ks[17], (DEPTH, D_MODEL, D_FF)) * D_MODEL ** -0.5,
        'w_down': nrm(ks[18], (DEPTH, D_FF, D_MODEL)) * D_FF ** -0.5,
        'norm_f_g': 1.0 + 0.01 * nrm(ks[19], (D_MODEL,)),
    }


def reference(x_prompt, x_sample, cache_win_k, cache_win_v, state_conv, state_C, state_n, state_m,
              norm1_g, w_in, b_gate, w_conv, b_conv, mh_norm_g, w_out, norm2_g, w_up, w_down, norm_f_g):
    pos_p = jnp.arange(x_prompt.shape[1])
    pos_s = PAST_LEN + jnp.arange(x_sample.shape[1])
    hp, hs = x_prompt, x_sample
    st_p, st_s = [], []
    for l in range(DEPTH):
        params = (w_in[l], b_gate[l], w_conv[l], b_conv[l], mh_norm_g[l], w_out[l])
        a_p, sp = mixer_block(rms_norm(hp, norm1_g[l]), pos_p, None, *params)
        past = (cache_win_k[l], cache_win_v[l], state_conv[l], state_C[l], state_n[l], state_m[l])
        a_s, ss = mixer_block(rms_norm(hs, norm1_g[l]), pos_s, past, *params)
        hp = hp + a_p
        hs = hs + a_s
        hp = hp + sqrelu_mlp(rms_norm(hp, norm2_g[l]), w_up[l], w_down[l])
        hs = hs + sqrelu_mlp(rms_norm(hs, norm2_g[l]), w_up[l], w_down[l])
        st_p.append(sp)
        st_s.append(ss)
    y_prompt = rms_norm(hp, norm_f_g)
    y_sample = rms_norm(hs, norm_f_g)
    pk, pv, pconv, pC, pn, pm = (jnp.stack([s[i] for s in st_p], 0) for i in range(6))
    sk, sv, sconv, sC, sn, sm = (jnp.stack([s[i] for s in st_s], 0) for i in range(6))
    return (y_prompt, y_sample, pk, pv, pconv, pC, pn, pm, sk, sv, sconv, sC, sn, sm)
```

```python
import functools

import jax
import jax.numpy as jnp
import numpy as np
from jax import lax
from jax.experimental import pallas as pl
from jax.experimental.pallas import tpu as pltpu

D_MODEL = 1024
HD = 64
N_ATT_HEADS = 8
ATT_WIDTH = N_ATT_HEADS * HD
MLSTM_HEADS = 4
MLSTM_WIDTH = D_MODEL - ATT_WIDTH
MLSTM_HD = MLSTM_WIDTH // MLSTM_HEADS
ROT_DIM = HD // 4
ROT_HALF = ROT_DIM // 2
ROPE_THETA = 500000.0
DILATIONS = (1, 4, 16)
SUB_WINDOW = 128
ATT_BLOCK = 128
CONV_W = 4
CHUNK = 128
D_FF = 4 * D_MODEL
EPS = 1e-6
PAST_LEN = 8192
IN_SIZES = (ATT_WIDTH, ATT_WIDTH, ATT_WIDTH, 2 * MLSTM_WIDTH, MLSTM_WIDTH, MLSTM_WIDTH, 2 * MLSTM_HEADS)
IN_COLS = sum(IN_SIZES)

LANES = 128
SUBLANES = 8
GATE_PAD = LANES
IN_COLS_PAD = IN_COLS - 2 * MLSTM_HEADS + GATE_PAD
VMEM_LIMIT = 56 * 1024 * 1024

F32 = jnp.float32
BF16 = jnp.bfloat16
NEG_INF = float("-inf")


def _cparams(sem):
    return pltpu.CompilerParams(dimension_semantics=sem, vmem_limit_bytes=VMEM_LIMIT)


def _const_spec(shape):
    nd = len(shape)
    return pl.BlockSpec(shape, lambda *_: (0,) * nd, pipeline_mode=pl.Buffered(1))


def _rms(x, g):
    return x * lax.rsqrt(jnp.mean(x * x, axis=-1, keepdims=True) + EPS) * g


def _inproj_kernel(x_ref, g_ref, w_ref, bg_ref, cos_ref, sa_ref, sb_ref,
                   q_ref, k_ref, v_ref, raw_ref, vb_ref, ob_ref, gate_ref):
    h = _rms(x_ref[...], g_ref[...]).astype(BF16)

    def proj(lo, width):
        return jnp.dot(h, w_ref[:, lo:lo + width], preferred_element_type=F32)

    cos, sa, sb = cos_ref[...], sa_ref[...], sb_ref[...]

    def rotary_store(dst, y, scale):
        for c in range(ATT_WIDTH // LANES):
            yc = y[:, c * LANES:(c + 1) * LANES]
            up = pltpu.roll(yc, LANES - ROT_HALF, 1)
            dn = pltpu.roll(yc, ROT_HALF, 1)
            r = yc * cos + up * sa + dn * sb
            dst[:, c * LANES:(c + 1) * LANES] = r * scale if scale != 1.0 else r

    off = 0
    rotary_store(q_ref, proj(off, ATT_WIDTH), HD ** -0.5)
    off += ATT_WIDTH
    rotary_store(k_ref, proj(off, ATT_WIDTH), 1.0)
    off += ATT_WIDTH
    v_ref[...] = proj(off, ATT_WIDTH)
    off += ATT_WIDTH
    raw_ref[...] = proj(off, 2 * MLSTM_WIDTH)
    off += 2 * MLSTM_WIDTH
    vb_ref[...] = proj(off, MLSTM_WIDTH)
    off += MLSTM_WIDTH
    ob_ref[...] = proj(off, MLSTM_WIDTH)
    off += MLSTM_WIDTH
    gate_ref[...] = proj(off, GATE_PAD) + bg_ref[...]


def _inproj(x2d, g1, w_in_bf, bg_pad, cos_t, sa_t, sb_t, tm, table_blocks):
    n = x2d.shape[0]
    grid = (n // tm,)
    row = lambda i: (i, 0)
    tab = lambda i: (i % table_blocks, 0)
    widths = (ATT_WIDTH, ATT_WIDTH, ATT_WIDTH, 2 * MLSTM_WIDTH, MLSTM_WIDTH, MLSTM_WIDTH, GATE_PAD)
    return pl.pallas_call(
        _inproj_kernel,
        grid=grid,
        in_specs=[
            pl.BlockSpec((tm, D_MODEL), row),
            _const_spec((1, D_MODEL)),
            _const_spec((D_MODEL, IN_COLS_PAD)),
            _const_spec((1, GATE_PAD)),
            pl.BlockSpec((tm, LANES), tab),
            pl.BlockSpec((tm, LANES), tab),
            pl.BlockSpec((tm, LANES), tab),
        ],
        out_specs=[pl.BlockSpec((tm, w), row) for w in widths],
        out_shape=[jax.ShapeDtypeStruct((n, w), F32) for w in widths],
        compiler_params=_cparams(("parallel",)),
        name="inproj",
    )(x2d, g1, w_in_bf, bg_pad, cos_t, sa_t, sb_t)


def _rotary_tables(pos):
    inv_freq = ROPE_THETA ** (-jnp.arange(ROT_HALF, dtype=F32) / ROT_HALF)
    ang = pos.astype(F32)[:, None] * inv_freq[None, :]
    cos, sin = jnp.cos(ang), jnp.sin(ang)
    p = pos.shape[0]
    ones = jnp.ones((p, HD - ROT_DIM), F32)
    zeros = jnp.zeros((p, HD - ROT_DIM), F32)
    z8 = jnp.zeros((p, ROT_HALF), F32)
    cos_h = jnp.concatenate([cos, cos, ones], axis=1)
    sa_h = jnp.concatenate([-sin, z8, zeros], axis=1)
    sb_h = jnp.concatenate([z8, sin, zeros], axis=1)
    two = lambda t: jnp.concatenate([t, t], axis=1)
    return two(cos_h), two(sa_h), two(sb_h)


def _attn_prompt_kernel(bias_ref, q_ref, k_ref, v_ref, o_ref, num_ref, m_ref, den_ref, *, seq):
    lane = lax.broadcasted_iota(jnp.int32, (ATT_BLOCK, LANES), 1)
    head0 = lane < HD
    blocks_total = seq // ATT_BLOCK

    def block_scores(q, kk, vv, first):
        q2 = jnp.concatenate([jnp.where(head0, q, 0.0), jnp.where(head0, 0.0, q)], axis=0).astype(BF16)
        s = lax.dot_general(q2, kk.astype(BF16), (((1,), (1,)), ((), ())),
                            preferred_element_type=F32)
        s = s + bias_ref[first]
        mx = jnp.max(s, axis=1, keepdims=True)
        p = jnp.exp(s - mx)
        den = jnp.sum(p, axis=1, keepdims=True)
        o2 = jnp.dot(p.astype(BF16), vv.astype(BF16), preferred_element_type=F32)
        o = jnp.where(head0, o2[:ATT_BLOCK], o2[ATT_BLOCK:])
        mxb = jnp.where(head0, mx[:ATT_BLOCK], mx[ATT_BLOCK:])
        denb = jnp.where(head0, den[:ATT_BLOCK], den[ATT_BLOCK:])
        return o, mxb, denb

    for dil in DILATIONS:
        nb = blocks_total // dil
        span = ATT_BLOCK * dil

        def body(i, carry, dil=dil, nb=nb, span=span):
            r = i // nb
            n = i % nb
            start = r + n * span
            prev = r + jnp.maximum(n - 1, 0) * span
            first = jnp.where(n == 0, 1, 0)
            if dil == 1:
                start = pl.multiple_of(start, ATT_BLOCK)
                prev = pl.multiple_of(prev, ATT_BLOCK)
                rows = pl.ds(start, ATT_BLOCK)
                prows = pl.ds(prev, ATT_BLOCK)
            else:
                rows = pl.ds(start, ATT_BLOCK, stride=dil)
                prows = pl.ds(prev, ATT_BLOCK, stride=dil)
            q = q_ref[rows, :]
            kk = jnp.concatenate([k_ref[prows, :], k_ref[rows, :]], axis=0)
            vv = jnp.concatenate([v_ref[prows, :], v_ref[rows, :]], axis=0)
            o, mxb, denb = block_scores(q, kk, vv, first)
            if dil == 1:
                num_ref[rows, :] = o
                m_ref[rows, :] = mxb
                den_ref[rows, :] = denb
            else:
                m_old = m_ref[rows, :]
                m_new = jnp.maximum(m_old, mxb)
                a = jnp.exp(m_old - m_new)
                b = jnp.exp(mxb - m_new)
                num_ref[rows, :] = num_ref[rows, :] * a + o * b
                den_ref[rows, :] = den_ref[rows, :] * a + denb * b
                m_ref[rows, :] = m_new
            return carry

        lax.fori_loop(0, blocks_total, body, 0)

    def finish(i, carry):
        rows = pl.ds(pl.multiple_of(i * ATT_BLOCK, ATT_BLOCK), ATT_BLOCK)
        o_ref[rows, :] = num_ref[rows, :] / den_ref[rows, :]
        return carry

    lax.fori_loop(0, blocks_total, finish, 0)


def _band_bias():
    row = np.arange(2 * ATT_BLOCK)[:, None] % ATT_BLOCK
    ki = np.arange(2 * ATT_BLOCK)[None, :] - ATT_BLOCK
    rel = row - ki
    valid = (rel >= 0) & (rel <= SUB_WINDOW)
    b0 = np.where(valid, 0.0, NEG_INF).astype(np.float32)
    b1 = np.where(valid & (ki >= 0), 0.0, NEG_INF).astype(np.float32)
    return jnp.asarray(np.stack([b0, b1]))


def _attn_prompt(q, k, v):
    b, s, _ = q.shape
    assert s % (ATT_BLOCK * DILATIONS[-1]) == 0
    blk = pl.BlockSpec((None, s, LANES), lambda i, j: (i, 0, j))
    return pl.pallas_call(
        functools.partial(_attn_prompt_kernel, seq=s),
        grid=(b, ATT_WIDTH // LANES),
        in_specs=[_const_spec((2, 2 * ATT_BLOCK, 2 * ATT_BLOCK)), blk, blk, blk],
        out_specs=blk,
        out_shape=jax.ShapeDtypeStruct((b, s, ATT_WIDTH), F32),
        scratch_shapes=[pltpu.VMEM((s, LANES), F32)] * 3,
        compiler_params=_cparams(("parallel", "parallel")),
        name="attn_prompt",
    )(_band_bias(), q, k, v)


def _attn_sample_kernel(mw_ref, mn_ref, q_ref, knt_ref, vnt_ref, kc_ref, vc_ref,
                        att_ref, nk_ref, nv_ref, *, win, t_new):
    mult_w = mw_ref[...]
    mult_n = mn_ref[...]
    lane = lax.broadcasted_iota(jnp.int32, (HD, LANES), 1)
    keep = lane < LANES - t_new
    for h in range(N_ATT_HEADS):
        qh = q_ref[h].astype(BF16)
        kc = kc_ref[h]
        vc = vc_ref[h]
        knt = knt_ref[h]
        vnt = vnt_ref[h]
        s_w = jnp.dot(qh, kc.astype(BF16), preferred_element_type=F32)
        s_n = jnp.dot(qh, knt.astype(BF16), preferred_element_type=F32)
        s_w = jnp.where(mult_w > 0, s_w, NEG_INF)
        s_n = jnp.where(mult_n > 0, s_n, NEG_INF)
        mx = jnp.maximum(jnp.max(s_w, axis=1, keepdims=True), jnp.max(s_n, axis=1, keepdims=True))
        p_w = mult_w * jnp.exp(s_w - mx)
        p_n = mult_n * jnp.exp(s_n - mx)
        den = jnp.sum(p_w, axis=1, keepdims=True) + jnp.sum(p_n, axis=1, keepdims=True)
        o = lax.dot_general(p_w.astype(BF16), vc.astype(BF16), (((1,), (1,)), ((), ())),
                            preferred_element_type=F32)
        o = o + lax.dot_general(p_n.astype(BF16), vnt.astype(BF16), (((1,), (1,)), ((), ())),
                                preferred_element_type=F32)
        att_ref[h] = o / den
        for src, new, dst in ((kc, knt, nk_ref), (vc, vnt, nv_ref)):
            rolled = pltpu.roll(src, win - t_new, 1)
            dst[h, :, 0:win - LANES] = rolled[:, 0:win - LANES]
            dst[h, :, win - LANES:win] = jnp.where(keep, rolled[:, win - LANES:win], new)


def _sample_multiplicity(win, t_new):
    t = np.arange(t_new)[:, None]
    idx = np.arange(win + t_new)[None, :]
    back = win + t - idx
    mult = np.zeros((t_new, win + t_new), np.float32)
    for dil in DILATIONS:
        mult += ((back >= 0) & (back % dil == 0) & (back // dil <= SUB_WINDOW)).astype(np.float32)
    mw = mult[:, :win]
    mn = np.zeros((t_new, LANES), np.float32)
    mn[:, LANES - t_new:] = mult[:, win:]
    return jnp.asarray(mw), jnp.asarray(mn)


def _attn_sample(q_hd, knt_pad, vnt_pad, kc_t, vc_t):
    b, nh, t_new, _ = q_hd.shape
    win = kc_t.shape[-1]
    assert win == DILATIONS[-1] * SUB_WINDOW and t_new <= DILATIONS[-1] and t_new == SUBLANES
    mw, mn = _sample_multiplicity(win, t_new)
    per_b = lambda *shape: pl.BlockSpec((None,) + shape, lambda i: (i,) + (0,) * len(shape))
    return pl.pallas_call(
        functools.partial(_attn_sample_kernel, win=win, t_new=t_new),
        grid=(b,),
        in_specs=[
            _const_spec((t_new, win)), _const_spec((t_new, LANES)),
            per_b(nh, t_new, HD), per_b(nh, HD, LANES), per_b(nh, HD, LANES),
            per_b(nh, HD, win), per_b(nh, HD, win),
        ],
        out_specs=[per_b(nh, t_new, HD), per_b(nh, HD, win), per_b(nh, HD, win)],
        out_shape=[
            jax.ShapeDtypeStruct((b, nh, t_new, HD), F32),
            jax.ShapeDtypeStruct((b, nh, HD, win), F32),
            jax.ShapeDtypeStruct((b, nh, HD, win), F32),
        ],
        compiler_params=_cparams(("parallel",)),
        name="attn_sample",
    )(mw, mn, q_hd, knt_pad, vnt_pad, kc_t, vc_t)


def _split_dot(tri, x):
    hi = x.astype(BF16)
    lo = (x - hi.astype(F32)).astype(BF16)
    return (jnp.dot(tri, hi, preferred_element_type=F32)
            + jnp.dot(tri, lo, preferred_element_type=F32))


def _log_sigmoid(x):
    return jnp.minimum(x, 0.0) - jnp.log1p(jnp.exp(-jnp.abs(x)))


def _mlstm_kernel(*refs, chunk, has_state):
    if has_state:
        (raw_ref, vb_ref, ob_ref, gate_ref, wc_ref, bc_ref, g_ref, tail_ref, c0_ref, n0_ref, m0_ref,
         hb_ref, tail_out_ref, c_out_ref, n_out_ref, m_out_ref, xp_ref, c_sc, n_sc, m_sc) = refs
    else:
        (raw_ref, vb_ref, ob_ref, gate_ref, wc_ref, bc_ref, g_ref,
         hb_ref, tail_out_ref, c_out_ref, n_out_ref, m_out_ref, xp_ref, c_sc, n_sc, m_sc) = refs
    L = chunk
    c_idx = pl.program_id(1)

    @pl.when(c_idx == 0)
    def _():
        if has_state:
            xp_ref[0:SUBLANES, :] = tail_ref[...]
            c_sc[...] = c0_ref[...]
            n_sc[...] = n0_ref[...]
            m_sc[...] = m0_ref[...]
        else:
            xp_ref[0:SUBLANES, :] = jnp.zeros((SUBLANES, 2 * MLSTM_WIDTH), F32)
            c_sc[...] = jnp.zeros_like(c_sc)
            n_sc[...] = jnp.zeros_like(n_sc)
            m_sc[...] = jnp.zeros_like(m_sc)

    raw = raw_ref[...]
    xp_ref[SUBLANES:SUBLANES + L, :] = raw
    conv = bc_ref[...]
    for w in range(CONV_W):
        lo = SUBLANES - (CONV_W - 1) + w
        conv = conv + xp_ref[lo:lo + L, :] * wc_ref[w:w + 1, :]
    tail = raw[L - SUBLANES:L, :]
    xp_ref[0:SUBLANES, :] = tail
    tail_out_ref[...] = tail
    act = conv * jax.nn.sigmoid(conv)

    gates = gate_ref[...]
    lf = _log_sigmoid(gates)
    r_i = lax.broadcasted_iota(jnp.int32, (L, L), 0)
    c_i = lax.broadcasted_iota(jnp.int32, (L, L), 1)
    causal = r_i >= c_i
    tri_l = jnp.where(causal, 1.0, 0.0).astype(BF16)
    tri_u = jnp.where(r_i <= c_i, 1.0, 0.0).astype(BF16)
    b_col = _split_dot(tri_l, lf)
    if L == LANES:
        gates_sq = gates
    else:
        gates_sq = jnp.concatenate([gates, jnp.zeros((LANES - L, LANES), F32)], axis=0)
    gates_t = jnp.transpose(gates_sq)[0:SUBLANES, 0:L]
    lf_t = _log_sigmoid(gates_t)
    b_row = _split_dot_rows(lf_t, tri_u)

    for h in range(MLSTM_HEADS):
        sl = slice(h * MLSTM_HD, (h + 1) * MLSTM_HD)
        q = act[:, sl]
        k = act[:, MLSTM_WIDTH + h * MLSTM_HD:MLSTM_WIDTH + (h + 1) * MLSTM_HD] * (MLSTM_HD ** -0.5)
        v = vb_ref[:, sl]
        qb, kb, vb = q.astype(BF16), k.astype(BF16), v.astype(BF16)
        ig_c = gates[:, h:h + 1]
        b_c = b_col[:, MLSTM_HEADS + h:MLSTM_HEADS + h + 1]
        ig_r = gates_t[h:h + 1, :]
        b_r = b_row[MLSTM_HEADS + h:MLSTM_HEADS + h + 1, :]
        m_prev = m_sc[h:h + 1, 0:1]
        a_c = b_c + m_prev
        dm = jnp.where(causal, b_c - b_r + ig_r, NEG_INF)
        mt = jnp.maximum(a_c, jnp.max(dm, axis=1, keepdims=True))
        w_inter = jnp.exp(a_c - mt)
        wm = jnp.exp(dm - mt)
        sc = lax.dot_general(qb, kb, (((1,), (1,)), ((), ())), preferred_element_type=F32) * wm
        c_old = c_sc[h]
        n_old = n_sc[h:h + 1, :]
        num = (w_inter * jnp.dot(qb, c_old.astype(BF16), preferred_element_type=F32)
               + jnp.dot(sc.astype(BF16), vb, preferred_element_type=F32))
        den = (w_inter * jnp.sum(q * n_old, axis=1, keepdims=True)
               + jnp.sum(sc, axis=1, keepdims=True))
        hh = num / jnp.maximum(jnp.abs(den), jnp.exp(-mt))
        m_last = mt[L - 1:L, :]
        w_last = jnp.exp(a_c[L - 1:L, :] - m_last)
        w_t = jnp.exp(b_c[L - 1:L, :] - b_c + ig_c - m_last)
        kw = k * w_t
        if L == LANES:
            kw_sq, v_sq = kw, v
        else:
            pad = jnp.zeros((LANES - L, MLSTM_HD), F32)
            kw_sq = jnp.concatenate([kw, pad], axis=0)
            v_sq = jnp.concatenate([v, pad], axis=0)
        c_new = w_last * c_old + jnp.dot(jnp.transpose(kw_sq).astype(BF16), v_sq.astype(BF16),
                                         preferred_element_type=F32)
        n_new = w_last * n_old + jnp.sum(kw, axis=0, keepdims=True)
        c_sc[h] = c_new
        n_sc[h:h + 1, :] = n_new
        m_sc[h:h + 1, :] = jnp.broadcast_to(m_last, (1, LANES))
        hn = hh * lax.rsqrt(jnp.mean(hh * hh, axis=1, keepdims=True) + EPS)
        hb_ref[:, sl] = hn * g_ref[:, sl] * jax.nn.sigmoid(ob_ref[:, sl])

    @pl.when(c_idx == pl.num_programs(1) - 1)
    def _():
        c_out_ref[...] = c_sc[...]
        n_out_ref[...] = n_sc[...]
        m_out_ref[...] = m_sc[...]


def _split_dot_rows(x, tri):
    hi = x.astype(BF16)
    lo = (x - hi.astype(F32)).astype(BF16)
    return (jnp.dot(hi, tri, preferred_element_type=F32)
            + jnp.dot(lo, tri, preferred_element_type=F32))


def _mlstm(raw, vb, ob, gates, w_conv, b_conv, mh_g, chunk, state=None):
    b, t, _ = raw.shape
    nc = t // chunk
    assert nc * chunk == t and chunk % SUBLANES == 0
    seq = lambda w: pl.BlockSpec((None, chunk, w), lambda i, j: (i, j, 0))
    per_b = lambda *shape: pl.BlockSpec((None,) + shape, lambda i, j: (i,) + (0,) * len(shape))
    in_specs = [seq(2 * MLSTM_WIDTH), seq(MLSTM_WIDTH), seq(MLSTM_WIDTH), seq(GATE_PAD),
                _const_spec((CONV_W, 2 * MLSTM_WIDTH)), _const_spec((1, 2 * MLSTM_WIDTH)),
                _const_spec((1, MLSTM_WIDTH))]
    args = [raw, vb, ob, gates, w_conv, b_conv, mh_g]
    state_specs = [per_b(SUBLANES, 2 * MLSTM_WIDTH), per_b(MLSTM_HEADS, MLSTM_HD, MLSTM_HD),
                   per_b(MLSTM_HEADS, MLSTM_HD), per_b(MLSTM_HEADS, LANES)]
    if state is not None:
        in_specs += state_specs
        args += list(state)
    return pl.pallas_call(
        functools.partial(_mlstm_kernel, chunk=chunk, has_state=state is not None),
        grid=(b, nc),
        in_specs=in_specs,
        out_specs=[seq(MLSTM_WIDTH)] + state_specs,
        out_shape=[
            jax.ShapeDtypeStruct((b, t, MLSTM_WIDTH), F32),
            jax.ShapeDtypeStruct((b, SUBLANES, 2 * MLSTM_WIDTH), F32),
            jax.ShapeDtypeStruct((b, MLSTM_HEADS, MLSTM_HD, MLSTM_HD), F32),
            jax.ShapeDtypeStruct((b, MLSTM_HEADS, MLSTM_HD), F32),
            jax.ShapeDtypeStruct((b, MLSTM_HEADS, LANES), F32),
        ],
        scratch_shapes=[
            pltpu.VMEM((SUBLANES + chunk, 2 * MLSTM_WIDTH), F32),
            pltpu.VMEM((MLSTM_HEADS, MLSTM_HD, MLSTM_HD), F32),
            pltpu.VMEM((MLSTM_HEADS, MLSTM_HD), F32),
            pltpu.VMEM((MLSTM_HEADS, LANES), F32),
        ],
        compiler_params=_cparams(("parallel", "arbitrary")),
        name="mlstm_state" if state is not None else "mlstm",
    )(*args)


def _outmlp_kernel(x_ref, att_ref, hb_ref, wo_ref, g2_ref, wu_ref, wd_ref, gf_ref, y_ref, *, ff_chunk):
    mix = (jnp.dot(att_ref[...].astype(BF16), wo_ref[0:ATT_WIDTH, :], preferred_element_type=F32)
           + jnp.dot(hb_ref[...].astype(BF16), wo_ref[ATT_WIDTH:D_MODEL, :], preferred_element_type=F32))
    x1 = x_ref[...] + mix
    h2 = _rms(x1, g2_ref[...]).astype(BF16)
    acc = x1
    for c in range(D_FF // ff_chunk):
        u = jnp.dot(h2, wu_ref[:, c * ff_chunk:(c + 1) * ff_chunk], preferred_element_type=F32)
        r = jnp.maximum(u, 0.0)
        acc = acc + jnp.dot((r * r).astype(BF16), wd_ref[c * ff_chunk:(c + 1) * ff_chunk, :],
                            preferred_element_type=F32)
    y_ref[...] = _rms(acc, gf_ref[...])


def _outmlp(x2d, att2d, hb2d, wo_bf, g2, wu_bf, wd_bf, gf, tm):
    n = x2d.shape[0]
    row = lambda w: pl.BlockSpec((tm, w), lambda i: (i, 0))
    return pl.pallas_call(
        functools.partial(_outmlp_kernel, ff_chunk=1024),
        grid=(n // tm,),
        in_specs=[row(D_MODEL), row(ATT_WIDTH), row(MLSTM_WIDTH),
                  _const_spec((D_MODEL, D_MODEL)), _const_spec((1, D_MODEL)),
                  _const_spec((D_MODEL, D_FF)), _const_spec((D_FF, D_MODEL)), _const_spec((1, D_MODEL))],
        out_specs=row(D_MODEL),
        out_shape=jax.ShapeDtypeStruct((n, D_MODEL), F32),
        compiler_params=_cparams(("parallel",)),
        name="outmlp",
    )(x2d, att2d, hb2d, wo_bf, g2, wu_bf, wd_bf, gf)


def _layer(x, pos, params, past, tm):
    (g1, w_in_bf, bg_pad, w_conv, b_conv, mh_g, wo_bf, g2, wu_bf, wd_bf, gf) = params
    b, t, _ = x.shape
    x2d = x.reshape(b * t, D_MODEL)
    tm = min(tm, b * t)
    cos_t, sa_t, sb_t = _rotary_tables(pos)
    if past is None:
        table_blocks = t // tm
    else:
        reps = tm // t
        cos_t, sa_t, sb_t = (jnp.tile(a, (reps, 1)) for a in (cos_t, sa_t, sb_t))
        table_blocks = 1
    q, k, v, raw, vb, ob, gates = _inproj(x2d, g1, w_in_bf, bg_pad, cos_t, sa_t, sb_t, tm, table_blocks)
    r3 = lambda a: a.reshape(b, t, a.shape[-1])
    if past is None:
        att = _attn_prompt(r3(q), r3(k), r3(v)).reshape(b * t, ATT_WIDTH)
        n_keep = min(DILATIONS[-1] * SUB_WINDOW, t)
        heads = lambda a: r3(a)[:, t - n_keep:].reshape(b, n_keep, N_ATT_HEADS, HD)
        new_k, new_v = heads(k), heads(v)
        hb, tail, c1, n1, m1 = _mlstm(r3(raw), r3(vb), r3(ob), r3(gates), w_conv, b_conv, mh_g,
                                      CHUNK if t % CHUNK == 0 else t)
    else:
        win_k, win_v, conv_buf, c0, n0, m0 = past
        to_hd = lambda a: a.reshape(b, t, N_ATT_HEADS, HD).transpose(0, 2, 1, 3)
        to_hdt_pad = lambda a: jnp.pad(a.reshape(b, t, N_ATT_HEADS, HD).transpose(0, 2, 3, 1),
                                       ((0, 0), (0, 0), (0, 0), (LANES - t, 0)))
        kc_t = win_k.transpose(0, 2, 3, 1)
        vc_t = win_v.transpose(0, 2, 3, 1)
        att_hd, nk_t, nv_t = _attn_sample(to_hd(q), to_hdt_pad(k), to_hdt_pad(v), kc_t, vc_t)
        att = att_hd.transpose(0, 2, 1, 3).reshape(b * t, ATT_WIDTH)
        new_k, new_v = nk_t.transpose(0, 3, 1, 2), nv_t.transpose(0, 3, 1, 2)
        tail0 = jnp.pad(conv_buf, ((0, 0), (SUBLANES - (CONV_W - 1), 0), (0, 0)))
        m0b = jnp.broadcast_to(m0[:, :, None], (b, MLSTM_HEADS, LANES))
        hb, tail, c1, n1, m1 = _mlstm(r3(raw), r3(vb), r3(ob), r3(gates), w_conv, b_conv, mh_g, t,
                                      state=(tail0, c0, n0, m0b))
    new_conv = tail[:, SUBLANES - (CONV_W - 1):, :]
    y = _outmlp(x2d, att, hb.reshape(b * t, MLSTM_WIDTH), wo_bf, g2, wu_bf, wd_bf, gf, tm)
    return y.reshape(b, t, D_MODEL), (new_k, new_v, new_conv, c1, n1, m1[:, :, 0])


def kernel(x_prompt, x_sample, cache_win_k, cache_win_v, state_conv, state_C, state_n, state_m, norm1_g, w_in, b_gate, w_conv, b_conv, mh_norm_g, w_out, norm2_g, w_up, w_down, norm_f_g):
    depth = w_in.shape[0]
    assert depth == 1, "the final norm is fused into the (single) layer's MLP kernel"
    l = 0
    n_gate = 2 * MLSTM_HEADS
    w_l = w_in[l]
    w_in_bf = jnp.concatenate(
        [w_l[:, :IN_COLS - n_gate], jnp.pad(w_l[:, IN_COLS - n_gate:], ((0, 0), (0, GATE_PAD - n_gate)))],
        axis=1).astype(BF16)
    bg_pad = jnp.pad(b_gate[l], (0, GATE_PAD - n_gate))[None, :]
    params = (norm1_g[l][None, :], w_in_bf, bg_pad, w_conv[l], b_conv[l][None, :], mh_norm_g[l][None, :],
              w_out[l].astype(BF16), norm2_g[l][None, :], w_up[l].astype(BF16), w_down[l].astype(BF16),
              norm_f_g[None, :])
    pos_p = jnp.arange(x_prompt.shape[1])
    pos_s = PAST_LEN + jnp.arange(x_sample.shape[1])
    y_p, sp = _layer(x_prompt, pos_p, params, None, 512)
    past = (cache_win_k[l], cache_win_v[l], state_conv[l], state_C[l], state_n[l], state_m[l])
    y_s, ss = _layer(x_sample, pos_s, params, past, 512)
    return (y_p, y_s) + tuple(s[None] for s in sp) + tuple(s[None] for s in ss)
```

```python
import functools

import jax
import jax.numpy as jnp
import numpy as np
from jax import lax
from jax.experimental import pallas as pl
from jax.experimental.pallas import tpu as pltpu

D_MODEL = 1024
HD = 64
N_ATT_HEADS = 8
ATT_WIDTH = N_ATT_HEADS * HD
MLSTM_HEADS = 4
MLSTM_WIDTH = D_MODEL - ATT_WIDTH
MLSTM_HD = MLSTM_WIDTH // MLSTM_HEADS
ROT_DIM = HD // 4
ROT_HALF = ROT_DIM // 2
ROPE_THETA = 500000.0
DILATIONS = (1, 4, 16)
SUB_WINDOW = 128
ATT_BLOCK = 128
ATT_UNROLL = 4
MLSTM_SAMPLE_GROUP = 8
CONV_W = 4
CHUNK = 128
D_FF = 4 * D_MODEL
EPS = 1e-6
PAST_LEN = 8192
IN_SIZES = (ATT_WIDTH, ATT_WIDTH, ATT_WIDTH, 2 * MLSTM_WIDTH, MLSTM_WIDTH, MLSTM_WIDTH, 2 * MLSTM_HEADS)
IN_COLS = sum(IN_SIZES)

LANES = 128
SUBLANES = 8
GATE_PAD = LANES
IN_COLS_PAD = IN_COLS - 2 * MLSTM_HEADS + GATE_PAD
VMEM_LIMIT = 56 * 1024 * 1024

F32 = jnp.float32
BF16 = jnp.bfloat16
NEG_INF = float("-inf")


def _cparams(sem):
    return pltpu.CompilerParams(dimension_semantics=sem, vmem_limit_bytes=VMEM_LIMIT)


def _const_spec(shape):
    nd = len(shape)
    return pl.BlockSpec(shape, lambda *_: (0,) * nd, pipeline_mode=pl.Buffered(1))


def _rms(x, g):
    return x * lax.rsqrt(jnp.mean(x * x, axis=-1, keepdims=True) + EPS) * g


def _inproj_kernel(x_ref, g_ref, w_ref, bg_ref, cos_ref, sa_ref, sb_ref,
                   q_ref, k_ref, v_ref, raw_ref, vb_ref, ob_ref, gate_ref):
    h = _rms(x_ref[...], g_ref[...]).astype(BF16)

    def proj(lo, width):
        return jnp.dot(h, w_ref[:, lo:lo + width], preferred_element_type=F32)

    cos, sa, sb = cos_ref[...], sa_ref[...], sb_ref[...]

    def rotary_store(dst, y, scale):
        for c in range(ATT_WIDTH // LANES):
            yc = y[:, c * LANES:(c + 1) * LANES]
            up = pltpu.roll(yc, LANES - ROT_HALF, 1)
            dn = pltpu.roll(yc, ROT_HALF, 1)
            r = yc * cos + up * sa + dn * sb
            dst[:, c * LANES:(c + 1) * LANES] = r * scale if scale != 1.0 else r

    off = 0
    rotary_store(q_ref, proj(off, ATT_WIDTH), HD ** -0.5)
    off += ATT_WIDTH
    rotary_store(k_ref, proj(off, ATT_WIDTH), 1.0)
    off += ATT_WIDTH
    v_ref[...] = proj(off, ATT_WIDTH)
    off += ATT_WIDTH
    raw_ref[...] = proj(off, 2 * MLSTM_WIDTH)
    off += 2 * MLSTM_WIDTH
    vb_ref[...] = proj(off, MLSTM_WIDTH)
    off += MLSTM_WIDTH
    ob_ref[...] = proj(off, MLSTM_WIDTH)
    off += MLSTM_WIDTH
    gate_ref[...] = proj(off, GATE_PAD) + bg_ref[...]


def _inproj(x2d, g1, w_in_bf, bg_pad, cos_t, sa_t, sb_t, tm, table_blocks):
    n = x2d.shape[0]
    grid = (n // tm,)
    row = lambda i: (i, 0)
    tab = lambda i: (i % table_blocks, 0)
    widths = (ATT_WIDTH, ATT_WIDTH, ATT_WIDTH, 2 * MLSTM_WIDTH, MLSTM_WIDTH, MLSTM_WIDTH, GATE_PAD)
    return pl.pallas_call(
        _inproj_kernel,
        grid=grid,
        in_specs=[
            pl.BlockSpec((tm, D_MODEL), row),
            _const_spec((1, D_MODEL)),
            _const_spec((D_MODEL, IN_COLS_PAD)),
            _const_spec((1, GATE_PAD)),
            pl.BlockSpec((tm, LANES), tab),
            pl.BlockSpec((tm, LANES), tab),
            pl.BlockSpec((tm, LANES), tab),
        ],
        out_specs=[pl.BlockSpec((tm, w), row) for w in widths],
        out_shape=[jax.ShapeDtypeStruct((n, w), F32) for w in widths],
        compiler_params=_cparams(("parallel",)),
        name="inproj",
    )(x2d, g1, w_in_bf, bg_pad, cos_t, sa_t, sb_t)


def _rotary_tables(pos):
    inv_freq = ROPE_THETA ** (-jnp.arange(ROT_HALF, dtype=F32) / ROT_HALF)
    ang = pos.astype(F32)[:, None] * inv_freq[None, :]
    cos, sin = jnp.cos(ang), jnp.sin(ang)
    p = pos.shape[0]
    ones = jnp.ones((p, HD - ROT_DIM), F32)
    zeros = jnp.zeros((p, HD - ROT_DIM), F32)
    z8 = jnp.zeros((p, ROT_HALF), F32)
    cos_h = jnp.concatenate([cos, cos, ones], axis=1)
    sa_h = jnp.concatenate([-sin, z8, zeros], axis=1)
    sb_h = jnp.concatenate([z8, sin, zeros], axis=1)
    two = lambda t: jnp.concatenate([t, t], axis=1)
    return two(cos_h), two(sa_h), two(sb_h)


def _attn_prompt_kernel(bias_ref, q_ref, k_ref, v_ref, o_ref, num_ref, m_ref, den_ref, *, seq):
    lane = lax.broadcasted_iota(jnp.int32, (ATT_BLOCK, LANES), 1)
    head0 = lane < HD
    blocks_total = seq // ATT_BLOCK

    def block_scores(q, kk, vv, first):
        q2 = jnp.concatenate([jnp.where(head0, q, 0.0), jnp.where(head0, 0.0, q)], axis=0).astype(BF16)
        s = lax.dot_general(q2, kk.astype(BF16), (((1,), (1,)), ((), ())),
                            preferred_element_type=F32)
        s = s + bias_ref[first]
        mx = jnp.max(s, axis=1, keepdims=True)
        p = jnp.exp(s - mx)
        den = jnp.sum(p, axis=1, keepdims=True)
        o2 = jnp.dot(p.astype(BF16), vv.astype(BF16), preferred_element_type=F32)
        o = jnp.where(head0, o2[:ATT_BLOCK], o2[ATT_BLOCK:])
        mxb = jnp.where(head0, mx[:ATT_BLOCK], mx[ATT_BLOCK:])
        denb = jnp.where(head0, den[:ATT_BLOCK], den[ATT_BLOCK:])
        return o, mxb, denb

    for dil in DILATIONS:
        nb = blocks_total // dil
        span = ATT_BLOCK * dil

        def body(i, carry, dil=dil, nb=nb, span=span):
            results = []
            for u in range(ATT_UNROLL):
                idx = i * ATT_UNROLL + u
                r = idx // nb
                n = idx % nb
                start = r + n * span
                prev = r + jnp.maximum(n - 1, 0) * span
                first = jnp.where(n == 0, 1, 0)
                if dil == 1:
                    rows = pl.ds(pl.multiple_of(start, ATT_BLOCK), ATT_BLOCK)
                    prows = pl.ds(pl.multiple_of(prev, ATT_BLOCK), ATT_BLOCK)
                else:
                    rows = pl.ds(start, ATT_BLOCK, stride=dil)
                    prows = pl.ds(prev, ATT_BLOCK, stride=dil)
                q = q_ref[rows, :]
                kk = jnp.concatenate([k_ref[prows, :], k_ref[rows, :]], axis=0)
                vv = jnp.concatenate([v_ref[prows, :], v_ref[rows, :]], axis=0)
                results.append((rows, block_scores(q, kk, vv, first)))
            for rows, (o, mxb, denb) in results:
                if dil == 1:
                    num_ref[rows, :] = o
                    m_ref[rows, :] = mxb
                    den_ref[rows, :] = denb
                else:
                    m_old = m_ref[rows, :]
                    m_new = jnp.maximum(m_old, mxb)
                    a = jnp.exp(m_old - m_new)
                    b = jnp.exp(mxb - m_new)
                    num_ref[rows, :] = num_ref[rows, :] * a + o * b
                    den_ref[rows, :] = den_ref[rows, :] * a + denb * b
                    m_ref[rows, :] = m_new
            return carry

        lax.fori_loop(0, blocks_total // ATT_UNROLL, body, 0)

    def finish(i, carry):
        rows = pl.ds(pl.multiple_of(i * ATT_BLOCK, ATT_BLOCK), ATT_BLOCK)
        o_ref[rows, :] = num_ref[rows, :] / den_ref[rows, :]
        return carry

    lax.fori_loop(0, blocks_total, finish, 0)


def _band_bias():
    row = np.arange(2 * ATT_BLOCK)[:, None] % ATT_BLOCK
    ki = np.arange(2 * ATT_BLOCK)[None, :] - ATT_BLOCK
    rel = row - ki
    valid = (rel >= 0) & (rel <= SUB_WINDOW)
    b0 = np.where(valid, 0.0, NEG_INF).astype(np.float32)
    b1 = np.where(valid & (ki >= 0), 0.0, NEG_INF).astype(np.float32)
    return jnp.asarray(np.stack([b0, b1]))


def _attn_prompt(q, k, v):
    b, s, _ = q.shape
    assert s % (ATT_BLOCK * DILATIONS[-1]) == 0
    blk = pl.BlockSpec((None, s, LANES), lambda i, j: (i, 0, j))
    return pl.pallas_call(
        functools.partial(_attn_prompt_kernel, seq=s),
        grid=(b, ATT_WIDTH // LANES),
        in_specs=[_const_spec((2, 2 * ATT_BLOCK, 2 * ATT_BLOCK)), blk, blk, blk],
        out_specs=blk,
        out_shape=jax.ShapeDtypeStruct((b, s, ATT_WIDTH), F32),
        scratch_shapes=[pltpu.VMEM((s, LANES), F32)] * 3,
        compiler_params=_cparams(("parallel", "parallel")),
        name="attn_prompt",
    )(_band_bias(), q, k, v)


def _attn_sample_kernel(mw_ref, mn_ref, q_ref, kn_ref, vn_ref, kc_ref, vc_ref,
                        att_ref, nk_ref, nv_ref, *, win, t_new):
    mult_w = mw_ref[...]
    mult_n = mn_ref[...]
    lane = lax.broadcasted_iota(jnp.int32, (HD, LANES), 1)
    keep = lane < LANES - t_new
    top = jnp.zeros((LANES - t_new, ATT_WIDTH), F32)
    knt_all = jnp.transpose(jnp.concatenate([top, kn_ref[...]], axis=0))
    vnt_all = jnp.transpose(jnp.concatenate([top, vn_ref[...]], axis=0))
    for h in range(N_ATT_HEADS):
        qh = q_ref[:, h * HD:(h + 1) * HD].astype(BF16)
        kc = kc_ref[h]
        vc = vc_ref[h]
        knt = knt_all[h * HD:(h + 1) * HD, :]
        vnt = vnt_all[h * HD:(h + 1) * HD, :]
        s_w = jnp.dot(qh, kc.astype(BF16), preferred_element_type=F32)
        s_n = jnp.dot(qh, knt.astype(BF16), preferred_element_type=F32)
        s_w = jnp.where(mult_w > 0, s_w, NEG_INF)
        s_n = jnp.where(mult_n > 0, s_n, NEG_INF)
        mx = jnp.maximum(jnp.max(s_w, axis=1, keepdims=True), jnp.max(s_n, axis=1, keepdims=True))
        p_w = mult_w * jnp.exp(s_w - mx)
        p_n = mult_n * jnp.exp(s_n - mx)
        den = jnp.sum(p_w, axis=1, keepdims=True) + jnp.sum(p_n, axis=1, keepdims=True)
        o = lax.dot_general(p_w.astype(BF16), vc.astype(BF16), (((1,), (1,)), ((), ())),
                            preferred_element_type=F32)
        o = o + lax.dot_general(p_n.astype(BF16), vnt.astype(BF16), (((1,), (1,)), ((), ())),
                                preferred_element_type=F32)
        att_ref[:, h * HD:(h + 1) * HD] = o / den
        for src, new, dst in ((kc, knt, nk_ref), (vc, vnt, nv_ref)):
            rolled = pltpu.roll(src, win - t_new, 1)
            dst[h, :, 0:win - LANES] = rolled[:, 0:win - LANES]
            dst[h, :, win - LANES:win] = jnp.where(keep, rolled[:, win - LANES:win], new)


def _sample_multiplicity(win, t_new):
    t = np.arange(t_new)[:, None]
    idx = np.arange(win + t_new)[None, :]
    back = win + t - idx
    mult = np.zeros((t_new, win + t_new), np.float32)
    for dil in DILATIONS:
        mult += ((back >= 0) & (back % dil == 0) & (back // dil <= SUB_WINDOW)).astype(np.float32)
    mw = mult[:, :win]
    mn = np.zeros((t_new, LANES), np.float32)
    mn[:, LANES - t_new:] = mult[:, win:]
    return jnp.asarray(mw), jnp.asarray(mn)


def _attn_sample(q, k_new, v_new, kc_t, vc_t, t_new):
    b, nh, _, win = kc_t.shape
    assert win == DILATIONS[-1] * SUB_WINDOW and t_new == SUBLANES and q.shape[0] == b * t_new
    mw, mn = _sample_multiplicity(win, t_new)
    rows = pl.BlockSpec((t_new, ATT_WIDTH), lambda i: (i, 0))
    per_b = lambda *shape: pl.BlockSpec((None,) + shape, lambda i: (i,) + (0,) * len(shape))
    return pl.pallas_call(
        functools.partial(_attn_sample_kernel, win=win, t_new=t_new),
        grid=(b,),
        in_specs=[
            _const_spec((t_new, win)), _const_spec((t_new, LANES)),
            rows, rows, rows, per_b(nh, HD, win), per_b(nh, HD, win),
        ],
        out_specs=[rows, per_b(nh, HD, win), per_b(nh, HD, win)],
        out_shape=[
            jax.ShapeDtypeStruct((b * t_new, ATT_WIDTH), F32),
            jax.ShapeDtypeStruct((b, nh, HD, win), F32),
            jax.ShapeDtypeStruct((b, nh, HD, win), F32),
        ],
        compiler_params=_cparams(("parallel",)),
        name="attn_sample",
    )(mw, mn, q, k_new, v_new, kc_t, vc_t)


def _split_dot(tri, x):
    hi = x.astype(BF16)
    lo = (x - hi.astype(F32)).astype(BF16)
    return (jnp.dot(tri, hi, preferred_element_type=F32)
            + jnp.dot(tri, lo, preferred_element_type=F32))


def _split_dot_rows(x, tri):
    hi = x.astype(BF16)
    lo = (x - hi.astype(F32)).astype(BF16)
    return (jnp.dot(hi, tri, preferred_element_type=F32)
            + jnp.dot(lo, tri, preferred_element_type=F32))


def _log_sigmoid(x):
    return jnp.minimum(x, 0.0) - jnp.log1p(jnp.exp(-jnp.abs(x)))


def _mlstm_kernel(*refs, chunk, group, has_state):
    if has_state:
        (raw_ref, vb_ref, ob_ref, gate_ref, wc_ref, bc_ref, g_ref, tail_ref, c0_ref, n0_ref, m0_ref,
         hb_ref, tail_out_ref, c_out_ref, n_out_ref, m_out_ref, xp_ref, act_ref, c_sc, n_sc, m_sc) = refs
    else:
        (raw_ref, vb_ref, ob_ref, gate_ref, wc_ref, bc_ref, g_ref,
         hb_ref, tail_out_ref, c_out_ref, n_out_ref, m_out_ref, xp_ref, act_ref, c_sc, n_sc, m_sc) = refs
    L = chunk
    c_idx = pl.program_id(1)

    @pl.when(c_idx == 0)
    def _():
        if has_state:
            for c in range(2 * MLSTM_HEADS):
                xp_ref[:, c, 0:SUBLANES, :] = tail_ref[:, :, c * LANES:(c + 1) * LANES]
            c_sc[...] = c0_ref[...]
            n_sc[...] = n0_ref[...]
            m_sc[...] = m0_ref[...]
        else:
            xp_ref[:, :, 0:SUBLANES, :] = jnp.zeros((group, 2 * MLSTM_HEADS, SUBLANES, LANES), F32)
            c_sc[...] = jnp.zeros_like(c_sc)
            n_sc[...] = jnp.zeros_like(n_sc)
            m_sc[...] = jnp.zeros_like(m_sc)

    r_i = lax.broadcasted_iota(jnp.int32, (L, L), 0)
    c_i = lax.broadcasted_iota(jnp.int32, (L, L), 1)
    causal = r_i >= c_i
    tri_l = jnp.where(causal, 1.0, 0.0).astype(BF16)
    tri_u = jnp.where(r_i <= c_i, 1.0, 0.0).astype(BF16)
    first_tap = SUBLANES - (CONV_W - 1)
    bc = bc_ref[...]
    taps = [wc_ref[w:w + 1, :] for w in range(CONV_W)]

    for g in range(group):
        tail_out_ref[g] = raw_ref[g, L - SUBLANES:L, :]
        for c in range(2 * MLSTM_HEADS):
            cl = slice(c * LANES, (c + 1) * LANES)
            raw = raw_ref[g, :, cl]
            xp_ref[g, c, SUBLANES:SUBLANES + L, :] = raw
            if L % (SUBLANES * SUBLANES) == 0:
                n_rows = L // SUBLANES
                slabs = [xp_ref[g, c, pl.ds(first_tap + j, n_rows, stride=SUBLANES), :]
                         for j in range(SUBLANES + CONV_W - 1)]
                for s in range(SUBLANES):
                    conv = bc[:, cl]
                    for w in range(CONV_W):
                        conv = conv + slabs[s + w] * taps[w][:, cl]
                    act_ref[g, c, pl.ds(s, n_rows, stride=SUBLANES), :] = conv * jax.nn.sigmoid(conv)
            else:
                conv = bc[:, cl]
                for w in range(CONV_W):
                    conv = conv + xp_ref[g, c, first_tap + w:first_tap + w + L, :] * taps[w][:, cl]
                act_ref[g, c] = conv * jax.nn.sigmoid(conv)
            xp_ref[g, c, 0:SUBLANES, :] = raw[L - SUBLANES:L, :]

        gates = gate_ref[g]
        lf = _log_sigmoid(gates)
        b_col = _split_dot(tri_l, lf)
        if L == LANES:
            gates_sq = gates
        else:
            gates_sq = jnp.concatenate([gates, jnp.zeros((LANES - L, LANES), F32)], axis=0)
        gates_t = jnp.transpose(gates_sq)[0:SUBLANES, 0:L]
        lf_t = _log_sigmoid(gates_t)
        b_row = _split_dot_rows(lf_t, tri_u)

        for h in range(MLSTM_HEADS):
            sl = slice(h * MLSTM_HD, (h + 1) * MLSTM_HD)
            q = act_ref[g, h]
            k = act_ref[g, MLSTM_HEADS + h] * (MLSTM_HD ** -0.5)
            v = vb_ref[g, :, sl]
            qb, kb, vb = q.astype(BF16), k.astype(BF16), v.astype(BF16)
            ig_c = gates[:, h:h + 1]
            b_c = b_col[:, MLSTM_HEADS + h:MLSTM_HEADS + h + 1]
            ig_r = gates_t[h:h + 1, :]
            b_r = b_row[MLSTM_HEADS + h:MLSTM_HEADS + h + 1, :]
            m_prev = m_sc[g, h:h + 1, 0:1]
            a_c = b_c + m_prev
            dm = jnp.where(causal, b_c - b_r + ig_r, NEG_INF)
            mt = jnp.maximum(a_c, jnp.max(dm, axis=1, keepdims=True))
            w_inter = jnp.exp(a_c - mt)
            wm = jnp.exp(dm - mt)
            sc = lax.dot_general(qb, kb, (((1,), (1,)), ((), ())), preferred_element_type=F32) * wm
            c_old = c_sc[g, h]
            n_old = n_sc[g, h:h + 1, :]
            num = (w_inter * jnp.dot(qb, c_old.astype(BF16), preferred_element_type=F32)
                   + jnp.dot(sc.astype(BF16), vb, preferred_element_type=F32))
            den = (w_inter * jnp.sum(q * n_old, axis=1, keepdims=True)
                   + jnp.sum(sc, axis=1, keepdims=True))
            hh = num / jnp.maximum(jnp.abs(den), jnp.exp(-mt))
            m_last = mt[L - 1:L, :]
            w_last = jnp.exp(a_c[L - 1:L, :] - m_last)
            w_t = jnp.exp(b_c[L - 1:L, :] - b_c + ig_c - m_last)
            kw = k * w_t
            if L == LANES:
                kw_sq, v_sq = kw, v
            else:
                pad = jnp.zeros((LANES - L, MLSTM_HD), F32)
                kw_sq = jnp.concatenate([kw, pad], axis=0)
                v_sq = jnp.concatenate([v, pad], axis=0)
            c_new = w_last * c_old + jnp.dot(jnp.transpose(kw_sq).astype(BF16), v_sq.astype(BF16),
                                             preferred_element_type=F32)
            n_new = w_last * n_old + jnp.sum(kw, axis=0, keepdims=True)
            c_sc[g, h] = c_new
            n_sc[g, h:h + 1, :] = n_new
            m_sc[g, h:h + 1, :] = jnp.broadcast_to(m_last, (1, LANES))
            hn = hh * lax.rsqrt(jnp.mean(hh * hh, axis=1, keepdims=True) + EPS)
            hb_ref[g, :, sl] = hn * g_ref[:, sl] * jax.nn.sigmoid(ob_ref[g, :, sl])

    @pl.when(c_idx == pl.num_programs(1) - 1)
    def _():
        c_out_ref[...] = c_sc[...]
        n_out_ref[...] = n_sc[...]
        m_out_ref[...] = m_sc[...]


def _mlstm(raw, vb, ob, gates, w_conv, b_conv, mh_g, chunk, group, state=None):
    b, t, _ = raw.shape
    nc = t // chunk
    assert nc * chunk == t and chunk % SUBLANES == 0 and b % group == 0
    seq = lambda w: pl.BlockSpec((group, chunk, w), lambda i, j: (i, j, 0))
    per_b = lambda *shape: pl.BlockSpec((group,) + shape, lambda i, j: (i,) + (0,) * len(shape))
    in_specs = [seq(2 * MLSTM_WIDTH), seq(MLSTM_WIDTH), seq(MLSTM_WIDTH), seq(GATE_PAD),
                _const_spec((CONV_W, 2 * MLSTM_WIDTH)), _const_spec((1, 2 * MLSTM_WIDTH)),
                _const_spec((1, MLSTM_WIDTH))]
    args = [raw, vb, ob, gates, w_conv, b_conv, mh_g]
    state_shapes = [(SUBLANES, 2 * MLSTM_WIDTH), (MLSTM_HEADS, MLSTM_HD, MLSTM_HD),
                    (MLSTM_HEADS, MLSTM_HD), (MLSTM_HEADS, LANES)]
    state_specs = [per_b(*sh) for sh in state_shapes]
    if state is not None:
        in_specs += state_specs
        args += list(state)
    return pl.pallas_call(
        functools.partial(_mlstm_kernel, chunk=chunk, group=group, has_state=state is not None),
        grid=(b // group, nc),
        in_specs=in_specs,
        out_specs=[seq(MLSTM_WIDTH)] + state_specs,
        out_shape=[jax.ShapeDtypeStruct((b, t, MLSTM_WIDTH), F32)]
        + [jax.ShapeDtypeStruct((b,) + sh, F32) for sh in state_shapes],
        scratch_shapes=[
            pltpu.VMEM((group, 2 * MLSTM_HEADS, SUBLANES + chunk, LANES), F32),
            pltpu.VMEM((group, 2 * MLSTM_HEADS, chunk, LANES), F32),
        ] + [pltpu.VMEM((group,) + sh, F32) for sh in state_shapes[1:]],
        compiler_params=_cparams(("parallel", "arbitrary")),
        name="mlstm_state" if state is not None else "mlstm",
    )(*args)


def _outmlp_kernel(x_ref, att_ref, hb_ref, wo_ref, g2_ref, wu_ref, wd_ref, gf_ref, y_ref, *, ff_chunk):
    mix = (jnp.dot(att_ref[...].astype(BF16), wo_ref[0:ATT_WIDTH, :], preferred_element_type=F32)
           + jnp.dot(hb_ref[...].astype(BF16), wo_ref[ATT_WIDTH:D_MODEL, :], preferred_element_type=F32))
    x1 = x_ref[...] + mix
    h2 = _rms(x1, g2_ref[...]).astype(BF16)
    acc = x1
    for c in range(D_FF // ff_chunk):
        u = jnp.dot(h2, wu_ref[:, c * ff_chunk:(c + 1) * ff_chunk], preferred_element_type=F32)
        r = jnp.maximum(u, 0.0)
        acc = acc + jnp.dot((r * r).astype(BF16), wd_ref[c * ff_chunk:(c + 1) * ff_chunk, :],
                            preferred_element_type=F32)
    y_ref[...] = _rms(acc, gf_ref[...])


def _outmlp(x2d, att2d, hb2d, wo_bf, g2, wu_bf, wd_bf, gf, tm):
    n = x2d.shape[0]
    row = lambda w: pl.BlockSpec((tm, w), lambda i: (i, 0))
    return pl.pallas_call(
        functools.partial(_outmlp_kernel, ff_chunk=1024),
        grid=(n // tm,),
        in_specs=[row(D_MODEL), row(ATT_WIDTH), row(MLSTM_WIDTH),
                  _const_spec((D_MODEL, D_MODEL)), _const_spec((1, D_MODEL)),
                  _const_spec((D_MODEL, D_FF)), _const_spec((D_FF, D_MODEL)), _const_spec((1, D_MODEL))],
        out_specs=row(D_MODEL),
        out_shape=jax.ShapeDtypeStruct((n, D_MODEL), F32),
        compiler_params=_cparams(("parallel",)),
        name="outmlp",
    )(x2d, att2d, hb2d, wo_bf, g2, wu_bf, wd_bf, gf)


def _layer(x, pos, params, past, tm):
    (g1, w_in_bf, bg_pad, w_conv, b_conv, mh_g, wo_bf, g2, wu_bf, wd_bf, gf) = params
    b, t, _ = x.shape
    x2d = x.reshape(b * t, D_MODEL)
    tm = min(tm, b * t)
    cos_t, sa_t, sb_t = _rotary_tables(pos)
    if past is None:
        table_blocks = t // tm
    else:
        reps = tm // t
        cos_t, sa_t, sb_t = (jnp.tile(a, (reps, 1)) for a in (cos_t, sa_t, sb_t))
        table_blocks = 1
    q, k, v, raw, vb, ob, gates = _inproj(x2d, g1, w_in_bf, bg_pad, cos_t, sa_t, sb_t, tm, table_blocks)
    r3 = lambda a: a.reshape(b, t, a.shape[-1])
    if past is None:
        att = _attn_prompt(r3(q), r3(k), r3(v)).reshape(b * t, ATT_WIDTH)
        n_keep = min(DILATIONS[-1] * SUB_WINDOW, t)
        heads = lambda a: r3(a)[:, t - n_keep:].reshape(b, n_keep, N_ATT_HEADS, HD)
        new_k, new_v = heads(k), heads(v)
        hb, tail, c1, n1, m1 = _mlstm(r3(raw), r3(vb), r3(ob), r3(gates), w_conv, b_conv, mh_g,
                                      CHUNK if t % CHUNK == 0 else t, 1)
    else:
        win_k, win_v, conv_buf, c0, n0, m0 = past
        kc_t = win_k.transpose(0, 2, 3, 1)
        vc_t = win_v.transpose(0, 2, 3, 1)
        att, nk_t, nv_t = _attn_sample(q, k, v, kc_t, vc_t, t)
        new_k, new_v = nk_t.transpose(0, 3, 1, 2), nv_t.transpose(0, 3, 1, 2)
        tail0 = jnp.pad(conv_buf, ((0, 0), (SUBLANES - (CONV_W - 1), 0), (0, 0)))
        m0b = jnp.broadcast_to(m0[:, :, None], (b, MLSTM_HEADS, LANES))
        hb, tail, c1, n1, m1 = _mlstm(r3(raw), r3(vb), r3(ob), r3(gates), w_conv, b_conv, mh_g, t,
                                      MLSTM_SAMPLE_GROUP if b % MLSTM_SAMPLE_GROUP == 0 else 1,
                                      state=(tail0, c0, n0, m0b))
    new_conv = tail[:, SUBLANES - (CONV_W - 1):, :]
    y = _outmlp(x2d, att, hb.reshape(b * t, MLSTM_WIDTH), wo_bf, g2, wu_bf, wd_bf, gf, tm)
    return y.reshape(b, t, D_MODEL), (new_k, new_v, new_conv, c1, n1, m1[:, :, 0])


def kernel(x_prompt, x_sample, cache_win_k, cache_win_v, state_conv, state_C, state_n, state_m, norm1_g, w_in, b_gate, w_conv, b_conv, mh_norm_g, w_out, norm2_g, w_up, w_down, norm_f_g):
    depth = w_in.shape[0]
    assert depth == 1, "the final norm is fused into the (single) layer's MLP kernel"
    l = 0
    n_gate = 2 * MLSTM_HEADS
    w_l = w_in[l]
    w_in_bf = jnp.concatenate(
        [w_l[:, :IN_COLS - n_gate], jnp.pad(w_l[:, IN_COLS - n_gate:], ((0, 0), (0, GATE_PAD - n_gate)))],
        axis=1).astype(BF16)
    bg_pad = jnp.pad(b_gate[l], (0, GATE_PAD - n_gate))[None, :]
    params = (norm1_g[l][None, :], w_in_bf, bg_pad, w_conv[l], b_conv[l][None, :], mh_norm_g[l][None, :],
              w_out[l].astype(BF16), norm2_g[l][None, :], w_up[l].astype(BF16), w_down[l].astype(BF16),
              norm_f_g[None, :])
    pos_p = jnp.arange(x_prompt.shape[1])
    pos_s = PAST_LEN + jnp.arange(x_sample.shape[1])
    y_p, sp = _layer(x_prompt, pos_p, params, None, 512)
    past = (cache_win_k[l], cache_win_v[l], state_conv[l], state_C[l], state_n[l], state_m[l])
    y_s, ss = _layer(x_sample, pos_s, params, past, 512)
    return (y_p, y_s) + tuple(s[None] for s in sp) + tuple(s[None] for s in ss)
```

```python
import functools

import jax
import jax.numpy as jnp
import numpy as np
from jax import lax
from jax.experimental import pallas as pl
from jax.experimental.pallas import tpu as pltpu

D_MODEL = 1024
HD = 64
N_ATT_HEADS = 8
ATT_WIDTH = N_ATT_HEADS * HD
MLSTM_HEADS = 4
MLSTM_WIDTH = D_MODEL - ATT_WIDTH
MLSTM_HD = MLSTM_WIDTH // MLSTM_HEADS
ROT_DIM = HD // 4
ROT_HALF = ROT_DIM // 2
ROPE_THETA = 500000.0
DILATIONS = (1, 4, 16)
SUB_WINDOW = 128
ATT_BLOCK = 128
ATT_UNROLL = 4
MLSTM_SAMPLE_GROUP = 8
FF_PARTS = 4
ATT_HEAD_SPLIT = 2
CONV_W = 4
CHUNK = 128
D_FF = 4 * D_MODEL
EPS = 1e-6
PAST_LEN = 8192
IN_SIZES = (ATT_WIDTH, ATT_WIDTH, ATT_WIDTH, 2 * MLSTM_WIDTH, MLSTM_WIDTH, MLSTM_WIDTH, 2 * MLSTM_HEADS)
IN_COLS = sum(IN_SIZES)

LANES = 128
SUBLANES = 8
GATE_PAD = LANES
IN_COLS_PAD = IN_COLS - 2 * MLSTM_HEADS + GATE_PAD
VMEM_LIMIT = 56 * 1024 * 1024

F32 = jnp.float32
BF16 = jnp.bfloat16
NEG_INF = float("-inf")


def _cparams(sem):
    return pltpu.CompilerParams(dimension_semantics=sem, vmem_limit_bytes=VMEM_LIMIT)


def _const_spec(shape):
    nd = len(shape)
    return pl.BlockSpec(shape, lambda *_: (0,) * nd, pipeline_mode=pl.Buffered(1))


def _rms(x, g):
    return x * lax.rsqrt(jnp.mean(x * x, axis=-1, keepdims=True) + EPS) * g


def _inproj_kernel(x_ref, g_ref, w_ref, bg_ref, cos_ref, sa_ref, sb_ref,
                   q_ref, k_ref, v_ref, raw_ref, vb_ref, ob_ref, gate_ref):
    h = _rms(x_ref[...], g_ref[...]).astype(BF16)

    def proj(lo, width):
        return jnp.dot(h, w_ref[:, lo:lo + width], preferred_element_type=F32)

    cos, sa, sb = cos_ref[...], sa_ref[...], sb_ref[...]

    def rotary_store(dst, y, scale):
        for c in range(ATT_WIDTH // LANES):
            yc = y[:, c * LANES:(c + 1) * LANES]
            up = pltpu.roll(yc, LANES - ROT_HALF, 1)
            dn = pltpu.roll(yc, ROT_HALF, 1)
            r = yc * cos + up * sa + dn * sb
            dst[:, c * LANES:(c + 1) * LANES] = r * scale if scale != 1.0 else r

    off = 0
    rotary_store(q_ref, proj(off, ATT_WIDTH), HD ** -0.5)
    off += ATT_WIDTH
    rotary_store(k_ref, proj(off, ATT_WIDTH), 1.0)
    off += ATT_WIDTH
    v_ref[...] = proj(off, ATT_WIDTH)
    off += ATT_WIDTH
    raw_ref[...] = proj(off, 2 * MLSTM_WIDTH)
    off += 2 * MLSTM_WIDTH
    vb_ref[...] = proj(off, MLSTM_WIDTH)
    off += MLSTM_WIDTH
    ob_ref[...] = proj(off, MLSTM_WIDTH)
    off += MLSTM_WIDTH
    gate_ref[...] = proj(off, GATE_PAD) + bg_ref[...]


def _inproj(x2d, g1, w_in_bf, bg_pad, cos_t, sa_t, sb_t, tm, table_blocks):
    n = x2d.shape[0]
    grid = (n // tm,)
    row = lambda i: (i, 0)
    tab = lambda i: (i % table_blocks, 0)
    widths = (ATT_WIDTH, ATT_WIDTH, ATT_WIDTH, 2 * MLSTM_WIDTH, MLSTM_WIDTH, MLSTM_WIDTH, GATE_PAD)
    return pl.pallas_call(
        _inproj_kernel,
        grid=grid,
        in_specs=[
            pl.BlockSpec((tm, D_MODEL), row),
            _const_spec((1, D_MODEL)),
            _const_spec((D_MODEL, IN_COLS_PAD)),
            _const_spec((1, GATE_PAD)),
            pl.BlockSpec((tm, LANES), tab),
            pl.BlockSpec((tm, LANES), tab),
            pl.BlockSpec((tm, LANES), tab),
        ],
        out_specs=[pl.BlockSpec((tm, w), row) for w in widths],
        out_shape=[jax.ShapeDtypeStruct((n, w), F32) for w in widths],
        compiler_params=_cparams(("parallel",)),
        name="inproj",
    )(x2d, g1, w_in_bf, bg_pad, cos_t, sa_t, sb_t)


def _rotary_tables(pos):
    inv_freq = ROPE_THETA ** (-jnp.arange(ROT_HALF, dtype=F32) / ROT_HALF)
    ang = pos.astype(F32)[:, None] * inv_freq[None, :]
    cos, sin = jnp.cos(ang), jnp.sin(ang)
    p = pos.shape[0]
    ones = jnp.ones((p, HD - ROT_DIM), F32)
    zeros = jnp.zeros((p, HD - ROT_DIM), F32)
    z8 = jnp.zeros((p, ROT_HALF), F32)
    cos_h = jnp.concatenate([cos, cos, ones], axis=1)
    sa_h = jnp.concatenate([-sin, z8, zeros], axis=1)
    sb_h = jnp.concatenate([z8, sin, zeros], axis=1)
    two = lambda t: jnp.concatenate([t, t], axis=1)
    return two(cos_h), two(sa_h), two(sb_h)


def _attn_prompt_kernel(bias_ref, q_ref, k_ref, v_ref, o_ref, num_ref, m_ref, den_ref, *, seq):
    lane = lax.broadcasted_iota(jnp.int32, (ATT_BLOCK, LANES), 1)
    head0 = lane < HD
    blocks_total = seq // ATT_BLOCK

    def block_scores(q, kk, vv, first):
        q2 = jnp.concatenate([jnp.where(head0, q, 0.0), jnp.where(head0, 0.0, q)], axis=0).astype(BF16)
        s = lax.dot_general(q2, kk.astype(BF16), (((1,), (1,)), ((), ())),
                            preferred_element_type=F32)
        s = s + bias_ref[first]
        mx = jnp.max(s, axis=1, keepdims=True)
        p = jnp.exp(s - mx)
        den = jnp.sum(p, axis=1, keepdims=True)
        o2 = jnp.dot(p.astype(BF16), vv.astype(BF16), preferred_element_type=F32)
        o = jnp.where(head0, o2[:ATT_BLOCK], o2[ATT_BLOCK:])
        mxb = jnp.where(head0, mx[:ATT_BLOCK], mx[ATT_BLOCK:])
        denb = jnp.where(head0, den[:ATT_BLOCK], den[ATT_BLOCK:])
        return o, mxb, denb

    for dil in DILATIONS:
        nb = blocks_total // dil
        span = ATT_BLOCK * dil

        def body(i, carry, dil=dil, nb=nb, span=span):
            results = []
            for u in range(ATT_UNROLL):
                idx = i * ATT_UNROLL + u
                r = idx // nb
                n = idx % nb
                start = r + n * span
                prev = r + jnp.maximum(n - 1, 0) * span
                first = jnp.where(n == 0, 1, 0)
                if dil == 1:
                    rows = pl.ds(pl.multiple_of(start, ATT_BLOCK), ATT_BLOCK)
                    prows = pl.ds(pl.multiple_of(prev, ATT_BLOCK), ATT_BLOCK)
                else:
                    rows = pl.ds(start, ATT_BLOCK, stride=dil)
                    prows = pl.ds(prev, ATT_BLOCK, stride=dil)
                q = q_ref[rows, :]
                kk = jnp.concatenate([k_ref[prows, :], k_ref[rows, :]], axis=0)
                vv = jnp.concatenate([v_ref[prows, :], v_ref[rows, :]], axis=0)
                results.append((rows, block_scores(q, kk, vv, first)))
            for rows, (o, mxb, denb) in results:
                if dil == 1:
                    num_ref[rows, :] = o
                    m_ref[rows, :] = mxb
                    den_ref[rows, :] = denb
                else:
                    m_old = m_ref[rows, :]
                    m_new = jnp.maximum(m_old, mxb)
                    a = jnp.exp(m_old - m_new)
                    b = jnp.exp(mxb - m_new)
                    num_ref[rows, :] = num_ref[rows, :] * a + o * b
                    den_ref[rows, :] = den_ref[rows, :] * a + denb * b
                    m_ref[rows, :] = m_new
            return carry

        lax.fori_loop(0, blocks_total // ATT_UNROLL, body, 0)

    def finish(i, carry):
        rows = pl.ds(pl.multiple_of(i * ATT_BLOCK, ATT_BLOCK), ATT_BLOCK)
        o_ref[rows, :] = num_ref[rows, :] / den_ref[rows, :]
        return carry

    lax.fori_loop(0, blocks_total, finish, 0)


def _band_bias():
    row = np.arange(2 * ATT_BLOCK)[:, None] % ATT_BLOCK
    ki = np.arange(2 * ATT_BLOCK)[None, :] - ATT_BLOCK
    rel = row - ki
    valid = (rel >= 0) & (rel <= SUB_WINDOW)
    b0 = np.where(valid, 0.0, NEG_INF).astype(np.float32)
    b1 = np.where(valid & (ki >= 0), 0.0, NEG_INF).astype(np.float32)
    return jnp.asarray(np.stack([b0, b1]))


def _attn_prompt(q, k, v):
    b, s, _ = q.shape
    assert s % (ATT_BLOCK * DILATIONS[-1]) == 0
    blk = pl.BlockSpec((None, s, LANES), lambda i, j: (i, 0, j))
    return pl.pallas_call(
        functools.partial(_attn_prompt_kernel, seq=s),
        grid=(b, ATT_WIDTH // LANES),
        in_specs=[_const_spec((2, 2 * ATT_BLOCK, 2 * ATT_BLOCK)), blk, blk, blk],
        out_specs=blk,
        out_shape=jax.ShapeDtypeStruct((b, s, ATT_WIDTH), F32),
        scratch_shapes=[pltpu.VMEM((s, LANES), F32)] * 3,
        compiler_params=_cparams(("parallel", "parallel")),
        name="attn_prompt",
    )(_band_bias(), q, k, v)


def _attn_sample_body(mw_ref, mn_ref, q_ref, kn_ref, vn_ref, kc_ref, vc_ref,
                      att_ref, nk_ref, nv_ref, *, win, t_new, heads):
    mult_w = mw_ref[...]
    mult_n = mn_ref[...]
    lane = lax.broadcasted_iota(jnp.int32, (HD, LANES), 1)
    keep = lane < LANES - t_new
    top = jnp.zeros((LANES - t_new, heads * HD), F32)
    knt_all = jnp.transpose(jnp.concatenate([top, kn_ref[...]], axis=0))
    vnt_all = jnp.transpose(jnp.concatenate([top, vn_ref[...]], axis=0))
    for h in range(heads):
        qh = q_ref[:, h * HD:(h + 1) * HD].astype(BF16)
        kc = kc_ref[h]
        vc = vc_ref[h]
        knt = knt_all[h * HD:(h + 1) * HD, :]
        vnt = vnt_all[h * HD:(h + 1) * HD, :]
        s_w = jnp.dot(qh, kc.astype(BF16), preferred_element_type=F32)
        s_n = jnp.dot(qh, knt.astype(BF16), preferred_element_type=F32)
        s_w = jnp.where(mult_w > 0, s_w, NEG_INF)
        s_n = jnp.where(mult_n > 0, s_n, NEG_INF)
        mx = jnp.maximum(jnp.max(s_w, axis=1, keepdims=True), jnp.max(s_n, axis=1, keepdims=True))
        p_w = mult_w * jnp.exp(s_w - mx)
        p_n = mult_n * jnp.exp(s_n - mx)
        den = jnp.sum(p_w, axis=1, keepdims=True) + jnp.sum(p_n, axis=1, keepdims=True)
        o = lax.dot_general(p_w.astype(BF16), vc.astype(BF16), (((1,), (1,)), ((), ())),
                            preferred_element_type=F32)
        o = o + lax.dot_general(p_n.astype(BF16), vnt.astype(BF16), (((1,), (1,)), ((), ())),
                                preferred_element_type=F32)
        att_ref[:, h * HD:(h + 1) * HD] = o / den
        for src, new, dst in ((kc, knt, nk_ref), (vc, vnt, nv_ref)):
            rolled = pltpu.roll(src, win - t_new, 1)
            dst[h, :, 0:win - LANES] = rolled[:, 0:win - LANES]
            dst[h, :, win - LANES:win] = jnp.where(keep, rolled[:, win - LANES:win], new)


def _attn_sample_kernel(*refs, win, t_new):
    _attn_sample_body(*refs, win=win, t_new=t_new, heads=N_ATT_HEADS)


def _attn_mlp_kernel(mw_ref, mn_ref, q_ref, kn_ref, vn_ref, kc_ref, vc_ref,
                     x_ref, attp_ref, hbp_ref, wo_ref, g2_ref, wu_ref, wd_ref, gf_ref,
                     att_ref, nk_ref, nv_ref, y_ref, h2_sc, acc_sc, *, win, t_new, heads):
    c = pl.program_id(0) % FF_PARTS

    @pl.when(c == 0)
    def _():
        mix = (jnp.dot(attp_ref[...].astype(BF16), wo_ref[0:ATT_WIDTH, :], preferred_element_type=F32)
               + jnp.dot(hbp_ref[...].astype(BF16), wo_ref[ATT_WIDTH:D_MODEL, :], preferred_element_type=F32))
        x1 = x_ref[...] + mix
        h2_sc[...] = _rms(x1, g2_ref[...]).astype(BF16)
        acc_sc[...] = x1

    u = jnp.dot(h2_sc[...], wu_ref[c], preferred_element_type=F32)
    r = jnp.maximum(u, 0.0)
    acc_sc[...] += jnp.dot((r * r).astype(BF16), wd_ref[c], preferred_element_type=F32)
    _attn_sample_body(mw_ref, mn_ref, q_ref, kn_ref, vn_ref, kc_ref, vc_ref, att_ref, nk_ref, nv_ref,
                      win=win, t_new=t_new, heads=heads)

    @pl.when(c == FF_PARTS - 1)
    def _():
        y_ref[...] = _rms(acc_sc[...], gf_ref[...])


def _sample_multiplicity(win, t_new):
    t = np.arange(t_new)[:, None]
    idx = np.arange(win + t_new)[None, :]
    back = win + t - idx
    mult = np.zeros((t_new, win + t_new), np.float32)
    for dil in DILATIONS:
        mult += ((back >= 0) & (back % dil == 0) & (back // dil <= SUB_WINDOW)).astype(np.float32)
    mw = mult[:, :win]
    mn = np.zeros((t_new, LANES), np.float32)
    mn[:, LANES - t_new:] = mult[:, win:]
    return jnp.asarray(mw), jnp.asarray(mn)


def _attn_sample(q, k_new, v_new, kc_t, vc_t, t_new):
    b, nh, _, win = kc_t.shape
    assert win == DILATIONS[-1] * SUB_WINDOW and t_new == SUBLANES and q.shape[0] == b * t_new
    mw, mn = _sample_multiplicity(win, t_new)
    rows = pl.BlockSpec((t_new, ATT_WIDTH), lambda i: (i, 0))
    per_b = lambda *shape: pl.BlockSpec((None,) + shape, lambda i: (i,) + (0,) * len(shape))
    return pl.pallas_call(
        functools.partial(_attn_sample_kernel, win=win, t_new=t_new),
        grid=(b,),
        in_specs=[
            _const_spec((t_new, win)), _const_spec((t_new, LANES)),
            rows, rows, rows, per_b(nh, HD, win), per_b(nh, HD, win),
        ],
        out_specs=[rows, per_b(nh, HD, win), per_b(nh, HD, win)],
        out_shape=[
            jax.ShapeDtypeStruct((b * t_new, ATT_WIDTH), F32),
            jax.ShapeDtypeStruct((b, nh, HD, win), F32),
            jax.ShapeDtypeStruct((b, nh, HD, win), F32),
        ],
        compiler_params=_cparams(("parallel",)),
        name="attn_sample",
    )(mw, mn, q, k_new, v_new, kc_t, vc_t)


def _attn_mlp_fits(b_s, n_p):
    steps = b_s * ATT_HEAD_SPLIT
    if steps % FF_PARTS:
        return False
    tiles = steps // FF_PARTS
    return n_p % tiles == 0 and (n_p // tiles) % SUBLANES == 0 and n_p // tiles <= 512


def _attn_mlp(q, k_new, v_new, kc_t, vc_t, t_new, x2d, attp, hbp, wo_bf, g2, wu_bf, wd_bf, gf):
    b, nh, _, win = kc_t.shape
    n_p = x2d.shape[0]
    assert win == DILATIONS[-1] * SUB_WINDOW and t_new == SUBLANES and q.shape[0] == b * t_new
    heads = nh // ATT_HEAD_SPLIT
    steps = b * ATT_HEAD_SPLIT
    tm = n_p // (steps // FF_PARTS)
    ffc = D_FF // FF_PARTS
    mw, mn = _sample_multiplicity(win, t_new)
    wu_parts = wu_bf.reshape(D_MODEL, FF_PARTS, ffc).transpose(1, 0, 2)
    wd_parts = wd_bf.reshape(FF_PARTS, ffc, D_MODEL)
    rows = pl.BlockSpec((t_new, heads * HD), lambda i: (i // ATT_HEAD_SPLIT, i % ATT_HEAD_SPLIT))
    win_blk = pl.BlockSpec((None, heads, HD, win), lambda i: (i // ATT_HEAD_SPLIT, i % ATT_HEAD_SPLIT, 0, 0))
    tile = lambda w: pl.BlockSpec((tm, w), lambda i: (i // FF_PARTS, 0))
    return pl.pallas_call(
        functools.partial(_attn_mlp_kernel, win=win, t_new=t_new, heads=heads),
        grid=(steps,),
        in_specs=[
            _const_spec((t_new, win)), _const_spec((t_new, LANES)),
            rows, rows, rows, win_blk, win_blk,
            tile(D_MODEL), tile(ATT_WIDTH), tile(MLSTM_WIDTH),
            _const_spec((D_MODEL, D_MODEL)), _const_spec((1, D_MODEL)),
            _const_spec((FF_PARTS, D_MODEL, ffc)), _const_spec((FF_PARTS, ffc, D_MODEL)),
            _const_spec((1, D_MODEL)),
        ],
        out_specs=[rows, win_blk, win_blk, tile(D_MODEL)],
        out_shape=[
            jax.ShapeDtypeStruct((b * t_new, ATT_WIDTH), F32),
            jax.ShapeDtypeStruct((b, nh, HD, win), F32),
            jax.ShapeDtypeStruct((b, nh, HD, win), F32),
            jax.ShapeDtypeStruct((n_p, D_MODEL), F32),
        ],
        scratch_shapes=[pltpu.VMEM((tm, D_MODEL), BF16), pltpu.VMEM((tm, D_MODEL), F32)],
        compiler_params=_cparams(("arbitrary",)),
        name="attn_mlp",
    )(mw, mn, q, k_new, v_new, kc_t, vc_t, x2d, attp, hbp, wo_bf, g2, wu_parts, wd_parts, gf)


def _split_dot(tri, x):
    hi = x.astype(BF16)
    lo = (x - hi.astype(F32)).astype(BF16)
    return (jnp.dot(tri, hi, preferred_element_type=F32)
            + jnp.dot(tri, lo, preferred_element_type=F32))


def _split_dot_rows(x, tri):
    hi = x.astype(BF16)
    lo = (x - hi.astype(F32)).astype(BF16)
    return (jnp.dot(hi, tri, preferred_element_type=F32)
            + jnp.dot(lo, tri, preferred_element_type=F32))


def _log_sigmoid(x):
    return jnp.minimum(x, 0.0) - jnp.log1p(jnp.exp(-jnp.abs(x)))


def _mlstm_kernel(*refs, chunk, group, has_state):
    if has_state:
        (raw_ref, vb_ref, ob_ref, gate_ref, wc_ref, bc_ref, g_ref, tail_ref, c0_ref, n0_ref, m0_ref,
         hb_ref, tail_out_ref, c_out_ref, n_out_ref, m_out_ref, xp_ref, act_ref, c_sc, n_sc, m_sc) = refs
    else:
        (raw_ref, vb_ref, ob_ref, gate_ref, wc_ref, bc_ref, g_ref,
         hb_ref, tail_out_ref, c_out_ref, n_out_ref, m_out_ref, xp_ref, act_ref, c_sc, n_sc, m_sc) = refs
    L = chunk
    c_idx = pl.program_id(1)

    @pl.when(c_idx == 0)
    def _():
        if has_state:
            for c in range(2 * MLSTM_HEADS):
                xp_ref[:, c, 0:SUBLANES, :] = tail_ref[:, :, c * LANES:(c + 1) * LANES]
            c_sc[...] = c0_ref[...]
            n_sc[...] = n0_ref[...]
            m_sc[...] = m0_ref[...]
        else:
            xp_ref[:, :, 0:SUBLANES, :] = jnp.zeros((group, 2 * MLSTM_HEADS, SUBLANES, LANES), F32)
            c_sc[...] = jnp.zeros_like(c_sc)
            n_sc[...] = jnp.zeros_like(n_sc)
            m_sc[...] = jnp.zeros_like(m_sc)

    r_i = lax.broadcasted_iota(jnp.int32, (L, L), 0)
    c_i = lax.broadcasted_iota(jnp.int32, (L, L), 1)
    causal = r_i >= c_i
    tri_l = jnp.where(causal, 1.0, 0.0).astype(BF16)
    tri_u = jnp.where(r_i <= c_i, 1.0, 0.0).astype(BF16)
    first_tap = SUBLANES - (CONV_W - 1)
    bc = bc_ref[...]
    taps = [wc_ref[w:w + 1, :] for w in range(CONV_W)]

    for g in range(group):
        tail_out_ref[g] = raw_ref[g, L - SUBLANES:L, :]
        for c in range(2 * MLSTM_HEADS):
            cl = slice(c * LANES, (c + 1) * LANES)
            raw = raw_ref[g, :, cl]
            xp_ref[g, c, SUBLANES:SUBLANES + L, :] = raw
            if L % (SUBLANES * SUBLANES) == 0:
                n_rows = L // SUBLANES
                slabs = [xp_ref[g, c, pl.ds(first_tap + j, n_rows, stride=SUBLANES), :]
                         for j in range(SUBLANES + CONV_W - 1)]
                for s in range(SUBLANES):
                    conv = bc[:, cl]
                    for w in range(CONV_W):
                        conv = conv + slabs[s + w] * taps[w][:, cl]
                    act_ref[g, c, pl.ds(s, n_rows, stride=SUBLANES), :] = conv * jax.nn.sigmoid(conv)
            else:
                conv = bc[:, cl]
                for w in range(CONV_W):
                    conv = conv + xp_ref[g, c, first_tap + w:first_tap + w + L, :] * taps[w][:, cl]
                act_ref[g, c] = conv * jax.nn.sigmoid(conv)
            xp_ref[g, c, 0:SUBLANES, :] = raw[L - SUBLANES:L, :]

        gates = gate_ref[g]
        lf = _log_sigmoid(gates)
        b_col = _split_dot(tri_l, lf)
        if L == LANES:
            gates_sq = gates
        else:
            gates_sq = jnp.concatenate([gates, jnp.zeros((LANES - L, LANES), F32)], axis=0)
        gates_t = jnp.transpose(gates_sq)[0:SUBLANES, 0:L]
        lf_t = _log_sigmoid(gates_t)
        b_row = _split_dot_rows(lf_t, tri_u)

        for h in range(MLSTM_HEADS):
            sl = slice(h * MLSTM_HD, (h + 1) * MLSTM_HD)
            q = act_ref[g, h]
            k = act_ref[g, MLSTM_HEADS + h] * (MLSTM_HD ** -0.5)
            v = vb_ref[g, :, sl]
            qb, kb, vb = q.astype(BF16), k.astype(BF16), v.astype(BF16)
            ig_c = gates[:, h:h + 1]
            b_c = b_col[:, MLSTM_HEADS + h:MLSTM_HEADS + h + 1]
            ig_r = gates_t[h:h + 1, :]
            b_r = b_row[MLSTM_HEADS + h:MLSTM_HEADS + h + 1, :]
            m_prev = m_sc[g, h:h + 1, 0:1]
            a_c = b_c + m_prev
            dm = jnp.where(causal, b_c - b_r + ig_r, NEG_INF)
            mt = jnp.maximum(a_c, jnp.max(dm, axis=1, keepdims=True))
            w_inter = jnp.exp(a_c - mt)
            wm = jnp.exp(dm - mt)
            sc = lax.dot_general(qb, kb, (((1,), (1,)), ((), ())), preferred_element_type=F32) * wm
            c_old = c_sc[g, h]
            n_old = n_sc[g, h:h + 1, :]
            num = (w_inter * jnp.dot(qb, c_old.astype(BF16), preferred_element_type=F32)
                   + jnp.dot(sc.astype(BF16), vb, preferred_element_type=F32))
            den = (w_inter * jnp.sum(q * n_old, axis=1, keepdims=True)
                   + jnp.sum(sc, axis=1, keepdims=True))
            hh = num / jnp.maximum(jnp.abs(den), jnp.exp(-mt))
            m_last = mt[L - 1:L, :]
            w_last = jnp.exp(a_c[L - 1:L, :] - m_last)
            w_t = jnp.exp(b_c[L - 1:L, :] - b_c + ig_c - m_last)
            kw = k * w_t
            if L == LANES:
                kw_sq, v_sq = kw, v
            else:
                pad = jnp.zeros((LANES - L, MLSTM_HD), F32)
                kw_sq = jnp.concatenate([kw, pad], axis=0)
                v_sq = jnp.concatenate([v, pad], axis=0)
            c_new = w_last * c_old + jnp.dot(jnp.transpose(kw_sq).astype(BF16), v_sq.astype(BF16),
                                             preferred_element_type=F32)
            n_new = w_last * n_old + jnp.sum(kw, axis=0, keepdims=True)
            c_sc[g, h] = c_new
            n_sc[g, h:h + 1, :] = n_new
            m_sc[g, h:h + 1, :] = jnp.broadcast_to(m_last, (1, LANES))
            hn = hh * lax.rsqrt(jnp.mean(hh * hh, axis=1, keepdims=True) + EPS)
            hb_ref[g, :, sl] = hn * g_ref[:, sl] * jax.nn.sigmoid(ob_ref[g, :, sl])

    @pl.when(c_idx == pl.num_programs(1) - 1)
    def _():
        c_out_ref[...] = c_sc[...]
        n_out_ref[...] = n_sc[...]
        m_out_ref[...] = m_sc[...]


def _mlstm(raw, vb, ob, gates, w_conv, b_conv, mh_g, chunk, group, state=None):
    b, t, _ = raw.shape
    nc = t // chunk
    assert nc * chunk == t and chunk % SUBLANES == 0 and b % group == 0
    seq = lambda w: pl.BlockSpec((group, chunk, w), lambda i, j: (i, j, 0))
    per_b = lambda *shape: pl.BlockSpec((group,) + shape, lambda i, j: (i,) + (0,) * len(shape))
    in_specs = [seq(2 * MLSTM_WIDTH), seq(MLSTM_WIDTH), seq(MLSTM_WIDTH), seq(GATE_PAD),
                _const_spec((CONV_W, 2 * MLSTM_WIDTH)), _const_spec((1, 2 * MLSTM_WIDTH)),
                _const_spec((1, MLSTM_WIDTH))]
    args = [raw, vb, ob, gates, w_conv, b_conv, mh_g]
    state_shapes = [(SUBLANES, 2 * MLSTM_WIDTH), (MLSTM_HEADS, MLSTM_HD, MLSTM_HD),
                    (MLSTM_HEADS, MLSTM_HD), (MLSTM_HEADS, LANES)]
    state_specs = [per_b(*sh) for sh in state_shapes]
    if state is not None:
        in_specs += state_specs
        args += list(state)
    return pl.pallas_call(
        functools.partial(_mlstm_kernel, chunk=chunk, group=group, has_state=state is not None),
        grid=(b // group, nc),
        in_specs=in_specs,
        out_specs=[seq(MLSTM_WIDTH)] + state_specs,
        out_shape=[jax.ShapeDtypeStruct((b, t, MLSTM_WIDTH), F32)]
        + [jax.ShapeDtypeStruct((b,) + sh, F32) for sh in state_shapes],
        scratch_shapes=[
            pltpu.VMEM((group, 2 * MLSTM_HEADS, SUBLANES + chunk, LANES), F32),
            pltpu.VMEM((group, 2 * MLSTM_HEADS, chunk, LANES), F32),
        ] + [pltpu.VMEM((group,) + sh, F32) for sh in state_shapes[1:]],
        compiler_params=_cparams(("parallel", "arbitrary")),
        name="mlstm_state" if state is not None else "mlstm",
    )(*args)


def _outmlp_kernel(x_ref, att_ref, hb_ref, wo_ref, g2_ref, wu_ref, wd_ref, gf_ref, y_ref, *, ff_chunk):
    mix = (jnp.dot(att_ref[...].astype(BF16), wo_ref[0:ATT_WIDTH, :], preferred_element_type=F32)
           + jnp.dot(hb_ref[...].astype(BF16), wo_ref[ATT_WIDTH:D_MODEL, :], preferred_element_type=F32))
    x1 = x_ref[...] + mix
    h2 = _rms(x1, g2_ref[...]).astype(BF16)
    acc = x1
    for c in range(D_FF // ff_chunk):
        u = jnp.dot(h2, wu_ref[:, c * ff_chunk:(c + 1) * ff_chunk], preferred_element_type=F32)
        r = jnp.maximum(u, 0.0)
        acc = acc + jnp.dot((r * r).astype(BF16), wd_ref[c * ff_chunk:(c + 1) * ff_chunk, :],
                            preferred_element_type=F32)
    y_ref[...] = _rms(acc, gf_ref[...])


def _outmlp(x2d, att2d, hb2d, wo_bf, g2, wu_bf, wd_bf, gf, tm):
    n = x2d.shape[0]
    row = lambda w: pl.BlockSpec((tm, w), lambda i: (i, 0))
    return pl.pallas_call(
        functools.partial(_outmlp_kernel, ff_chunk=1024),
        grid=(n // tm,),
        in_specs=[row(D_MODEL), row(ATT_WIDTH), row(MLSTM_WIDTH),
                  _const_spec((D_MODEL, D_MODEL)), _const_spec((1, D_MODEL)),
                  _const_spec((D_MODEL, D_FF)), _const_spec((D_FF, D_MODEL)), _const_spec((1, D_MODEL))],
        out_specs=row(D_MODEL),
        out_shape=jax.ShapeDtypeStruct((n, D_MODEL), F32),
        compiler_params=_cparams(("parallel",)),
        name="outmlp",
    )(x2d, att2d, hb2d, wo_bf, g2, wu_bf, wd_bf, gf)


def _project(x, pos, mix_params, tm, tile_tables):
    g1, w_in_bf, bg_pad = mix_params
    b, t, _ = x.shape
    x2d = x.reshape(b * t, D_MODEL)
    tm = min(tm, b * t)
    cos_t, sa_t, sb_t = _rotary_tables(pos)
    if tile_tables:
        cos_t, sa_t, sb_t = (jnp.tile(a, (tm // t, 1)) for a in (cos_t, sa_t, sb_t))
        table_blocks = 1
    else:
        table_blocks = t // tm
    return x2d, tm, _inproj(x2d, g1, w_in_bf, bg_pad, cos_t, sa_t, sb_t, tm, table_blocks)


def kernel(x_prompt, x_sample, cache_win_k, cache_win_v, state_conv, state_C, state_n, state_m, norm1_g, w_in, b_gate, w_conv, b_conv, mh_norm_g, w_out, norm2_g, w_up, w_down, norm_f_g):
    depth = w_in.shape[0]
    assert depth == 1, "the final norm is fused into the (single) layer's MLP kernel"
    l = 0
    n_gate = 2 * MLSTM_HEADS
    w_l = w_in[l]
    w_in_bf = jnp.concatenate(
        [w_l[:, :IN_COLS - n_gate], jnp.pad(w_l[:, IN_COLS - n_gate:], ((0, 0), (0, GATE_PAD - n_gate)))],
        axis=1).astype(BF16)
    bg_pad = jnp.pad(b_gate[l], (0, GATE_PAD - n_gate))[None, :]
    mix_params = (norm1_g[l][None, :], w_in_bf, bg_pad)
    conv_params = (w_conv[l], b_conv[l][None, :], mh_norm_g[l][None, :])
    mlp_params = (w_out[l].astype(BF16), norm2_g[l][None, :], w_up[l].astype(BF16), w_down[l].astype(BF16),
                  norm_f_g[None, :])

    bp, tp, _ = x_prompt.shape
    xp2d, tm_p, (q, k, v, raw, vb, ob, gates) = _project(x_prompt, jnp.arange(tp), mix_params, 512, False)
    r3 = lambda a: a.reshape(bp, tp, a.shape[-1])
    att_p = _attn_prompt(r3(q), r3(k), r3(v)).reshape(bp * tp, ATT_WIDTH)
    n_keep = min(DILATIONS[-1] * SUB_WINDOW, tp)
    heads = lambda a: r3(a)[:, tp - n_keep:].reshape(bp, n_keep, N_ATT_HEADS, HD)
    pk, pv = heads(k), heads(v)
    hb_p, tail_p, pc, pn, pm = _mlstm(r3(raw), r3(vb), r3(ob), r3(gates), *conv_params,
                                      CHUNK if tp % CHUNK == 0 else tp, 1)
    hb_p = hb_p.reshape(bp * tp, MLSTM_WIDTH)

    bs, ts, _ = x_sample.shape
    xs2d, tm_s, (q, k, v, raw, vb, ob, gates) = _project(x_sample, PAST_LEN + jnp.arange(ts), mix_params, 512, True)
    r3 = lambda a: a.reshape(bs, ts, a.shape[-1])
    tail0 = jnp.pad(state_conv[l], ((0, 0), (SUBLANES - (CONV_W - 1), 0), (0, 0)))
    m0b = jnp.broadcast_to(state_m[l][:, :, None], (bs, MLSTM_HEADS, LANES))
    hb_s, tail_s, sc, sn, sm = _mlstm(r3(raw), r3(vb), r3(ob), r3(gates), *conv_params, ts,
                                      MLSTM_SAMPLE_GROUP if bs % MLSTM_SAMPLE_GROUP == 0 else 1,
                                      state=(tail0, state_C[l], state_n[l], m0b))
    hb_s = hb_s.reshape(bs * ts, MLSTM_WIDTH)

    kc_t = cache_win_k[l].transpose(0, 2, 3, 1)
    vc_t = cache_win_v[l].transpose(0, 2, 3, 1)
    if _attn_mlp_fits(bs, bp * tp):
        att_s, nk_t, nv_t, y_p = _attn_mlp(q, k, v, kc_t, vc_t, ts, xp2d, att_p, hb_p, *mlp_params)
    else:
        att_s, nk_t, nv_t = _attn_sample(q, k, v, kc_t, vc_t, ts)
        y_p = _outmlp(xp2d, att_p, hb_p, *mlp_params, tm_p)
    sk, sv = nk_t.transpose(0, 3, 1, 2), nv_t.transpose(0, 3, 1, 2)
    y_s = _outmlp(xs2d, att_s, hb_s, *mlp_params, tm_s)

    first_tail = SUBLANES - (CONV_W - 1)
    outs = (y_p.reshape(bp, tp, D_MODEL), y_s.reshape(bs, ts, D_MODEL),
            pk, pv, tail_p[:, first_tail:, :], pc, pn, pm[:, :, 0],
            sk, sv, tail_s[:, first_tail:, :], sc, sn, sm[:, :, 0])
    return outs[:2] + tuple(o[None] for o in outs[2:])
```

```python
import functools

import jax
import jax.numpy as jnp
import numpy as np
from jax import lax
from jax.experimental import pallas as pl
from jax.experimental.pallas import tpu as pltpu

D_MODEL = 1024
HD = 64
N_ATT_HEADS = 8
ATT_WIDTH = N_ATT_HEADS * HD
MLSTM_HEADS = 4
MLSTM_WIDTH = D_MODEL - ATT_WIDTH
MLSTM_HD = MLSTM_WIDTH // MLSTM_HEADS
ROT_DIM = HD // 4
ROT_HALF = ROT_DIM // 2
ROPE_THETA = 500000.0
DILATIONS = (1, 4, 16)
SUB_WINDOW = 128
ATT_BLOCK = 128
ATT_UNROLL = 8
MLSTM_SAMPLE_GROUP = 8
FF_PARTS = 4
ATT_HEAD_SPLIT = 2
CONV_W = 4
CHUNK = 128
D_FF = 4 * D_MODEL
EPS = 1e-6
PAST_LEN = 8192
IN_SIZES = (ATT_WIDTH, ATT_WIDTH, ATT_WIDTH, 2 * MLSTM_WIDTH, MLSTM_WIDTH, MLSTM_WIDTH, 2 * MLSTM_HEADS)
IN_COLS = sum(IN_SIZES)

LANES = 128
SUBLANES = 8
GATE_PAD = LANES
VMEM_LIMIT = 56 * 1024 * 1024

F32 = jnp.float32
BF16 = jnp.bfloat16
NEG_INF = float("-inf")
LOG2E = 1.4426950408889634


def _cparams(sem):
    return pltpu.CompilerParams(dimension_semantics=sem, vmem_limit_bytes=VMEM_LIMIT)


def _const_spec(shape):
    nd = len(shape)
    return pl.BlockSpec(shape, lambda *_: (0,) * nd, pipeline_mode=pl.Buffered(1))


def _rms(x, g):
    return x * lax.rsqrt(jnp.mean(x * x, axis=-1, keepdims=True) + EPS) * g


def _inproj_kernel(x_ref, g_ref, w_ref, wg_ref, bg_ref, cos_ref, sa_ref, sb_ref,
                   q_ref, k_ref, v_ref, raw_ref, vb_ref, ob_ref, gate_ref):
    h = _rms(x_ref[...], g_ref[...]).astype(BF16)

    def proj(lo, width):
        return jnp.dot(h, w_ref[:, lo:lo + width], preferred_element_type=F32)

    cos, sa, sb = cos_ref[...], sa_ref[...], sb_ref[...]

    def rotary_store(dst, y, scale):
        for c in range(ATT_WIDTH // LANES):
            yc = y[:, c * LANES:(c + 1) * LANES]
            up = pltpu.roll(yc, LANES - ROT_HALF, 1)
            dn = pltpu.roll(yc, ROT_HALF, 1)
            r = yc * cos + up * sa + dn * sb
            dst[:, c * LANES:(c + 1) * LANES] = r * scale if scale != 1.0 else r

    off = 0
    rotary_store(q_ref, proj(off, ATT_WIDTH), HD ** -0.5)
    off += ATT_WIDTH
    rotary_store(k_ref, proj(off, ATT_WIDTH), 1.0)
    off += ATT_WIDTH
    v_ref[...] = proj(off, ATT_WIDTH)
    off += ATT_WIDTH
    raw_ref[...] = proj(off, 2 * MLSTM_WIDTH)
    off += 2 * MLSTM_WIDTH
    vb_ref[...] = proj(off, MLSTM_WIDTH)
    off += MLSTM_WIDTH
    ob_ref[...] = proj(off, MLSTM_WIDTH)
    gate_ref[...] = jnp.dot(h, wg_ref[...], preferred_element_type=F32) + bg_ref[...]


def _inproj(x2d, g1, w_main_bf, w_gate_bf, bg_pad, cos_t, sa_t, sb_t, tm, table_blocks):
    n = x2d.shape[0]
    grid = (n // tm,)
    row = lambda i: (i, 0)
    tab = lambda i: (i % table_blocks, 0)
    widths = (ATT_WIDTH, ATT_WIDTH, ATT_WIDTH, 2 * MLSTM_WIDTH, MLSTM_WIDTH, MLSTM_WIDTH, GATE_PAD)
    return pl.pallas_call(
        _inproj_kernel,
        grid=grid,
        in_specs=[
            pl.BlockSpec((tm, D_MODEL), row),
            _const_spec((1, D_MODEL)),
            _const_spec((D_MODEL, IN_COLS - 2 * MLSTM_HEADS)),
            _const_spec((D_MODEL, GATE_PAD)),
            _const_spec((1, GATE_PAD)),
            pl.BlockSpec((tm, LANES), tab),
            pl.BlockSpec((tm, LANES), tab),
            pl.BlockSpec((tm, LANES), tab),
        ],
        out_specs=[pl.BlockSpec((tm, w), row) for w in widths],
        out_shape=[jax.ShapeDtypeStruct((n, w), F32) for w in widths],
        compiler_params=_cparams(("parallel",)),
        name="inproj",
    )(x2d, g1, w_main_bf, w_gate_bf, bg_pad, cos_t, sa_t, sb_t)


def _rotary_tables(pos):
    inv_freq = ROPE_THETA ** (-jnp.arange(ROT_HALF, dtype=F32) / ROT_HALF)
    ang = pos.astype(F32)[:, None] * inv_freq[None, :]
    cos, sin = jnp.cos(ang), jnp.sin(ang)
    p = pos.shape[0]
    ones = jnp.ones((p, HD - ROT_DIM), F32)
    zeros = jnp.zeros((p, HD - ROT_DIM), F32)
    z8 = jnp.zeros((p, ROT_HALF), F32)
    cos_h = jnp.concatenate([cos, cos, ones], axis=1)
    sa_h = jnp.concatenate([-sin, z8, zeros], axis=1)
    sb_h = jnp.concatenate([z8, sin, zeros], axis=1)
    two = lambda t: jnp.concatenate([t, t], axis=1)
    return two(cos_h), two(sa_h), two(sb_h)


def _attn_prompt_kernel(bias_ref, q_ref, k_ref, v_ref, o_ref,
                        q4_ref, k4_ref, v4_ref, num1_ref, m1_ref, den1_ref, num4_ref, m4_ref, den4_ref,
                        *, seq):
    d_mid, d_out = DILATIONS[1], DILATIONS[2]
    d_in = d_out // d_mid
    lane = lax.broadcasted_iota(jnp.int32, (ATT_BLOCK, LANES), 1)
    head0 = lane < HD
    nblk = seq // ATT_BLOCK
    sub_blocks = nblk // d_mid
    ones_cols = jnp.ones((2 * ATT_BLOCK, LANES), BF16)

    def block_scores(q, kk, vv, first):
        q = q * LOG2E
        q2 = jnp.concatenate([jnp.where(head0, q, 0.0), jnp.where(head0, 0.0, q)], axis=0).astype(BF16)
        s = lax.dot_general(q2, kk.astype(BF16), (((1,), (1,)), ((), ())),
                            preferred_element_type=F32)
        s = s + bias_ref[first]
        mx = jnp.max(s, axis=1, keepdims=True)
        p = jnp.exp2(s - mx)
        o2 = jnp.dot(p.astype(BF16), jnp.concatenate([vv.astype(BF16), ones_cols], axis=1),
                     preferred_element_type=F32)
        o = jnp.where(head0, o2[:ATT_BLOCK, :LANES], o2[ATT_BLOCK:, :LANES])
        den = jnp.where(head0, o2[:ATT_BLOCK, LANES:], o2[ATT_BLOCK:, LANES:])
        mxb = jnp.where(head0, mx[:ATT_BLOCK], mx[ATT_BLOCK:])
        return o, mxb, den

    def aligned(start):
        return pl.ds(pl.multiple_of(start, ATT_BLOCK), ATT_BLOCK)

    def split(i, carry):
        r, c = i // sub_blocks, i % sub_blocks
        src = pl.ds(r + c * ATT_BLOCK * d_mid, ATT_BLOCK, stride=d_mid)
        dst = aligned(i * ATT_BLOCK)
        q4_ref[dst, :] = q_ref[src, :]
        k4_ref[dst, :] = k_ref[src, :]
        v4_ref[dst, :] = v_ref[src, :]
        return carry

    lax.fori_loop(0, nblk, split, 0)

    def run_group(srcs, blocks_per_class, rows_of, merge):
        qs, ks, vs = srcs

        def body(i, carry):
            results = []
            for u in range(ATT_UNROLL):
                idx = i * ATT_UNROLL + u
                n = idx % blocks_per_class
                rows = rows_of(idx, n)
                prows = rows_of(idx - jnp.minimum(n, 1), jnp.maximum(n - 1, 0))
                first = jnp.where(n == 0, 1, 0)
                kk = jnp.concatenate([ks[prows, :], ks[rows, :]], axis=0)
                vv = jnp.concatenate([vs[prows, :], vs[rows, :]], axis=0)
                results.append((rows, block_scores(qs[rows, :], kk, vv, first)))
            for rows, (o, mxb, den) in results:
                merge(rows, o, mxb, den)
            return carry

        lax.fori_loop(0, nblk // ATT_UNROLL, body, 0)

    def init_state(num_ref, m_ref, den_ref):
        def merge(rows, o, mxb, den):
            num_ref[rows, :] = o
            m_ref[rows, :] = mxb
            den_ref[rows, :] = den
        return merge

    def merge4(rows, o, mxb, den):
        m_old = m4_ref[rows, :]
        m_new = jnp.maximum(m_old, mxb)
        a = jnp.exp2(m_old - m_new)
        b = jnp.exp2(mxb - m_new)
        num4_ref[rows, :] = num4_ref[rows, :] * a + o * b
        den4_ref[rows, :] = den4_ref[rows, :] * a + den * b
        m4_ref[rows, :] = m_new

    run_group((q_ref, k_ref, v_ref), nblk, lambda idx, n: aligned(idx * ATT_BLOCK),
              init_state(num1_ref, m1_ref, den1_ref))
    run_group((q4_ref, k4_ref, v4_ref), sub_blocks, lambda idx, n: aligned(idx * ATT_BLOCK),
              init_state(num4_ref, m4_ref, den4_ref))
    blocks16 = nblk // d_out

    def rows16(idx, n):
        cls = idx // blocks16
        r, j = cls // d_in, cls % d_in
        return pl.ds(r * (sub_blocks * ATT_BLOCK) + j + n * ATT_BLOCK * d_in, ATT_BLOCK, stride=d_in)

    run_group((q4_ref, k4_ref, v4_ref), blocks16, rows16, merge4)

    def finish(i, carry):
        r, c = i // sub_blocks, i % sub_blocks
        tok = pl.ds(r + c * ATT_BLOCK * d_mid, ATT_BLOCK, stride=d_mid)
        res = aligned(i * ATT_BLOCK)
        m_a, m_b = m4_ref[res, :], m1_ref[tok, :]
        m_new = jnp.maximum(m_a, m_b)
        wa = jnp.exp2(m_a - m_new)
        wb = jnp.exp2(m_b - m_new)
        num = num4_ref[res, :] * wa + num1_ref[tok, :] * wb
        den = den4_ref[res, :] * wa + den1_ref[tok, :] * wb
        o_ref[tok, :] = num / den
        return carry

    lax.fori_loop(0, nblk, finish, 0)


def _band_bias():
    row = np.arange(2 * ATT_BLOCK)[:, None] % ATT_BLOCK
    ki = np.arange(2 * ATT_BLOCK)[None, :] - ATT_BLOCK
    rel = row - ki
    valid = (rel >= 0) & (rel <= SUB_WINDOW)
    b0 = np.where(valid, 0.0, NEG_INF).astype(np.float32)
    b1 = np.where(valid & (ki >= 0), 0.0, NEG_INF).astype(np.float32)
    return jnp.asarray(np.stack([b0, b1]))


def _attn_prompt(q, k, v):
    b, s, _ = q.shape
    assert DILATIONS[0] == 1 and DILATIONS[2] % DILATIONS[1] == 0
    assert s % (ATT_BLOCK * DILATIONS[-1]) == 0 and (s // ATT_BLOCK) % ATT_UNROLL == 0
    blk = pl.BlockSpec((None, s, LANES), lambda i, j: (i, 0, j))
    return pl.pallas_call(
        functools.partial(_attn_prompt_kernel, seq=s),
        grid=(b, ATT_WIDTH // LANES),
        in_specs=[_const_spec((2, 2 * ATT_BLOCK, 2 * ATT_BLOCK)), blk, blk, blk],
        out_specs=blk,
        out_shape=jax.ShapeDtypeStruct((b, s, ATT_WIDTH), F32),
        scratch_shapes=[pltpu.VMEM((s, LANES), F32)] * 9,
        compiler_params=_cparams(("parallel", "parallel")),
        name="attn_prompt",
    )(_band_bias(), q, k, v)


def _attn_sample_body(mw_ref, mn_ref, q_ref, kn_ref, vn_ref, kc_ref, vc_ref,
                      att_ref, nk_ref, nv_ref, *, win, t_new, heads):
    mult_w = mw_ref[...]
    mult_n = mn_ref[...]
    lane = lax.broadcasted_iota(jnp.int32, (HD, LANES), 1)
    keep = lane < LANES - t_new
    top = jnp.zeros((LANES - t_new, heads * HD), F32)
    knt_all = jnp.transpose(jnp.concatenate([top, kn_ref[...]], axis=0))
    vnt_all = jnp.transpose(jnp.concatenate([top, vn_ref[...]], axis=0))
    for h in range(heads):
        qh = q_ref[:, h * HD:(h + 1) * HD].astype(BF16)
        kc = kc_ref[h]
        vc = vc_ref[h]
        knt = knt_all[h * HD:(h + 1) * HD, :]
        vnt = vnt_all[h * HD:(h + 1) * HD, :]
        s_w = jnp.dot(qh, kc.astype(BF16), preferred_element_type=F32)
        s_n = jnp.dot(qh, knt.astype(BF16), preferred_element_type=F32)
        s_w = jnp.where(mult_w > 0, s_w, NEG_INF)
        s_n = jnp.where(mult_n > 0, s_n, NEG_INF)
        mx = jnp.maximum(jnp.max(s_w, axis=1, keepdims=True), jnp.max(s_n, axis=1, keepdims=True))
        p_w = mult_w * jnp.exp(s_w - mx)
        p_n = mult_n * jnp.exp(s_n - mx)
        den = jnp.sum(p_w, axis=1, keepdims=True) + jnp.sum(p_n, axis=1, keepdims=True)
        o = lax.dot_general(p_w.astype(BF16), vc.astype(BF16), (((1,), (1,)), ((), ())),
                            preferred_element_type=F32)
        o = o + lax.dot_general(p_n.astype(BF16), vnt.astype(BF16), (((1,), (1,)), ((), ())),
                                preferred_element_type=F32)
        att_ref[:, h * HD:(h + 1) * HD] = o / den
        for src, new, dst in ((kc, knt, nk_ref), (vc, vnt, nv_ref)):
            rolled = pltpu.roll(src, win - t_new, 1)
            dst[h, :, 0:win - LANES] = rolled[:, 0:win - LANES]
            dst[h, :, win - LANES:win] = jnp.where(keep, rolled[:, win - LANES:win], new)


def _attn_sample_kernel(*refs, win, t_new):
    _attn_sample_body(*refs, win=win, t_new=t_new, heads=N_ATT_HEADS)


def _attn_mlp_kernel(mw_ref, mn_ref, q_ref, kn_ref, vn_ref, kc_ref, vc_ref,
                     x_ref, attp_ref, hbp_ref, wo_ref, g2_ref, wu_ref, wd_ref, gf_ref,
                     att_ref, nk_ref, nv_ref, y_ref, h2_sc, acc_sc, *, win, t_new, heads):
    c = pl.program_id(0) % FF_PARTS

    @pl.when(c == 0)
    def _():
        mix = (jnp.dot(attp_ref[...].astype(BF16), wo_ref[0:ATT_WIDTH, :], preferred_element_type=F32)
               + jnp.dot(hbp_ref[...].astype(BF16), wo_ref[ATT_WIDTH:D_MODEL, :], preferred_element_type=F32))
        x1 = x_ref[...] + mix
        h2_sc[...] = _rms(x1, g2_ref[...]).astype(BF16)
        acc_sc[...] = x1

    u = jnp.dot(h2_sc[...], wu_ref[c], preferred_element_type=F32)
    r = jnp.maximum(u, 0.0)
    acc_sc[...] += jnp.dot((r * r).astype(BF16), wd_ref[c], preferred_element_type=F32)
    _attn_sample_body(mw_ref, mn_ref, q_ref, kn_ref, vn_ref, kc_ref, vc_ref, att_ref, nk_ref, nv_ref,
                      win=win, t_new=t_new, heads=heads)

    @pl.when(c == FF_PARTS - 1)
    def _():
        y_ref[...] = _rms(acc_sc[...], gf_ref[...])


def _sample_multiplicity(win, t_new):
    t = np.arange(t_new)[:, None]
    idx = np.arange(win + t_new)[None, :]
    back = win + t - idx
    mult = np.zeros((t_new, win + t_new), np.float32)
    for dil in DILATIONS:
        mult += ((back >= 0) & (back % dil == 0) & (back // dil <= SUB_WINDOW)).astype(np.float32)
    mw = mult[:, :win]
    mn = np.zeros((t_new, LANES), np.float32)
    mn[:, LANES - t_new:] = mult[:, win:]
    return jnp.asarray(mw), jnp.asarray(mn)


def _attn_sample(q, k_new, v_new, kc_t, vc_t, t_new):
    b, nh, _, win = kc_t.shape
    assert win == DILATIONS[-1] * SUB_WINDOW and t_new == SUBLANES and q.shape[0] == b * t_new
    mw, mn = _sample_multiplicity(win, t_new)
    rows = pl.BlockSpec((t_new, ATT_WIDTH), lambda i: (i, 0))
    per_b = lambda *shape: pl.BlockSpec((None,) + shape, lambda i: (i,) + (0,) * len(shape))
    return pl.pallas_call(
        functools.partial(_attn_sample_kernel, win=win, t_new=t_new),
        grid=(b,),
        in_specs=[
            _const_spec((t_new, win)), _const_spec((t_new, LANES)),
            rows, rows, rows, per_b(nh, HD, win), per_b(nh, HD, win),
        ],
        out_specs=[rows, per_b(nh, HD, win), per_b(nh, HD, win)],
        out_shape=[
            jax.ShapeDtypeStruct((b * t_new, ATT_WIDTH), F32),
            jax.ShapeDtypeStruct((b, nh, HD, win), F32),
            jax.ShapeDtypeStruct((b, nh, HD, win), F32),
        ],
        compiler_params=_cparams(("parallel",)),
        name="attn_sample",
    )(mw, mn, q, k_new, v_new, kc_t, vc_t)


def _attn_mlp_fits(b_s, n_p):
    steps = b_s * ATT_HEAD_SPLIT
    if steps % FF_PARTS:
        return False
    tiles = steps // FF_PARTS
    return n_p % tiles == 0 and (n_p // tiles) % SUBLANES == 0 and n_p // tiles <= 512


def _attn_mlp(q, k_new, v_new, kc_t, vc_t, t_new, x2d, attp, hbp, wo_bf, g2, wu_parts, wd_parts, gf):
    b, nh, _, win = kc_t.shape
    n_p = x2d.shape[0]
    assert win == DILATIONS[-1] * SUB_WINDOW and t_new == SUBLANES and q.shape[0] == b * t_new
    heads = nh // ATT_HEAD_SPLIT
    steps = b * ATT_HEAD_SPLIT
    tm = n_p // (steps // FF_PARTS)
    ffc = D_FF // FF_PARTS
    mw, mn = _sample_multiplicity(win, t_new)
    rows = pl.BlockSpec((t_new, heads * HD), lambda i: (i // ATT_HEAD_SPLIT, i % ATT_HEAD_SPLIT))
    win_blk = pl.BlockSpec((None, heads, HD, win), lambda i: (i // ATT_HEAD_SPLIT, i % ATT_HEAD_SPLIT, 0, 0))
    tile = lambda w: pl.BlockSpec((tm, w), lambda i: (i // FF_PARTS, 0))
    return pl.pallas_call(
        functools.partial(_attn_mlp_kernel, win=win, t_new=t_new, heads=heads),
        grid=(steps,),
        in_specs=[
            _const_spec((t_new, win)), _const_spec((t_new, LANES)),
            rows, rows, rows, win_blk, win_blk,
            tile(D_MODEL), tile(ATT_WIDTH), tile(MLSTM_WIDTH),
            _const_spec((D_MODEL, D_MODEL)), _const_spec((1, D_MODEL)),
            _const_spec((FF_PARTS, D_MODEL, ffc)), _const_spec((FF_PARTS, ffc, D_MODEL)),
            _const_spec((1, D_MODEL)),
        ],
        out_specs=[rows, win_blk, win_blk, tile(D_MODEL)],
        out_shape=[
            jax.ShapeDtypeStruct((b * t_new, ATT_WIDTH), F32),
            jax.ShapeDtypeStruct((b, nh, HD, win), F32),
            jax.ShapeDtypeStruct((b, nh, HD, win), F32),
            jax.ShapeDtypeStruct((n_p, D_MODEL), F32),
        ],
        scratch_shapes=[pltpu.VMEM((tm, D_MODEL), BF16), pltpu.VMEM((tm, D_MODEL), F32)],
        compiler_params=_cparams(("arbitrary",)),
        name="attn_mlp",
    )(mw, mn, q, k_new, v_new, kc_t, vc_t, x2d, attp, hbp, wo_bf, g2, wu_parts, wd_parts, gf)


def _split_dot(tri, x):
    hi = x.astype(BF16)
    lo = (x - hi.astype(F32)).astype(BF16)
    return (jnp.dot(tri, hi, preferred_element_type=F32)
            + jnp.dot(tri, lo, preferred_element_type=F32))


def _split_dot_rows(x, tri):
    hi = x.astype(BF16)
    lo = (x - hi.astype(F32)).astype(BF16)
    return (jnp.dot(hi, tri, preferred_element_type=F32)
            + jnp.dot(lo, tri, preferred_element_type=F32))


def _log_sigmoid(x):
    return jnp.minimum(x, 0.0) - jnp.log1p(jnp.exp(-jnp.abs(x)))


def _mlstm_kernel(*refs, chunk, group, has_state):
    if has_state:
        (raw_ref, vb_ref, ob_ref, gate_ref, wc_ref, bc_ref, g_ref, tail_ref, c0_ref, n0_ref, m0_ref,
         hb_ref, tail_out_ref, c_out_ref, n_out_ref, m_out_ref, xp_ref, act_ref, c_sc, n_sc, m_sc) = refs
    else:
        (raw_ref, vb_ref, ob_ref, gate_ref, wc_ref, bc_ref, g_ref,
         hb_ref, tail_out_ref, c_out_ref, n_out_ref, m_out_ref, xp_ref, act_ref, c_sc, n_sc, m_sc) = refs
    L = chunk
    c_idx = pl.program_id(1)

    @pl.when(c_idx == 0)
    def _():
        if has_state:
            for c in range(2 * MLSTM_HEADS):
                xp_ref[:, c, 0:SUBLANES, :] = tail_ref[:, :, c * LANES:(c + 1) * LANES]
            c_sc[...] = c0_ref[...]
            n_sc[...] = n0_ref[...]
            m_sc[...] = m0_ref[...]
        else:
            xp_ref[:, :, 0:SUBLANES, :] = jnp.zeros((group, 2 * MLSTM_HEADS, SUBLANES, LANES), F32)
            c_sc[...] = jnp.zeros_like(c_sc)
            n_sc[...] = jnp.zeros_like(n_sc)
            m_sc[...] = jnp.zeros_like(m_sc)

    r_i = lax.broadcasted_iota(jnp.int32, (L, L), 0)
    c_i = lax.broadcasted_iota(jnp.int32, (L, L), 1)
    causal = r_i >= c_i
    tri_l = jnp.where(causal, 1.0, 0.0).astype(BF16)
    tri_u = jnp.where(r_i <= c_i, 1.0, 0.0).astype(BF16)
    first_tap = SUBLANES - (CONV_W - 1)
    bc = bc_ref[...]
    taps = [wc_ref[w:w + 1, :] for w in range(CONV_W)]

    for g in range(group):
        tail_out_ref[g] = raw_ref[g, L - SUBLANES:L, :]
        for c in range(2 * MLSTM_HEADS):
            cl = slice(c * LANES, (c + 1) * LANES)
            raw = raw_ref[g, :, cl]
            xp_ref[g, c, SUBLANES:SUBLANES + L, :] = raw
            if L % (SUBLANES * SUBLANES) == 0:
                n_rows = L // SUBLANES
                slabs = [xp_ref[g, c, pl.ds(first_tap + j, n_rows, stride=SUBLANES), :]
                         for j in range(SUBLANES + CONV_W - 1)]
                for s in range(SUBLANES):
                    conv = bc[:, cl]
                    for w in range(CONV_W):
                        conv = conv + slabs[s + w] * taps[w][:, cl]
                    act_ref[g, c, pl.ds(s, n_rows, stride=SUBLANES), :] = conv * jax.nn.sigmoid(conv)
            else:
                conv = bc[:, cl]
                for w in range(CONV_W):
                    conv = conv + xp_ref[g, c, first_tap + w:first_tap + w + L, :] * taps[w][:, cl]
                act_ref[g, c] = conv * jax.nn.sigmoid(conv)
            xp_ref[g, c, 0:SUBLANES, :] = raw[L - SUBLANES:L, :]

        gates = gate_ref[g]
        lf = _log_sigmoid(gates)
        b_col = _split_dot(tri_l, lf)
        if L == LANES:
            gates_sq = gates
        else:
            gates_sq = jnp.concatenate([gates, jnp.zeros((LANES - L, LANES), F32)], axis=0)
        gates_t = jnp.transpose(gates_sq)[0:SUBLANES, 0:L]
        lf_t = _log_sigmoid(gates_t)
        b_row = _split_dot_rows(lf_t, tri_u)

        for h in range(MLSTM_HEADS):
            sl = slice(h * MLSTM_HD, (h + 1) * MLSTM_HD)
            q = act_ref[g, h]
            k = act_ref[g, MLSTM_HEADS + h] * (MLSTM_HD ** -0.5)
            v = vb_ref[g, :, sl]
            qb, kb, vb = q.astype(BF16), k.astype(BF16), v.astype(BF16)
            ig_c = gates[:, h:h + 1]
            b_c = b_col[:, MLSTM_HEADS + h:MLSTM_HEADS + h + 1]
            ig_r = gates_t[h:h + 1, :]
            b_r = b_row[MLSTM_HEADS + h:MLSTM_HEADS + h + 1, :]
            m_prev = m_sc[g, h:h + 1, 0:1]
            a_c = b_c + m_prev
            dm = jnp.where(causal, b_c - b_r + ig_r, NEG_INF)
            mt = jnp.maximum(a_c, jnp.max(dm, axis=1, keepdims=True))
            w_inter = jnp.exp(a_c - mt)
            wm = jnp.exp(dm - mt)
            sc = lax.dot_general(qb, kb, (((1,), (1,)), ((), ())), preferred_element_type=F32) * wm
            c_old = c_sc[g, h]
            n_old = n_sc[g, h:h + 1, :]
            num = (w_inter * jnp.dot(qb, c_old.astype(BF16), preferred_element_type=F32)
                   + jnp.dot(sc.astype(BF16), vb, preferred_element_type=F32))
            den = (w_inter * jnp.sum(q * n_old, axis=1, keepdims=True)
                   + jnp.sum(sc, axis=1, keepdims=True))
            hh = num / jnp.maximum(jnp.abs(den), jnp.exp(-mt))
            m_last = mt[L - 1:L, :]
            w_last = jnp.exp(a_c[L - 1:L, :] - m_last)
            w_t = jnp.exp(b_c[L - 1:L, :] - b_c + ig_c - m_last)
            kw = k * w_t
            if L == LANES:
                kw_sq, v_sq = kw, v
            else:
                pad = jnp.zeros((LANES - L, MLSTM_HD), F32)
                kw_sq = jnp.concatenate([kw, pad], axis=0)
                v_sq = jnp.concatenate([v, pad], axis=0)
            c_new = w_last * c_old + jnp.dot(jnp.transpose(kw_sq).astype(BF16), v_sq.astype(BF16),
                                             preferred_element_type=F32)
            n_new = w_last * n_old + jnp.sum(kw, axis=0, keepdims=True)
            c_sc[g, h] = c_new
            n_sc[g, h:h + 1, :] = n_new
            m_sc[g, h:h + 1, :] = jnp.broadcast_to(m_last, (1, LANES))
            hn = hh * lax.rsqrt(jnp.mean(hh * hh, axis=1, keepdims=True) + EPS)
            hb_ref[g, :, sl] = hn * g_ref[:, sl] * jax.nn.sigmoid(ob_ref[g, :, sl])

    @pl.when(c_idx == pl.num_programs(1) - 1)
    def _():
        c_out_ref[...] = c_sc[...]
        n_out_ref[...] = n_sc[...]
        m_out_ref[...] = m_sc[...]


def _mlstm(raw, vb, ob, gates, w_conv, b_conv, mh_g, chunk, group, state=None):
    b, t, _ = raw.shape
    nc = t // chunk
    assert nc * chunk == t and chunk % SUBLANES == 0 and b % group == 0
    seq = lambda w: pl.BlockSpec((group, chunk, w), lambda i, j: (i, j, 0))
    per_b = lambda *shape: pl.BlockSpec((group,) + shape, lambda i, j: (i,) + (0,) * len(shape))
    in_specs = [seq(2 * MLSTM_WIDTH), seq(MLSTM_WIDTH), seq(MLSTM_WIDTH), seq(GATE_PAD),
                _const_spec((CONV_W, 2 * MLSTM_WIDTH)), _const_spec((1, 2 * MLSTM_WIDTH)),
                _const_spec((1, MLSTM_WIDTH))]
    args = [raw, vb, ob, gates, w_conv, b_conv, mh_g]
    state_shapes = [(SUBLANES, 2 * MLSTM_WIDTH), (MLSTM_HEADS, MLSTM_HD, MLSTM_HD),
                    (MLSTM_HEADS, MLSTM_HD), (MLSTM_HEADS, LANES)]
    state_specs = [per_b(*sh) for sh in state_shapes]
    if state is not None:
        in_specs += state_specs
        args += list(state)
    return pl.pallas_call(
        functools.partial(_mlstm_kernel, chunk=chunk, group=group, has_state=state is not None),
        grid=(b // group, nc),
        in_specs=in_specs,
        out_specs=[seq(MLSTM_WIDTH)] + state_specs,
        out_shape=[jax.ShapeDtypeStruct((b, t, MLSTM_WIDTH), F32)]
        + [jax.ShapeDtypeStruct((b,) + sh, F32) for sh in state_shapes],
        scratch_shapes=[
            pltpu.VMEM((group, 2 * MLSTM_HEADS, SUBLANES + chunk, LANES), F32),
            pltpu.VMEM((group, 2 * MLSTM_HEADS, chunk, LANES), F32),
        ] + [pltpu.VMEM((group,) + sh, F32) for sh in state_shapes[1:]],
        compiler_params=_cparams(("parallel", "arbitrary")),
        name="mlstm_state" if state is not None else "mlstm",
    )(*args)


def _outmlp_kernel(x_ref, att_ref, hb_ref, wo_ref, g2_ref, wu_ref, wd_ref, gf_ref, y_ref):
    mix = (jnp.dot(att_ref[...].astype(BF16), wo_ref[0:ATT_WIDTH, :], preferred_element_type=F32)
           + jnp.dot(hb_ref[...].astype(BF16), wo_ref[ATT_WIDTH:D_MODEL, :], preferred_element_type=F32))
    x1 = x_ref[...] + mix
    h2 = _rms(x1, g2_ref[...]).astype(BF16)
    acc = x1
    for c in range(FF_PARTS):
        u = jnp.dot(h2, wu_ref[c], preferred_element_type=F32)
        r = jnp.maximum(u, 0.0)
        acc = acc + jnp.dot((r * r).astype(BF16), wd_ref[c], preferred_element_type=F32)
    y_ref[...] = _rms(acc, gf_ref[...])


def _outmlp(x2d, att2d, hb2d, wo_bf, g2, wu_parts, wd_parts, gf, tm):
    n = x2d.shape[0]
    ffc = D_FF // FF_PARTS
    row = lambda w: pl.BlockSpec((tm, w), lambda i: (i, 0))
    return pl.pallas_call(
        _outmlp_kernel,
        grid=(n // tm,),
        in_specs=[row(D_MODEL), row(ATT_WIDTH), row(MLSTM_WIDTH),
                  _const_spec((D_MODEL, D_MODEL)), _const_spec((1, D_MODEL)),
                  _const_spec((FF_PARTS, D_MODEL, ffc)), _const_spec((FF_PARTS, ffc, D_MODEL)),
                  _const_spec((1, D_MODEL))],
        out_specs=row(D_MODEL),
        out_shape=jax.ShapeDtypeStruct((n, D_MODEL), F32),
        compiler_params=_cparams(("parallel",)),
        name="outmlp",
    )(x2d, att2d, hb2d, wo_bf, g2, wu_parts, wd_parts, gf)


def _project(x, pos, mix_params, tm, tile_tables):
    g1, w_main_bf, w_gate_bf, bg_pad = mix_params
    b, t, _ = x.shape
    x2d = x.reshape(b * t, D_MODEL)
    tm = min(tm, b * t)
    cos_t, sa_t, sb_t = _rotary_tables(pos)
    if tile_tables:
        cos_t, sa_t, sb_t = (jnp.tile(a, (tm // t, 1)) for a in (cos_t, sa_t, sb_t))
        table_blocks = 1
    else:
        table_blocks = t // tm
    return x2d, tm, _inproj(x2d, g1, w_main_bf, w_gate_bf, bg_pad, cos_t, sa_t, sb_t, tm, table_blocks)


def kernel(x_prompt, x_sample, cache_win_k, cache_win_v, state_conv, state_C, state_n, state_m, norm1_g, w_in, b_gate, w_conv, b_conv, mh_norm_g, w_out, norm2_g, w_up, w_down, norm_f_g):
    depth = w_in.shape[0]
    assert depth == 1, "the final norm is fused into the (single) layer's MLP kernel"
    l = 0
    n_gate = 2 * MLSTM_HEADS
    w_l = w_in[l]
    w_main_bf = w_l[:, :IN_COLS - n_gate].astype(BF16)
    w_gate_bf = jnp.pad(w_l[:, IN_COLS - n_gate:], ((0, 0), (0, GATE_PAD - n_gate))).astype(BF16)
    bg_pad = jnp.pad(b_gate[l], (0, GATE_PAD - n_gate))[None, :]
    mix_params = (norm1_g[l][None, :], w_main_bf, w_gate_bf, bg_pad)
    conv_params = (w_conv[l], b_conv[l][None, :], mh_norm_g[l][None, :])
    ffc = D_FF // FF_PARTS
    wu_parts = w_up[l].reshape(D_MODEL, FF_PARTS, ffc).transpose(1, 0, 2).astype(BF16)
    wd_parts = w_down[l].astype(BF16).reshape(FF_PARTS, ffc, D_MODEL)
    mlp_params = (w_out[l].astype(BF16), norm2_g[l][None, :], wu_parts, wd_parts, norm_f_g[None, :])

    bp, tp, _ = x_prompt.shape
    xp2d, tm_p, (q, k, v, raw, vb, ob, gates) = _project(x_prompt, jnp.arange(tp), mix_params, 512, False)
    r3 = lambda a: a.reshape(bp, tp, a.shape[-1])
    att_p = _attn_prompt(r3(q), r3(k), r3(v)).reshape(bp * tp, ATT_WIDTH)
    n_keep = min(DILATIONS[-1] * SUB_WINDOW, tp)
    heads = lambda a: r3(a)[:, tp - n_keep:].reshape(bp, n_keep, N_ATT_HEADS, HD)
    pk, pv = heads(k), heads(v)
    hb_p, tail_p, pc, pn, pm = _mlstm(r3(raw), r3(vb), r3(ob), r3(gates), *conv_params,
                                      CHUNK if tp % CHUNK == 0 else tp, 1)
    hb_p = hb_p.reshape(bp * tp, MLSTM_WIDTH)

    bs, ts, _ = x_sample.shape
    xs2d, tm_s, (q, k, v, raw, vb, ob, gates) = _project(x_sample, PAST_LEN + jnp.arange(ts), mix_params, 512, True)
    r3 = lambda a: a.reshape(bs, ts, a.shape[-1])
    tail0 = jnp.pad(state_conv[l], ((0, 0), (SUBLANES - (CONV_W - 1), 0), (0, 0)))
    m0b = jnp.broadcast_to(state_m[l][:, :, None], (bs, MLSTM_HEADS, LANES))
    hb_s, tail_s, sc, sn, sm = _mlstm(r3(raw), r3(vb), r3(ob), r3(gates), *conv_params, ts,
                                      MLSTM_SAMPLE_GROUP if bs % MLSTM_SAMPLE_GROUP == 0 else 1,
                                      state=(tail0, state_C[l], state_n[l], m0b))
    hb_s = hb_s.reshape(bs * ts, MLSTM_WIDTH)

    kc_t = cache_win_k[l].transpose(0, 2, 3, 1)
    vc_t = cache_win_v[l].transpose(0, 2, 3, 1)
    if _attn_mlp_fits(bs, bp * tp):
        att_s, nk_t, nv_t, y_p = _attn_mlp(q, k, v, kc_t, vc_t, ts, xp2d, att_p, hb_p, *mlp_params)
    else:
        att_s, nk_t, nv_t = _attn_sample(q, k, v, kc_t, vc_t, ts)
        y_p = _outmlp(xp2d, att_p, hb_p, *mlp_params, tm_p)
    sk, sv = nk_t.transpose(0, 3, 1, 2), nv_t.transpose(0, 3, 1, 2)
    y_s = _outmlp(xs2d, att_s, hb_s, *mlp_params, tm_s)

    first_tail = SUBLANES - (CONV_W - 1)
    outs = (y_p.reshape(bp, tp, D_MODEL), y_s.reshape(bs, ts, D_MODEL),
            pk, pv, tail_p[:, first_tail:, :], pc, pn, pm[:, :, 0],
            sk, sv, tail_s[:, first_tail:, :], sc, sn, sm[:, :, 0])
    return outs[:2] + tuple(o[None] for o in outs[2:])
```

```python
import functools

import jax
import jax.numpy as jnp
import numpy as np
from jax import lax
from jax.experimental import pallas as pl
from jax.experimental.pallas import tpu as pltpu

D_MODEL = 1024
HD = 64
N_ATT_HEADS = 8
ATT_WIDTH = N_ATT_HEADS * HD
MLSTM_HEADS = 4
MLSTM_WIDTH = D_MODEL - ATT_WIDTH
MLSTM_HD = MLSTM_WIDTH // MLSTM_HEADS
ROT_DIM = HD // 4
ROT_HALF = ROT_DIM // 2
ROPE_THETA = 500000.0
DILATIONS = (1, 4, 16)
SUB_WINDOW = 128
ATT_BLOCK = 128
ATT_UNROLL = 8
MLSTM_SAMPLE_GROUP = 8
MLSTM_PROMPT_GROUP = 2
FF_PARTS = 4
ATT_HEAD_SPLIT = 2
CONV_W = 4
CHUNK = 128
D_FF = 4 * D_MODEL
EPS = 1e-6
PAST_LEN = 8192
IN_SIZES = (ATT_WIDTH, ATT_WIDTH, ATT_WIDTH, 2 * MLSTM_WIDTH, MLSTM_WIDTH, MLSTM_WIDTH, 2 * MLSTM_HEADS)
IN_COLS = sum(IN_SIZES)

LANES = 128
SUBLANES = 8
GATE_PAD = LANES
VMEM_LIMIT = 56 * 1024 * 1024

F32 = jnp.float32
BF16 = jnp.bfloat16
NEG_INF = float("-inf")
LOG2E = 1.4426950408889634


def _cparams(sem):
    return pltpu.CompilerParams(dimension_semantics=sem, vmem_limit_bytes=VMEM_LIMIT)


def _const_spec(shape):
    nd = len(shape)
    return pl.BlockSpec(shape, lambda *_: (0,) * nd, pipeline_mode=pl.Buffered(1))


def _rms(x, g):
    return x * lax.rsqrt(jnp.mean(x * x, axis=-1, keepdims=True) + EPS) * g


def _inproj_kernel(x_ref, g_ref, w_ref, wg_ref, bg_ref, cos_ref, sa_ref, sb_ref,
                   q_ref, k_ref, v_ref, raw_ref, vb_ref, ob_ref, gate_ref, *win_refs,
                   tiles_per_seq, window_tiles):
    h = _rms(x_ref[...], g_ref[...]).astype(BF16)

    def proj(lo, width):
        return jnp.dot(h, w_ref[:, lo:lo + width], preferred_element_type=F32)

    cos, sa, sb = cos_ref[...], sa_ref[...], sb_ref[...]

    def rotary_store(dst, y, scale):
        for c in range(ATT_WIDTH // LANES):
            yc = y[:, c * LANES:(c + 1) * LANES]
            up = pltpu.roll(yc, LANES - ROT_HALF, 1)
            dn = pltpu.roll(yc, ROT_HALF, 1)
            r = yc * cos + up * sa + dn * sb
            dst[:, c * LANES:(c + 1) * LANES] = r * scale if scale != 1.0 else r

    off = 0
    rotary_store(q_ref, proj(off, ATT_WIDTH), HD ** -0.5)
    off += ATT_WIDTH
    rotary_store(k_ref, proj(off, ATT_WIDTH), 1.0)
    off += ATT_WIDTH
    v_ref[...] = proj(off, ATT_WIDTH)
    off += ATT_WIDTH
    raw_ref[...] = proj(off, 2 * MLSTM_WIDTH)
    off += 2 * MLSTM_WIDTH
    vb_ref[...] = proj(off, MLSTM_WIDTH)
    off += MLSTM_WIDTH
    ob_ref[...] = proj(off, MLSTM_WIDTH)
    gate_ref[...] = jnp.dot(h, wg_ref[...], preferred_element_type=F32) + bg_ref[...]

    if window_tiles:
        kt_ref, vt_ref = win_refs

        @pl.when(pl.program_id(0) % tiles_per_seq >= tiles_per_seq - window_tiles)
        def _():
            kt_ref[...] = jnp.transpose(k_ref[...])
            vt_ref[...] = jnp.transpose(v_ref[...])


def _inproj(x2d, g1, w_main_bf, w_gate_bf, bg_pad, tables, tm, table_blocks, tiles_per_seq=1, window_tiles=0):
    n = x2d.shape[0]
    grid = (n // tm,)
    row = lambda i: (i, 0)
    tab = lambda i: (i % table_blocks, 0)
    widths = (ATT_WIDTH, ATT_WIDTH, ATT_WIDTH, 2 * MLSTM_WIDTH, MLSTM_WIDTH, MLSTM_WIDTH, GATE_PAD)
    out_specs = [pl.BlockSpec((tm, w), row) for w in widths]
    out_shape = [jax.ShapeDtypeStruct((n, w), F32) for w in widths]
    if window_tiles:
        first = tiles_per_seq - window_tiles
        win = pl.BlockSpec((None, ATT_WIDTH, tm),
                           lambda i: (i // tiles_per_seq, 0, jnp.maximum(i % tiles_per_seq - first, 0)))
        out_specs += [win, win]
        out_shape += [jax.ShapeDtypeStruct((n // (tm * tiles_per_seq), ATT_WIDTH, window_tiles * tm), F32)] * 2
    return pl.pallas_call(
        functools.partial(_inproj_kernel, tiles_per_seq=tiles_per_seq, window_tiles=window_tiles),
        grid=grid,
        in_specs=[
            pl.BlockSpec((tm, D_MODEL), row),
            _const_spec((1, D_MODEL)),
            _const_spec((D_MODEL, IN_COLS - 2 * MLSTM_HEADS)),
            _const_spec((D_MODEL, GATE_PAD)),
            _const_spec((1, GATE_PAD)),
            pl.BlockSpec((tm, LANES), tab),
            pl.BlockSpec((tm, LANES), tab),
            pl.BlockSpec((tm, LANES), tab),
        ],
        out_specs=out_specs,
        out_shape=out_shape,
        compiler_params=_cparams(("arbitrary",)),
        name="inproj",
    )(x2d, g1, w_main_bf, w_gate_bf, bg_pad, *tables)


def _rotary_tables(pos):
    half = np.arange(ROT_HALF, dtype=np.float64)
    ang = np.asarray(pos, np.float64)[:, None] * (ROPE_THETA ** (-half / ROT_HALF))[None, :]
    cos, sin = np.cos(ang), np.sin(ang)
    p = ang.shape[0]
    ones = np.ones((p, HD - ROT_DIM))
    zeros = np.zeros((p, HD - ROT_DIM))
    z8 = np.zeros((p, ROT_HALF))
    cos_h = np.concatenate([cos, cos, ones], axis=1)
    sa_h = np.concatenate([-sin, z8, zeros], axis=1)
    sb_h = np.concatenate([z8, sin, zeros], axis=1)
    two = lambda t: np.concatenate([t, t], axis=1).astype(np.float32)
    return two(cos_h), two(sa_h), two(sb_h)


def _attn_prompt_kernel(bias_ref, q_ref, k_ref, v_ref, o_ref,
                        q4_ref, k4_ref, v4_ref, num1_ref, m1_ref, den1_ref, num4_ref, m4_ref, den4_ref,
                        *, seq):
    d_mid, d_out = DILATIONS[1], DILATIONS[2]
    d_in = d_out // d_mid
    lane = lax.broadcasted_iota(jnp.int32, (ATT_BLOCK, LANES), 1)
    head0 = lane < HD
    nblk = seq // ATT_BLOCK
    sub_blocks = nblk // d_mid
    ones_cols = jnp.ones((2 * ATT_BLOCK, LANES), BF16)

    def scores(q, kk, first):
        q = q * LOG2E
        q2 = jnp.concatenate([jnp.where(head0, q, 0.0), jnp.where(head0, 0.0, q)], axis=0).astype(BF16)
        s = lax.dot_general(q2, kk.astype(BF16), (((1,), (1,)), ((), ())),
                            preferred_element_type=F32)
        return s + bias_ref[first]

    def weighted_values(p, vv):
        o2 = jnp.dot(p.astype(BF16), jnp.concatenate([vv.astype(BF16), ones_cols], axis=1),
                     preferred_element_type=F32)
        o = jnp.where(head0, o2[:ATT_BLOCK, :LANES], o2[ATT_BLOCK:, :LANES])
        den = jnp.where(head0, o2[:ATT_BLOCK, LANES:], o2[ATT_BLOCK:, LANES:])
        return o, den

    def aligned(start):
        return pl.ds(pl.multiple_of(start, ATT_BLOCK), ATT_BLOCK)

    def split(i, carry):
        r, c = i // sub_blocks, i % sub_blocks
        src = pl.ds(r + c * ATT_BLOCK * d_mid, ATT_BLOCK, stride=d_mid)
        dst = aligned(i * ATT_BLOCK)
        q4_ref[dst, :] = q_ref[src, :]
        k4_ref[dst, :] = k_ref[src, :]
        v4_ref[dst, :] = v_ref[src, :]
        return carry

    lax.fori_loop(0, nblk, split, 0)

    def run_group(srcs, blocks_per_class, rows_of, merge, unroll=ATT_UNROLL):
        qs, ks, vs = srcs

        def body(i, carry):
            blocks = []
            for u in range(unroll):
                idx = i * unroll + u
                n = idx % blocks_per_class
                rows = rows_of(idx, n)
                prows = rows_of(idx - jnp.minimum(n, 1), jnp.maximum(n - 1, 0))
                first = jnp.where(n == 0, 1, 0)
                kk = jnp.concatenate([ks[prows, :], ks[rows, :]], axis=0)
                blocks.append([rows, prows, scores(qs[rows, :], kk, first)])
            for blk in blocks:
                s = blk[2]
                mx = jnp.max(s, axis=1, keepdims=True)
                blk[2] = jnp.exp2(s - mx)
                blk.append(jnp.where(head0, mx[:ATT_BLOCK], mx[ATT_BLOCK:]))
            for blk in blocks:
                rows, prows, p, _ = blk
                vv = jnp.concatenate([vs[prows, :], vs[rows, :]], axis=0)
                blk[2:3] = weighted_values(p, vv)
            for rows, _, o, den, mxb in blocks:
                merge(rows, o, mxb, den)
            return carry

        lax.fori_loop(0, nblk // unroll, body, 0)

    def init_state(num_ref, m_ref, den_ref):
        def merge(rows, o, mxb, den):
            num_ref[rows, :] = o
            m_ref[rows, :] = mxb
            den_ref[rows, :] = den
        return merge

    def merge4(rows, o, mxb, den):
        m_old = m4_ref[rows, :]
        m_new = jnp.maximum(m_old, mxb)
        a = jnp.exp2(m_old - m_new)
        b = jnp.exp2(mxb - m_new)
        num4_ref[rows, :] = num4_ref[rows, :] * a + o * b
        den4_ref[rows, :] = den4_ref[rows, :] * a + den * b
        m4_ref[rows, :] = m_new

    run_group((q_ref, k_ref, v_ref), nblk, lambda idx, n: aligned(idx * ATT_BLOCK),
              init_state(num1_ref, m1_ref, den1_ref))
    blocks16 = nblk // d_out

    def rows16(idx, n):
        cls = idx // blocks16
        r, j = cls // d_in, cls % d_in
        return pl.ds(r * (sub_blocks * ATT_BLOCK) + j + n * ATT_BLOCK * d_in, ATT_BLOCK, stride=d_in)

    run_group((q4_ref, k4_ref, v4_ref), blocks16, rows16, init_state(num4_ref, m4_ref, den4_ref))
    run_group((q4_ref, k4_ref, v4_ref), sub_blocks, lambda idx, n: aligned(idx * ATT_BLOCK), merge4)

    def finish(i, carry):
        r, c = i // sub_blocks, i % sub_blocks
        tok = pl.ds(r + c * ATT_BLOCK * d_mid, ATT_BLOCK, stride=d_mid)
        res = aligned(i * ATT_BLOCK)
        m_a, m_b = m4_ref[res, :], m1_ref[tok, :]
        m_new = jnp.maximum(m_a, m_b)
        wa = jnp.exp2(m_a - m_new)
        wb = jnp.exp2(m_b - m_new)
        num = num4_ref[res, :] * wa + num1_ref[tok, :] * wb
        den = den4_ref[res, :] * wa + den1_ref[tok, :] * wb
        o_ref[tok, :] = num / den
        return carry

    lax.fori_loop(0, nblk, finish, 0)


def _band_bias():
    row = np.arange(2 * ATT_BLOCK)[:, None] % ATT_BLOCK
    ki = np.arange(2 * ATT_BLOCK)[None, :] - ATT_BLOCK
    rel = row - ki
    valid = (rel >= 0) & (rel <= SUB_WINDOW)
    b0 = np.where(valid, 0.0, NEG_INF).astype(np.float32)
    b1 = np.where(valid & (ki >= 0), 0.0, NEG_INF).astype(np.float32)
    return jnp.asarray(np.stack([b0, b1]))


def _attn_prompt(q, k, v):
    b, s, _ = q.shape
    assert DILATIONS[0] == 1 and DILATIONS[2] % DILATIONS[1] == 0
    assert s % (ATT_BLOCK * DILATIONS[-1]) == 0 and (s // ATT_BLOCK) % ATT_UNROLL == 0
    blk = pl.BlockSpec((None, s, LANES), lambda i, j: (i, 0, j))
    return pl.pallas_call(
        functools.partial(_attn_prompt_kernel, seq=s),
        grid=(b, ATT_WIDTH // LANES),
        in_specs=[_const_spec((2, 2 * ATT_BLOCK, 2 * ATT_BLOCK)), blk, blk, blk],
        out_specs=blk,
        out_shape=jax.ShapeDtypeStruct((b, s, ATT_WIDTH), F32),
        scratch_shapes=[pltpu.VMEM((s, LANES), F32)] * 9,
        compiler_params=_cparams(("parallel", "parallel")),
        name="attn_prompt",
    )(_band_bias(), q, k, v)


def _attn_sample_body(mw_ref, mn_ref, q_ref, kn_ref, vn_ref, kc_ref, vc_ref,
                      att_ref, nk_ref, nv_ref, *, win, t_new, heads):
    mult_w = mw_ref[...]
    mult_n = mn_ref[...]
    lane = lax.broadcasted_iota(jnp.int32, (HD, LANES), 1)
    keep = lane < LANES - t_new
    top = jnp.zeros((LANES - t_new, heads * HD), F32)
    knt_all = jnp.transpose(jnp.concatenate([top, kn_ref[...]], axis=0))
    vnt_all = jnp.transpose(jnp.concatenate([top, vn_ref[...]], axis=0))
    head_rows = lambda a, h: a[h * HD:(h + 1) * HD, :]
    for h in range(heads):
        for src_ref, new_all, dst in ((kc_ref, knt_all, nk_ref), (vc_ref, vnt_all, nv_ref)):
            rolled = pltpu.roll(src_ref[h], win - t_new, 1)
            dst[h, :, 0:win - LANES] = rolled[:, 0:win - LANES]
            dst[h, :, win - LANES:win] = jnp.where(keep, rolled[:, win - LANES:win], head_rows(new_all, h))
    sc = []
    for h in range(heads):
        qh = q_ref[:, h * HD:(h + 1) * HD].astype(BF16)
        s_w = jnp.dot(qh, kc_ref[h].astype(BF16), preferred_element_type=F32)
        s_n = jnp.dot(qh, head_rows(knt_all, h).astype(BF16), preferred_element_type=F32)
        sc.append((jnp.where(mult_w > 0, s_w, NEG_INF), jnp.where(mult_n > 0, s_n, NEG_INF)))
    pr = []
    for s_w, s_n in sc:
        mx = jnp.maximum(jnp.max(s_w, axis=1, keepdims=True), jnp.max(s_n, axis=1, keepdims=True))
        p_w = mult_w * jnp.exp(s_w - mx)
        p_n = mult_n * jnp.exp(s_n - mx)
        den = jnp.sum(p_w, axis=1, keepdims=True) + jnp.sum(p_n, axis=1, keepdims=True)
        pr.append((p_w, p_n, den))
    for h, (p_w, p_n, den) in enumerate(pr):
        o = lax.dot_general(p_w.astype(BF16), vc_ref[h].astype(BF16), (((1,), (1,)), ((), ())),
                            preferred_element_type=F32)
        o = o + lax.dot_general(p_n.astype(BF16), head_rows(vnt_all, h).astype(BF16),
                                (((1,), (1,)), ((), ())), preferred_element_type=F32)
        att_ref[:, h * HD:(h + 1) * HD] = o / den


def _attn_sample_kernel(*refs, win, t_new):
    _attn_sample_body(*refs, win=win, t_new=t_new, heads=N_ATT_HEADS)


def _attn_mlp_kernel(mw_ref, mn_ref, q_ref, kn_ref, vn_ref, kc_ref, vc_ref,
                     x_ref, attp_ref, hbp_ref, wo_ref, g2_ref, wu_ref, wd_ref, gf_ref,
                     att_ref, nk_ref, nv_ref, y_ref, h2_sc, acc_sc, *, win, t_new, heads):
    c = pl.program_id(0) % FF_PARTS

    @pl.when(c == 0)
    def _():
        mix = (jnp.dot(attp_ref[...].astype(BF16), wo_ref[0:ATT_WIDTH, :], preferred_element_type=F32)
               + jnp.dot(hbp_ref[...].astype(BF16), wo_ref[ATT_WIDTH:D_MODEL, :], preferred_element_type=F32))
        x1 = x_ref[...] + mix
        h2_sc[...] = _rms(x1, g2_ref[...]).astype(BF16)
        acc_sc[...] = x1

    u = jnp.dot(h2_sc[...], wu_ref[c], preferred_element_type=F32)
    r = jnp.maximum(u, 0.0)
    acc_sc[...] += jnp.dot((r * r).astype(BF16), wd_ref[c], preferred_element_type=F32)
    _attn_sample_body(mw_ref, mn_ref, q_ref, kn_ref, vn_ref, kc_ref, vc_ref, att_ref, nk_ref, nv_ref,
                      win=win, t_new=t_new, heads=heads)

    @pl.when(c == FF_PARTS - 1)
    def _():
        y_ref[...] = _rms(acc_sc[...], gf_ref[...])


def _sample_multiplicity(win, t_new):
    t = np.arange(t_new)[:, None]
    idx = np.arange(win + t_new)[None, :]
    back = win + t - idx
    mult = np.zeros((t_new, win + t_new), np.float32)
    for dil in DILATIONS:
        mult += ((back >= 0) & (back % dil == 0) & (back // dil <= SUB_WINDOW)).astype(np.float32)
    mw = mult[:, :win]
    mn = np.zeros((t_new, LANES), np.float32)
    mn[:, LANES - t_new:] = mult[:, win:]
    return jnp.asarray(mw), jnp.asarray(mn)


def _attn_sample(q, k_new, v_new, kc_t, vc_t, t_new):
    b, nh, _, win = kc_t.shape
    assert win == DILATIONS[-1] * SUB_WINDOW and t_new == SUBLANES and q.shape[0] == b * t_new
    mw, mn = _sample_multiplicity(win, t_new)
    rows = pl.BlockSpec((t_new, ATT_WIDTH), lambda i: (i, 0))
    per_b = lambda *shape: pl.BlockSpec((None,) + shape, lambda i: (i,) + (0,) * len(shape))
    return pl.pallas_call(
        functools.partial(_attn_sample_kernel, win=win, t_new=t_new),
        grid=(b,),
        in_specs=[
            _const_spec((t_new, win)), _const_spec((t_new, LANES)),
            rows, rows, rows, per_b(nh, HD, win), per_b(nh, HD, win),
        ],
        out_specs=[rows, per_b(nh, HD, win), per_b(nh, HD, win)],
        out_shape=[
            jax.ShapeDtypeStruct((b * t_new, ATT_WIDTH), F32),
            jax.ShapeDtypeStruct((b, nh, HD, win), F32),
            jax.ShapeDtypeStruct((b, nh, HD, win), F32),
        ],
        compiler_params=_cparams(("parallel",)),
        name="attn_sample",
    )(mw, mn, q, k_new, v_new, kc_t, vc_t)


def _attn_mlp_fits(b_s, n_p):
    steps = b_s * ATT_HEAD_SPLIT
    if steps % FF_PARTS:
        return False
    tiles = steps // FF_PARTS
    return n_p % tiles == 0 and (n_p // tiles) % SUBLANES == 0 and n_p // tiles <= 512


def _attn_mlp(q, k_new, v_new, kc_t, vc_t, t_new, x2d, attp, hbp, wo_bf, g2, wu_parts, wd_parts, gf):
    b, nh, _, win = kc_t.shape
    n_p = x2d.shape[0]
    assert win == DILATIONS[-1] * SUB_WINDOW and t_new == SUBLANES and q.shape[0] == b * t_new
    heads = nh // ATT_HEAD_SPLIT
    steps = b * ATT_HEAD_SPLIT
    tm = n_p // (steps // FF_PARTS)
    ffc = D_FF // FF_PARTS
    mw, mn = _sample_multiplicity(win, t_new)
    rows = pl.BlockSpec((t_new, heads * HD), lambda i: (i // ATT_HEAD_SPLIT, i % ATT_HEAD_SPLIT))
    win_blk = pl.BlockSpec((None, heads, HD, win), lambda i: (i // ATT_HEAD_SPLIT, i % ATT_HEAD_SPLIT, 0, 0))
    tile = lambda w: pl.BlockSpec((tm, w), lambda i: (i // FF_PARTS, 0))
    return pl.pallas_call(
        functools.partial(_attn_mlp_kernel, win=win, t_new=t_new, heads=heads),
        grid=(steps,),
        in_specs=[
            _const_spec((t_new, win)), _const_spec((t_new, LANES)),
            rows, rows, rows, win_blk, win_blk,
            tile(D_MODEL), tile(ATT_WIDTH), tile(MLSTM_WIDTH),
            _const_spec((D_MODEL, D_MODEL)), _const_spec((1, D_MODEL)),
            _const_spec((FF_PARTS, D_MODEL, ffc)), _const_spec((FF_PARTS, ffc, D_MODEL)),
            _const_spec((1, D_MODEL)),
        ],
        out_specs=[rows, win_blk, win_blk, tile(D_MODEL)],
        out_shape=[
            jax.ShapeDtypeStruct((b * t_new, ATT_WIDTH), F32),
            jax.ShapeDtypeStruct((b, nh, HD, win), F32),
            jax.ShapeDtypeStruct((b, nh, HD, win), F32),
            jax.ShapeDtypeStruct((n_p, D_MODEL), F32),
        ],
        scratch_shapes=[pltpu.VMEM((tm, D_MODEL), BF16), pltpu.VMEM((tm, D_MODEL), F32)],
        compiler_params=_cparams(("arbitrary",)),
        name="attn_mlp",
    )(mw, mn, q, k_new, v_new, kc_t, vc_t, x2d, attp, hbp, wo_bf, g2, wu_parts, wd_parts, gf)


def _split_dot(tri, x):
    hi = x.astype(BF16)
    lo = (x - hi.astype(F32)).astype(BF16)
    return (jnp.dot(tri, hi, preferred_element_type=F32)
            + jnp.dot(tri, lo, preferred_element_type=F32))


def _split_dot_rows(x, tri):
    hi = x.astype(BF16)
    lo = (x - hi.astype(F32)).astype(BF16)
    return (jnp.dot(hi, tri, preferred_element_type=F32)
            + jnp.dot(lo, tri, preferred_element_type=F32))


def _log_sigmoid(x):
    return jnp.minimum(x, 0.0) - jnp.log1p(jnp.exp(-jnp.abs(x)))


def _mlstm_kernel(*refs, chunk, group, has_state):
    if has_state:
        (raw_ref, vb_ref, ob_ref, gate_ref, wc_ref, bc_ref, g_ref, tail_ref, c0_ref, n0_ref, m0_ref,
         hb_ref, tail_out_ref, c_out_ref, n_out_ref, m_out_ref, xp_ref, act_ref, c_sc, n_sc, m_sc) = refs
        nm_sc = None
    else:
        (raw_ref, vb_ref, ob_ref, gate_ref, wc_ref, bc_ref, g_ref,
         hb_ref, tail_out_ref, c_out_ref, n_out_ref, m_out_ref, xp_ref, act_ref, c_sc, n_sc, m_sc, nm_sc) = refs
    L = chunk
    c_idx = pl.program_id(1)
    mxu_sums = nm_sc is not None and L == LANES

    @pl.when(c_idx == 0)
    def _():
        if has_state:
            for c in range(2 * MLSTM_HEADS):
                xp_ref[:, c, 0:SUBLANES, :] = tail_ref[:, :, c * LANES:(c + 1) * LANES]
            c_sc[...] = c0_ref[...]
            n_sc[...] = n0_ref[...]
            m_sc[...] = m0_ref[...]
        else:
            xp_ref[:, :, 0:SUBLANES, :] = jnp.zeros((group, 2 * MLSTM_HEADS, SUBLANES, LANES), F32)
            c_sc[...] = jnp.zeros_like(c_sc)
            n_sc[...] = jnp.zeros_like(n_sc)
            m_sc[...] = jnp.zeros_like(m_sc)
            nm_sc[...] = jnp.zeros_like(nm_sc)

    r_i = lax.broadcasted_iota(jnp.int32, (L, L), 0)
    c_i = lax.broadcasted_iota(jnp.int32, (L, L), 1)
    causal = r_i >= c_i
    tri_l = jnp.where(causal, 1.0, 0.0).astype(BF16)
    tri_u = jnp.where(r_i <= c_i, 1.0, 0.0).astype(BF16)
    first_tap = SUBLANES - (CONV_W - 1)
    bc = bc_ref[...]
    taps = [wc_ref[w:w + 1, :] for w in range(CONV_W)]
    ones_sq = jnp.ones((LANES, LANES), BF16)

    per_batch = []
    for g in range(group):
        tail_out_ref[g] = raw_ref[g, L - SUBLANES:L, :]
        for c in range(2 * MLSTM_HEADS):
            cl = slice(c * LANES, (c + 1) * LANES)
            raw = raw_ref[g, :, cl]
            xp_ref[g, c, SUBLANES:SUBLANES + L, :] = raw
            if L % (SUBLANES * SUBLANES) == 0:
                n_rows = L // SUBLANES
                slabs = [xp_ref[g, c, pl.ds(first_tap + j, n_rows, stride=SUBLANES), :]
                         for j in range(SUBLANES + CONV_W - 1)]
                for s in range(SUBLANES):
                    conv = bc[:, cl]
                    for w in range(CONV_W):
                        conv = conv + slabs[s + w] * taps[w][:, cl]
                    act_ref[g, c, pl.ds(s, n_rows, stride=SUBLANES), :] = conv * jax.nn.sigmoid(conv)
            else:
                conv = bc[:, cl]
                for w in range(CONV_W):
                    conv = conv + xp_ref[g, c, first_tap + w:first_tap + w + L, :] * taps[w][:, cl]
                act_ref[g, c] = conv * jax.nn.sigmoid(conv)
            xp_ref[g, c, 0:SUBLANES, :] = raw[L - SUBLANES:L, :]

        gates = gate_ref[g]
        lf = _log_sigmoid(gates)
        b_col = _split_dot(tri_l, lf)
        if L == LANES:
            gates_sq = gates
        else:
            gates_sq = jnp.concatenate([gates, jnp.zeros((LANES - L, LANES), F32)], axis=0)
        gates_t = jnp.transpose(gates_sq)[0:SUBLANES, 0:L]
        lf_t = _log_sigmoid(gates_t)
        b_row = _split_dot_rows(lf_t, tri_u)

        per_batch.append((gates, b_col, gates_t, b_row))

    items = [(g, h) for g in range(group) for h in range(MLSTM_HEADS)]
    st = {}
    for g, h in items:
        gates, b_col, gates_t, b_row = per_batch[g]
        ig_c = gates[:, h:h + 1]
        b_c = b_col[:, MLSTM_HEADS + h:MLSTM_HEADS + h + 1]
        ig_r = gates_t[h:h + 1, :]
        b_r = b_row[MLSTM_HEADS + h:MLSTM_HEADS + h + 1, :]
        a_c = b_c + m_sc[g, h:h + 1, 0:1]
        dm = jnp.where(causal, b_c - b_r + ig_r, NEG_INF)
        mt = jnp.maximum(a_c, jnp.max(dm, axis=1, keepdims=True))
        st[g, h] = dict(ig_c=ig_c, b_c=b_c, a_c=a_c, dm=dm, mt=mt)
    for g, h in items:
        d = st[g, h]
        q = act_ref[g, h]
        k = act_ref[g, MLSTM_HEADS + h] * (MLSTM_HD ** -0.5)
        v = vb_ref[g, :, h * MLSTM_HD:(h + 1) * MLSTM_HD]
        qb, kb = q.astype(BF16), k.astype(BF16)
        w_inter = jnp.exp(d["a_c"] - d["mt"])
        wm = jnp.exp(d["dm"] - d["mt"])
        sc = lax.dot_general(qb, kb, (((1,), (1,)), ((), ())), preferred_element_type=F32) * wm
        d.update(q=q, k=k, v=v, qb=qb, vb=v.astype(BF16), w_inter=w_inter, sc=sc)
        del d["dm"]
    for g, h in items:
        d = st[g, h]
        c_old = c_sc[g, h]
        if mxu_sums:
            nm_old = nm_sc[g, h]
            v_ones = jnp.concatenate([d["vb"], ones_sq], axis=1)
            state = jnp.concatenate([c_old, nm_old], axis=1).astype(BF16)
            both = (d["w_inter"] * jnp.dot(d["qb"], state, preferred_element_type=F32)
                    + jnp.dot(d["sc"].astype(BF16), v_ones, preferred_element_type=F32))
            num, den = both[:, :LANES], both[:, LANES:]
            d.update(v_ones=v_ones, nm_old=nm_old)
        else:
            n_old = n_sc[g, h:h + 1, :]
            num = (d["w_inter"] * jnp.dot(d["qb"], c_old.astype(BF16), preferred_element_type=F32)
                   + jnp.dot(d["sc"].astype(BF16), d["vb"], preferred_element_type=F32))
            den = (d["w_inter"] * jnp.sum(d["q"] * n_old, axis=1, keepdims=True)
                   + jnp.sum(d["sc"], axis=1, keepdims=True))
            d.update(n_old=n_old)
        d.update(c_old=c_old, hh=num / jnp.maximum(jnp.abs(den), jnp.exp(-d["mt"])))
        del d["sc"]
    for g, h in items:
        d = st[g, h]
        mt, a_c, b_c = d["mt"], d["a_c"], d["b_c"]
        m_last = mt[L - 1:L, :]
        w_last = jnp.exp(a_c[L - 1:L, :] - m_last)
        w_t = jnp.exp(b_c[L - 1:L, :] - b_c + d["ig_c"] - m_last)
        kw = d["k"] * w_t
        if mxu_sums:
            upd = jnp.dot(jnp.transpose(kw).astype(BF16), d["v_ones"], preferred_element_type=F32)
            c_sc[g, h] = w_last * d["c_old"] + upd[:, :LANES]
            nm_sc[g, h] = w_last * d["nm_old"] + upd[:, LANES:]
        else:
            if L == LANES:
                kw_sq, v_sq = kw, d["v"]
            else:
                pad = jnp.zeros((LANES - L, MLSTM_HD), F32)
                kw_sq = jnp.concatenate([kw, pad], axis=0)
                v_sq = jnp.concatenate([d["v"], pad], axis=0)
            c_sc[g, h] = w_last * d["c_old"] + jnp.dot(jnp.transpose(kw_sq).astype(BF16), v_sq.astype(BF16),
                                                       preferred_element_type=F32)
            n_sc[g, h:h + 1, :] = w_last * d["n_old"] + jnp.sum(kw, axis=0, keepdims=True)
        m_sc[g, h:h + 1, :] = jnp.broadcast_to(m_last, (1, LANES))
    for g, h in items:
        hh = st[g, h]["hh"]
        sl = slice(h * MLSTM_HD, (h + 1) * MLSTM_HD)
        if mxu_sums:
            mean_sq = _split_dot_rows(hh * hh, ones_sq) * (1.0 / MLSTM_HD)
        else:
            mean_sq = jnp.mean(hh * hh, axis=1, keepdims=True)
        hn = hh * lax.rsqrt(mean_sq + EPS)
        hb_ref[g, :, sl] = hn * g_ref[:, sl] * jax.nn.sigmoid(ob_ref[g, :, sl])

    @pl.when(c_idx == pl.num_programs(1) - 1)
    def _():
        if mxu_sums:
            for g in range(group):
                for h in range(MLSTM_HEADS):
                    n_sc[g, h:h + 1, :] = jnp.transpose(nm_sc[g, h])[0:1, :]
        c_out_ref[...] = c_sc[...]
        n_out_ref[...] = n_sc[...]
        m_out_ref[...] = m_sc[...]


def _mlstm(raw, vb, ob, gates, w_conv, b_conv, mh_g, chunk, group, state=None):
    b, t, _ = raw.shape
    nc = t // chunk
    assert nc * chunk == t and chunk % SUBLANES == 0 and b % group == 0
    seq = lambda w: pl.BlockSpec((group, chunk, w), lambda i, j: (i, j, 0))
    per_b = lambda *shape: pl.BlockSpec((group,) + shape, lambda i, j: (i,) + (0,) * len(shape))
    in_specs = [seq(2 * MLSTM_WIDTH), seq(MLSTM_WIDTH), seq(MLSTM_WIDTH), seq(GATE_PAD),
                _const_spec((CONV_W, 2 * MLSTM_WIDTH)), _const_spec((1, 2 * MLSTM_WIDTH)),
                _const_spec((1, MLSTM_WIDTH))]
    args = [raw, vb, ob, gates, w_conv, b_conv, mh_g]
    state_shapes = [(SUBLANES, 2 * MLSTM_WIDTH), (MLSTM_HEADS, MLSTM_HD, MLSTM_HD),
                    (MLSTM_HEADS, MLSTM_HD), (MLSTM_HEADS, LANES)]
    state_specs = [per_b(*sh) for sh in state_shapes]
    if state is not None:
        in_specs += state_specs
        args += list(state)
    return pl.pallas_call(
        functools.partial(_mlstm_kernel, chunk=chunk, group=group, has_state=state is not None),
        grid=(b // group, nc),
        in_specs=in_specs,
        out_specs=[seq(MLSTM_WIDTH)] + state_specs,
        out_shape=[jax.ShapeDtypeStruct((b, t, MLSTM_WIDTH), F32)]
        + [jax.ShapeDtypeStruct((b,) + sh, F32) for sh in state_shapes],
        scratch_shapes=[
            pltpu.VMEM((group, 2 * MLSTM_HEADS, SUBLANES + chunk, LANES), F32),
            pltpu.VMEM((group, 2 * MLSTM_HEADS, chunk, LANES), F32),
        ] + [pltpu.VMEM((group,) + sh, F32) for sh in state_shapes[1:]]
        + ([] if state is not None else [pltpu.VMEM((group,) + state_shapes[1], F32)]),
        compiler_params=_cparams(("parallel", "arbitrary")),
        name="mlstm_state" if state is not None else "mlstm",
    )(*args)


def _outmlp_kernel(x_ref, att_ref, hb_ref, wo_ref, g2_ref, wu_ref, wd_ref, gf_ref, y_ref):
    mix = (jnp.dot(att_ref[...].astype(BF16), wo_ref[0:ATT_WIDTH, :], preferred_element_type=F32)
           + jnp.dot(hb_ref[...].astype(BF16), wo_ref[ATT_WIDTH:D_MODEL, :], preferred_element_type=F32))
    x1 = x_ref[...] + mix
    h2 = _rms(x1, g2_ref[...]).astype(BF16)
    acc = x1
    for c in range(FF_PARTS):
        u = jnp.dot(h2, wu_ref[c], preferred_element_type=F32)
        r = jnp.maximum(u, 0.0)
        acc = acc + jnp.dot((r * r).astype(BF16), wd_ref[c], preferred_element_type=F32)
    y_ref[...] = _rms(acc, gf_ref[...])


def _outmlp(x2d, att2d, hb2d, wo_bf, g2, wu_parts, wd_parts, gf, tm):
    n = x2d.shape[0]
    ffc = D_FF // FF_PARTS
    row = lambda w: pl.BlockSpec((tm, w), lambda i: (i, 0))
    return pl.pallas_call(
        _outmlp_kernel,
        grid=(n // tm,),
        in_specs=[row(D_MODEL), row(ATT_WIDTH), row(MLSTM_WIDTH),
                  _const_spec((D_MODEL, D_MODEL)), _const_spec((1, D_MODEL)),
                  _const_spec((FF_PARTS, D_MODEL, ffc)), _const_spec((FF_PARTS, ffc, D_MODEL)),
                  _const_spec((1, D_MODEL))],
        out_specs=row(D_MODEL),
        out_shape=jax.ShapeDtypeStruct((n, D_MODEL), F32),
        compiler_params=_cparams(("parallel",)),
        name="outmlp",
    )(x2d, att2d, hb2d, wo_bf, g2, wu_parts, wd_parts, gf)


def _project(x, pos, mix_params, tm, tile_tables, window=0):
    g1, w_main_bf, w_gate_bf, bg_pad = mix_params
    b, t, _ = x.shape
    x2d = x.reshape(b * t, D_MODEL)
    tm = min(tm, b * t)
    tables = _rotary_tables(pos)
    if tile_tables:
        tables = tuple(np.tile(a, (tm // t, 1)) for a in tables)
        table_blocks, tiles_per_seq = 1, 1
    else:
        table_blocks = tiles_per_seq = t // tm
    assert window % tm == 0
    outs = _inproj(x2d, g1, w_main_bf, w_gate_bf, bg_pad, tuple(jnp.asarray(a) for a in tables), tm,
                   table_blocks, tiles_per_seq, window // tm)
    return x2d, tm, outs


def kernel(x_prompt, x_sample, cache_win_k, cache_win_v, state_conv, state_C, state_n, state_m, norm1_g, w_in, b_gate, w_conv, b_conv, mh_norm_g, w_out, norm2_g, w_up, w_down, norm_f_g):
    depth = w_in.shape[0]
    assert depth == 1, "the final norm is fused into the (single) layer's MLP kernel"
    l = 0
    n_gate = 2 * MLSTM_HEADS
    w_l = w_in[l]
    w_main_bf = w_l[:, :IN_COLS - n_gate].astype(BF16)
    w_gate_bf = jnp.pad(w_l[:, IN_COLS - n_gate:], ((0, 0), (0, GATE_PAD - n_gate))).astype(BF16)
    bg_pad = jnp.pad(b_gate[l], (0, GATE_PAD - n_gate))[None, :]
    mix_params = (norm1_g[l][None, :], w_main_bf, w_gate_bf, bg_pad)
    conv_params = (w_conv[l], b_conv[l][None, :], mh_norm_g[l][None, :])
    ffc = D_FF // FF_PARTS
    wu_parts = w_up[l].reshape(D_MODEL, FF_PARTS, ffc).transpose(1, 0, 2).astype(BF16)
    wd_parts = w_down[l].astype(BF16).reshape(FF_PARTS, ffc, D_MODEL)
    mlp_params = (w_out[l].astype(BF16), norm2_g[l][None, :], wu_parts, wd_parts, norm_f_g[None, :])

    bp, tp, _ = x_prompt.shape
    n_keep = min(DILATIONS[-1] * SUB_WINDOW, tp)
    xp2d, tm_p, (q, k, v, raw, vb, ob, gates, kt, vt) = _project(x_prompt, np.arange(tp), mix_params, 512, False,
                                                               window=n_keep)
    r3 = lambda a: a.reshape(bp, tp, a.shape[-1])
    att_p = _attn_prompt(r3(q), r3(k), r3(v)).reshape(bp * tp, ATT_WIDTH)
    heads = lambda a: a.reshape(bp, N_ATT_HEADS, HD, n_keep).transpose(0, 3, 1, 2)
    pk, pv = heads(kt), heads(vt)
    hb_p, tail_p, pc, pn, pm = _mlstm(r3(raw), r3(vb), r3(ob), r3(gates), *conv_params,
                                      CHUNK if tp % CHUNK == 0 else tp,
                                      MLSTM_PROMPT_GROUP if bp % MLSTM_PROMPT_GROUP == 0 else 1)
    hb_p = hb_p.reshape(bp * tp, MLSTM_WIDTH)

    bs, ts, _ = x_sample.shape
    xs2d, tm_s, (q, k, v, raw, vb, ob, gates) = _project(x_sample, PAST_LEN + np.arange(ts), mix_params, 512, True)
    r3 = lambda a: a.reshape(bs, ts, a.shape[-1])
    tail0 = jnp.pad(state_conv[l], ((0, 0), (SUBLANES - (CONV_W - 1), 0), (0, 0)))
    m0b = jnp.broadcast_to(state_m[l][:, :, None], (bs, MLSTM_HEADS, LANES))
    hb_s, tail_s, sc, sn, sm = _mlstm(r3(raw), r3(vb), r3(ob), r3(gates), *conv_params, ts,
                                      MLSTM_SAMPLE_GROUP if bs % MLSTM_SAMPLE_GROUP == 0 else 1,
                                      state=(tail0, state_C[l], state_n[l], m0b))
    hb_s = hb_s.reshape(bs * ts, MLSTM_WIDTH)

    kc_t = cache_win_k[l].transpose(0, 2, 3, 1)
    vc_t = cache_win_v[l].transpose(0, 2, 3, 1)
    if _attn_mlp_fits(bs, bp * tp):
        att_s, nk_t, nv_t, y_p = _attn_mlp(q, k, v, kc_t, vc_t, ts, xp2d, att_p, hb_p, *mlp_params)
    else:
        att_s, nk_t, nv_t = _attn_sample(q, k, v, kc_t, vc_t, ts)
        y_p = _outmlp(xp2d, att_p, hb_p, *mlp_params, tm_p)
    sk, sv = nk_t.transpose(0, 3, 1, 2), nv_t.transpose(0, 3, 1, 2)
    y_s = _outmlp(xs2d, att_s, hb_s, *mlp_params, tm_s)

    first_tail = SUBLANES - (CONV_W - 1)
    outs = (y_p.reshape(bp, tp, D_MODEL), y_s.reshape(bs, ts, D_MODEL),
            pk, pv, tail_p[:, first_tail:, :], pc, pn, pm[:, :, 0],
            sk, sv, tail_s[:, first_tail:, :], sc, sn, sm[:, :, 0])
    return outs[:2] + tuple(o[None] for o in outs[2:])
```

```python
import functools

import jax
import jax.numpy as jnp
import numpy as np
from jax import lax
from jax.experimental import pallas as pl
from jax.experimental.pallas import tpu as pltpu

D_MODEL = 1024
HD = 64
N_ATT_HEADS = 8
ATT_WIDTH = N_ATT_HEADS * HD
MLSTM_HEADS = 4
MLSTM_WIDTH = D_MODEL - ATT_WIDTH
MLSTM_HD = MLSTM_WIDTH // MLSTM_HEADS
ROT_DIM = HD // 4
ROT_HALF = ROT_DIM // 2
ROPE_THETA = 500000.0
DILATIONS = (1, 4, 16)
SUB_WINDOW = 128
ATT_BLOCK = 128
ATT_UNROLL = 8
MLSTM_SAMPLE_GROUP = 8
MLSTM_PROMPT_GROUP = 4
FF_PARTS = 4
ATT_HEAD_SPLIT = 2
CONV_W = 4
CHUNK = 128
D_FF = 4 * D_MODEL
EPS = 1e-6
PAST_LEN = 8192
IN_SIZES = (ATT_WIDTH, ATT_WIDTH, ATT_WIDTH, 2 * MLSTM_WIDTH, MLSTM_WIDTH, MLSTM_WIDTH, 2 * MLSTM_HEADS)
IN_COLS = sum(IN_SIZES)

LANES = 128
SUBLANES = 8
GATE_PAD = LANES
VMEM_LIMIT = 56 * 1024 * 1024

F32 = jnp.float32
BF16 = jnp.bfloat16
NEG_INF = float("-inf")
LOG2E = 1.4426950408889634


def _cparams(sem):
    return pltpu.CompilerParams(dimension_semantics=sem, vmem_limit_bytes=VMEM_LIMIT)


def _const_spec(shape):
    nd = len(shape)
    return pl.BlockSpec(shape, lambda *_: (0,) * nd, pipeline_mode=pl.Buffered(1))


def _rms(x, g):
    return x * lax.rsqrt(jnp.mean(x * x, axis=-1, keepdims=True) + EPS) * g


def _inproj_kernel(x_ref, g_ref, w_ref, wg_ref, bg_ref, cos_ref, sa_ref, sb_ref,
                   q_ref, k_ref, v_ref, raw_ref, vb_ref, ob_ref, gate_ref, *win_refs,
                   tiles_per_seq, window_tiles):
    h = _rms(x_ref[...], g_ref[...]).astype(BF16)

    def proj(lo, width):
        return jnp.dot(h, w_ref[:, lo:lo + width], preferred_element_type=F32)

    cos, sa, sb = cos_ref[...], sa_ref[...], sb_ref[...]

    def rotary_store(dst, y, scale):
        for c in range(ATT_WIDTH // LANES):
            yc = y[:, c * LANES:(c + 1) * LANES]
            up = pltpu.roll(yc, LANES - ROT_HALF, 1)
            dn = pltpu.roll(yc, ROT_HALF, 1)
            r = yc * cos + up * sa + dn * sb
            dst[:, c * LANES:(c + 1) * LANES] = r * scale if scale != 1.0 else r

    off = 0
    rotary_store(q_ref, proj(off, ATT_WIDTH), HD ** -0.5)
    off += ATT_WIDTH
    rotary_store(k_ref, proj(off, ATT_WIDTH), 1.0)
    off += ATT_WIDTH
    v_ref[...] = proj(off, ATT_WIDTH)
    off += ATT_WIDTH
    raw_ref[...] = proj(off, 2 * MLSTM_WIDTH)
    off += 2 * MLSTM_WIDTH
    vb_ref[...] = proj(off, MLSTM_WIDTH)
    off += MLSTM_WIDTH
    ob_ref[...] = proj(off, MLSTM_WIDTH)
    gate_ref[...] = jnp.dot(h, wg_ref[...], preferred_element_type=F32) + bg_ref[...]

    if window_tiles:
        kt_ref, vt_ref = win_refs

        @pl.when(pl.program_id(0) % tiles_per_seq >= tiles_per_seq - window_tiles)
        def _():
            kt_ref[...] = jnp.transpose(k_ref[...])
            vt_ref[...] = jnp.transpose(v_ref[...])


def _inproj(x2d, g1, w_main_bf, w_gate_bf, bg_pad, tables, tm, table_blocks, tiles_per_seq=1, window_tiles=0):
    n = x2d.shape[0]
    grid = (n // tm,)
    row = lambda i: (i, 0)
    tab = lambda i: (i % table_blocks, 0)
    widths = (ATT_WIDTH, ATT_WIDTH, ATT_WIDTH, 2 * MLSTM_WIDTH, MLSTM_WIDTH, MLSTM_WIDTH, GATE_PAD)
    out_specs = [pl.BlockSpec((tm, w), row) for w in widths]
    out_shape = [jax.ShapeDtypeStruct((n, w), F32) for w in widths]
    if window_tiles:
        first = tiles_per_seq - window_tiles
        win = pl.BlockSpec((None, ATT_WIDTH, tm),
                           lambda i: (i // tiles_per_seq, 0, jnp.maximum(i % tiles_per_seq - first, 0)))
        out_specs += [win, win]
        out_shape += [jax.ShapeDtypeStruct((n // (tm * tiles_per_seq), ATT_WIDTH, window_tiles * tm), F32)] * 2
    return pl.pallas_call(
        functools.partial(_inproj_kernel, tiles_per_seq=tiles_per_seq, window_tiles=window_tiles),
        grid=grid,
        in_specs=[
            pl.BlockSpec((tm, D_MODEL), row),
            _const_spec((1, D_MODEL)),
            _const_spec((D_MODEL, IN_COLS - 2 * MLSTM_HEADS)),
            _const_spec((D_MODEL, GATE_PAD)),
            _const_spec((1, GATE_PAD)),
            pl.BlockSpec((tm, LANES), tab),
            pl.BlockSpec((tm, LANES), tab),
            pl.BlockSpec((tm, LANES), tab),
        ],
        out_specs=out_specs,
        out_shape=out_shape,
        compiler_params=_cparams(("arbitrary",)),
        name="inproj",
    )(x2d, g1, w_main_bf, w_gate_bf, bg_pad, *tables)


def _rotary_tables(pos):
    half = np.arange(ROT_HALF, dtype=np.float64)
    ang = np.asarray(pos, np.float64)[:, None] * (ROPE_THETA ** (-half / ROT_HALF))[None, :]
    cos, sin = np.cos(ang), np.sin(ang)
    p = ang.shape[0]
    ones = np.ones((p, HD - ROT_DIM))
    zeros = np.zeros((p, HD - ROT_DIM))
    z8 = np.zeros((p, ROT_HALF))
    cos_h = np.concatenate([cos, cos, ones], axis=1)
    sa_h = np.concatenate([-sin, z8, zeros], axis=1)
    sb_h = np.concatenate([z8, sin, zeros], axis=1)
    two = lambda t: np.concatenate([t, t], axis=1).astype(np.float32)
    return two(cos_h), two(sa_h), two(sb_h)


def _attn_prompt_kernel(bias_ref, q_ref, k_ref, v_ref, o_ref,
                        q4_ref, k4_ref, v4_ref, num1_ref, m1_ref, den1_ref, num4_ref, m4_ref, den4_ref,
                        *, seq):
    d_mid, d_out = DILATIONS[1], DILATIONS[2]
    d_in = d_out // d_mid
    lane = lax.broadcasted_iota(jnp.int32, (ATT_BLOCK, LANES), 1)
    head0 = lane < HD
    nblk = seq // ATT_BLOCK
    sub_blocks = nblk // d_mid
    ones_cols = jnp.ones((2 * ATT_BLOCK, LANES), BF16)

    def scores(q, kk, first):
        q = q * LOG2E
        q2 = jnp.concatenate([jnp.where(head0, q, 0.0), jnp.where(head0, 0.0, q)], axis=0).astype(BF16)
        s = lax.dot_general(q2, kk.astype(BF16), (((1,), (1,)), ((), ())),
                            preferred_element_type=F32)
        return s + bias_ref[first]

    def weighted_values(p, vv):
        o2 = jnp.dot(p.astype(BF16), jnp.concatenate([vv.astype(BF16), ones_cols], axis=1),
                     preferred_element_type=F32)
        o = jnp.where(head0, o2[:ATT_BLOCK, :LANES], o2[ATT_BLOCK:, :LANES])
        den = jnp.where(head0, o2[:ATT_BLOCK, LANES:], o2[ATT_BLOCK:, LANES:])
        return o, den

    def aligned(start):
        return pl.ds(pl.multiple_of(start, ATT_BLOCK), ATT_BLOCK)

    def split(i, carry):
        r, c = i // sub_blocks, i % sub_blocks
        src = pl.ds(r + c * ATT_BLOCK * d_mid, ATT_BLOCK, stride=d_mid)
        dst = aligned(i * ATT_BLOCK)
        q4_ref[dst, :] = q_ref[src, :]
        k4_ref[dst, :] = k_ref[src, :]
        v4_ref[dst, :] = v_ref[src, :]
        return carry

    lax.fori_loop(0, nblk, split, 0)

    def run_group(srcs, blocks_per_class, rows_of, merge, unroll=ATT_UNROLL):
        qs, ks, vs = srcs

        def body(i, carry):
            blocks = []
            for u in range(unroll):
                idx = i * unroll + u
                n = idx % blocks_per_class
                rows = rows_of(idx, n)
                prows = rows_of(idx - jnp.minimum(n, 1), jnp.maximum(n - 1, 0))
                first = jnp.where(n == 0, 1, 0)
                kk = jnp.concatenate([ks[prows, :], ks[rows, :]], axis=0)
                blocks.append([rows, prows, scores(qs[rows, :], kk, first)])
            for blk in blocks:
                s = blk[2]
                mx = jnp.max(s, axis=1, keepdims=True)
                blk[2] = jnp.exp2(s - mx)
                blk.append(jnp.where(head0, mx[:ATT_BLOCK], mx[ATT_BLOCK:]))
            for blk in blocks:
                rows, prows, p, _ = blk
                vv = jnp.concatenate([vs[prows, :], vs[rows, :]], axis=0)
                blk[2:3] = weighted_values(p, vv)
            for rows, _, o, den, mxb in blocks:
                merge(rows, o, mxb, den)
            return carry

        lax.fori_loop(0, nblk // unroll, body, 0)

    def init_state(num_ref, m_ref, den_ref):
        def merge(rows, o, mxb, den):
            num_ref[rows, :] = o
            m_ref[rows, :] = mxb
            den_ref[rows, :] = den
        return merge

    def merge4(rows, o, mxb, den):
        m_old = m4_ref[rows, :]
        m_new = jnp.maximum(m_old, mxb)
        a = jnp.exp2(m_old - m_new)
        b = jnp.exp2(mxb - m_new)
        num4_ref[rows, :] = num4_ref[rows, :] * a + o * b
        den4_ref[rows, :] = den4_ref[rows, :] * a + den * b
        m4_ref[rows, :] = m_new

    run_group((q_ref, k_ref, v_ref), nblk, lambda idx, n: aligned(idx * ATT_BLOCK),
              init_state(num1_ref, m1_ref, den1_ref))
    blocks16 = nblk // d_out

    def rows16(idx, n):
        cls = idx // blocks16
        r, j = cls // d_in, cls % d_in
        return pl.ds(r * (sub_blocks * ATT_BLOCK) + j + n * ATT_BLOCK * d_in, ATT_BLOCK, stride=d_in)

    run_group((q4_ref, k4_ref, v4_ref), blocks16, rows16, init_state(num4_ref, m4_ref, den4_ref))
    run_group((q4_ref, k4_ref, v4_ref), sub_blocks, lambda idx, n: aligned(idx * ATT_BLOCK), merge4)

    def finish(i, carry):
        r, c = i // sub_blocks, i % sub_blocks
        tok = pl.ds(r + c * ATT_BLOCK * d_mid, ATT_BLOCK, stride=d_mid)
        res = aligned(i * ATT_BLOCK)
        m_a, m_b = m4_ref[res, :], m1_ref[tok, :]
        m_new = jnp.maximum(m_a, m_b)
        wa = jnp.exp2(m_a - m_new)
        wb = jnp.exp2(m_b - m_new)
        num = num4_ref[res, :] * wa + num1_ref[tok, :] * wb
        den = den4_ref[res, :] * wa + den1_ref[tok, :] * wb
        o_ref[tok, :] = num / den
        return carry

    lax.fori_loop(0, nblk, finish, 0, unroll=4)


def _band_bias():
    row = np.arange(2 * ATT_BLOCK)[:, None] % ATT_BLOCK
    ki = np.arange(2 * ATT_BLOCK)[None, :] - ATT_BLOCK
    rel = row - ki
    valid = (rel >= 0) & (rel <= SUB_WINDOW)
    b0 = np.where(valid, 0.0, NEG_INF).astype(np.float32)
    b1 = np.where(valid & (ki >= 0), 0.0, NEG_INF).astype(np.float32)
    return jnp.asarray(np.stack([b0, b1]))


def _attn_prompt(q, k, v):
    b, s, _ = q.shape
    assert DILATIONS[0] == 1 and DILATIONS[2] % DILATIONS[1] == 0
    assert s % (ATT_BLOCK * DILATIONS[-1]) == 0 and (s // ATT_BLOCK) % ATT_UNROLL == 0
    blk = pl.BlockSpec((None, s, LANES), lambda i, j: (i, 0, j))
    return pl.pallas_call(
        functools.partial(_attn_prompt_kernel, seq=s),
        grid=(b, ATT_WIDTH // LANES),
        in_specs=[_const_spec((2, 2 * ATT_BLOCK, 2 * ATT_BLOCK)), blk, blk, blk],
        out_specs=blk,
        out_shape=jax.ShapeDtypeStruct((b, s, ATT_WIDTH), F32),
        scratch_shapes=[pltpu.VMEM((s, LANES), F32)] * 9,
        compiler_params=_cparams(("parallel", "parallel")),
        name="attn_prompt",
    )(_band_bias(), q, k, v)


def _attn_sample_body(mw_ref, mn_ref, q_ref, kn_ref, vn_ref, kc_ref, vc_ref,
                      att_ref, nk_ref, nv_ref, *, win, t_new, heads):
    mult_w = mw_ref[...]
    mult_n = mn_ref[...]
    lane = lax.broadcasted_iota(jnp.int32, (HD, LANES), 1)
    keep = lane < LANES - t_new
    top = jnp.zeros((LANES - t_new, heads * HD), F32)
    knt_all = jnp.transpose(jnp.concatenate([top, kn_ref[...]], axis=0))
    vnt_all = jnp.transpose(jnp.concatenate([top, vn_ref[...]], axis=0))
    head_rows = lambda a, h: a[h * HD:(h + 1) * HD, :]
    for h in range(heads):
        for src_ref, new_all, dst in ((kc_ref, knt_all, nk_ref), (vc_ref, vnt_all, nv_ref)):
            rolled = pltpu.roll(src_ref[h], win - t_new, 1)
            dst[h, :, 0:win - LANES] = rolled[:, 0:win - LANES]
            dst[h, :, win - LANES:win] = jnp.where(keep, rolled[:, win - LANES:win], head_rows(new_all, h))
    sc = []
    for h in range(heads):
        qh = q_ref[:, h * HD:(h + 1) * HD].astype(BF16)
        s_w = jnp.dot(qh, kc_ref[h].astype(BF16), preferred_element_type=F32)
        s_n = jnp.dot(qh, head_rows(knt_all, h).astype(BF16), preferred_element_type=F32)
        sc.append((jnp.where(mult_w > 0, s_w, NEG_INF), jnp.where(mult_n > 0, s_n, NEG_INF)))
    pr = []
    for s_w, s_n in sc:
        mx = jnp.maximum(jnp.max(s_w, axis=1, keepdims=True), jnp.max(s_n, axis=1, keepdims=True))
        p_w = mult_w * jnp.exp(s_w - mx)
        p_n = mult_n * jnp.exp(s_n - mx)
        den = jnp.sum(p_w, axis=1, keepdims=True) + jnp.sum(p_n, axis=1, keepdims=True)
        pr.append((p_w, p_n, den))
    for h, (p_w, p_n, den) in enumerate(pr):
        o = lax.dot_general(p_w.astype(BF16), vc_ref[h].astype(BF16), (((1,), (1,)), ((), ())),
                            preferred_element_type=F32)
        o = o + lax.dot_general(p_n.astype(BF16), head_rows(vnt_all, h).astype(BF16),
                                (((1,), (1,)), ((), ())), preferred_element_type=F32)
        att_ref[:, h * HD:(h + 1) * HD] = o / den


def _attn_sample_kernel(*refs, win, t_new):
    _attn_sample_body(*refs, win=win, t_new=t_new, heads=N_ATT_HEADS)


def _attn_mlp_kernel(mw_ref, mn_ref, q_ref, kn_ref, vn_ref, kc_ref, vc_ref,
                     x_ref, attp_ref, hbp_ref, wo_ref, g2_ref, wu_ref, wd_ref, gf_ref,
                     att_ref, nk_ref, nv_ref, y_ref, h2_sc, acc_sc, *, win, t_new, heads):
    c = pl.program_id(0) % FF_PARTS

    @pl.when(c == 0)
    def _():
        mix = (jnp.dot(attp_ref[...].astype(BF16), wo_ref[0:ATT_WIDTH, :], preferred_element_type=F32)
               + jnp.dot(hbp_ref[...].astype(BF16), wo_ref[ATT_WIDTH:D_MODEL, :], preferred_element_type=F32))
        x1 = x_ref[...] + mix
        h2_sc[...] = _rms(x1, g2_ref[...]).astype(BF16)
        acc_sc[...] = x1

    u = jnp.dot(h2_sc[...], wu_ref[c], preferred_element_type=F32)
    r = jnp.maximum(u, 0.0)
    acc_sc[...] += jnp.dot((r * r).astype(BF16), wd_ref[c], preferred_element_type=F32)
    _attn_sample_body(mw_ref, mn_ref, q_ref, kn_ref, vn_ref, kc_ref, vc_ref, att_ref, nk_ref, nv_ref,
                      win=win, t_new=t_new, heads=heads)

    @pl.when(c == FF_PARTS - 1)
    def _():
        y_ref[...] = _rms(acc_sc[...], gf_ref[...])


def _sample_multiplicity(win, t_new):
    t = np.arange(t_new)[:, None]
    idx = np.arange(win + t_new)[None, :]
    back = win + t - idx
    mult = np.zeros((t_new, win + t_new), np.float32)
    for dil in DILATIONS:
        mult += ((back >= 0) & (back % dil == 0) & (back // dil <= SUB_WINDOW)).astype(np.float32)
    mw = mult[:, :win]
    mn = np.zeros((t_new, LANES), np.float32)
    mn[:, LANES - t_new:] = mult[:, win:]
    return jnp.asarray(mw), jnp.asarray(mn)


def _attn_sample(q, k_new, v_new, kc_t, vc_t, t_new):
    b, nh, _, win = kc_t.shape
    assert win == DILATIONS[-1] * SUB_WINDOW and t_new == SUBLANES and q.shape[0] == b * t_new
    mw, mn = _sample_multiplicity(win, t_new)
    rows = pl.BlockSpec((t_new, ATT_WIDTH), lambda i: (i, 0))
    per_b = lambda *shape: pl.BlockSpec((None,) + shape, lambda i: (i,) + (0,) * len(shape))
    return pl.pallas_call(
        functools.partial(_attn_sample_kernel, win=win, t_new=t_new),
        grid=(b,),
        in_specs=[
            _const_spec((t_new, win)), _const_spec((t_new, LANES)),
            rows, rows, rows, per_b(nh, HD, win), per_b(nh, HD, win),
        ],
        out_specs=[rows, per_b(nh, HD, win), per_b(nh, HD, win)],
        out_shape=[
            jax.ShapeDtypeStruct((b * t_new, ATT_WIDTH), F32),
            jax.ShapeDtypeStruct((b, nh, HD, win), F32),
            jax.ShapeDtypeStruct((b, nh, HD, win), F32),
        ],
        compiler_params=_cparams(("parallel",)),
        name="attn_sample",
    )(mw, mn, q, k_new, v_new, kc_t, vc_t)


def _attn_mlp_fits(b_s, n_p):
    steps = b_s * ATT_HEAD_SPLIT
    if steps % FF_PARTS:
        return False
    tiles = steps // FF_PARTS
    return n_p % tiles == 0 and (n_p // tiles) % SUBLANES == 0 and n_p // tiles <= 512


def _attn_mlp(q, k_new, v_new, kc_t, vc_t, t_new, x2d, attp, hbp, wo_bf, g2, wu_parts, wd_parts, gf):
    b, nh, _, win = kc_t.shape
    n_p = x2d.shape[0]
    assert win == DILATIONS[-1] * SUB_WINDOW and t_new == SUBLANES and q.shape[0] == b * t_new
    heads = nh // ATT_HEAD_SPLIT
    steps = b * ATT_HEAD_SPLIT
    tm = n_p // (steps // FF_PARTS)
    ffc = D_FF // FF_PARTS
    mw, mn = _sample_multiplicity(win, t_new)
    rows = pl.BlockSpec((t_new, heads * HD), lambda i: (i // ATT_HEAD_SPLIT, i % ATT_HEAD_SPLIT))
    win_blk = pl.BlockSpec((None, heads, HD, win), lambda i: (i // ATT_HEAD_SPLIT, i % ATT_HEAD_SPLIT, 0, 0))
    tile = lambda w: pl.BlockSpec((tm, w), lambda i: (i // FF_PARTS, 0))
    return pl.pallas_call(
        functools.partial(_attn_mlp_kernel, win=win, t_new=t_new, heads=heads),
        grid=(steps,),
        in_specs=[
            _const_spec((t_new, win)), _const_spec((t_new, LANES)),
            rows, rows, rows, win_blk, win_blk,
            tile(D_MODEL), tile(ATT_WIDTH), tile(MLSTM_WIDTH),
            _const_spec((D_MODEL, D_MODEL)), _const_spec((1, D_MODEL)),
            _const_spec((FF_PARTS, D_MODEL, ffc)), _const_spec((FF_PARTS, ffc, D_MODEL)),
            _const_spec((1, D_MODEL)),
        ],
        out_specs=[rows, win_blk, win_blk, tile(D_MODEL)],
        out_shape=[
            jax.ShapeDtypeStruct((b * t_new, ATT_WIDTH), F32),
            jax.ShapeDtypeStruct((b, nh, HD, win), F32),
            jax.ShapeDtypeStruct((b, nh, HD, win), F32),
            jax.ShapeDtypeStruct((n_p, D_MODEL), F32),
        ],
        scratch_shapes=[pltpu.VMEM((tm, D_MODEL), BF16), pltpu.VMEM((tm, D_MODEL), F32)],
        compiler_params=_cparams(("arbitrary",)),
        name="attn_mlp",
    )(mw, mn, q, k_new, v_new, kc_t, vc_t, x2d, attp, hbp, wo_bf, g2, wu_parts, wd_parts, gf)


def _split_dot(tri, x):
    hi = x.astype(BF16)
    lo = (x - hi.astype(F32)).astype(BF16)
    return (jnp.dot(tri, hi, preferred_element_type=F32)
            + jnp.dot(tri, lo, preferred_element_type=F32))


def _split_dot_rows(x, tri):
    hi = x.astype(BF16)
    lo = (x - hi.astype(F32)).astype(BF16)
    return (jnp.dot(hi, tri, preferred_element_type=F32)
            + jnp.dot(lo, tri, preferred_element_type=F32))


def _log_sigmoid(x):
    return jnp.minimum(x, 0.0) - jnp.log1p(jnp.exp(-jnp.abs(x)))


def _mlstm_kernel(*refs, chunk, group, has_state):
    if has_state:
        (raw_ref, vb_ref, ob_ref, gate_ref, wc_ref, bc_ref, g_ref, tail_ref, c0_ref, n0_ref, m0_ref,
         hb_ref, tail_out_ref, c_out_ref, n_out_ref, m_out_ref, xp_ref, act_ref, c_sc, n_sc, m_sc) = refs
        nm_sc = None
    else:
        (raw_ref, vb_ref, ob_ref, gate_ref, wc_ref, bc_ref, g_ref,
         hb_ref, tail_out_ref, c_out_ref, n_out_ref, m_out_ref, xp_ref, act_ref, c_sc, n_sc, m_sc, nm_sc) = refs
    L = chunk
    c_idx = pl.program_id(1)
    mxu_sums = nm_sc is not None and L == LANES

    @pl.when(c_idx == 0)
    def _():
        if has_state:
            for c in range(2 * MLSTM_HEADS):
                xp_ref[:, c, 0:SUBLANES, :] = tail_ref[:, :, c * LANES:(c + 1) * LANES]
            c_sc[...] = c0_ref[...]
            n_sc[...] = n0_ref[...]
            m_sc[...] = m0_ref[...]
        else:
            xp_ref[:, :, 0:SUBLANES, :] = jnp.zeros((group, 2 * MLSTM_HEADS, SUBLANES, LANES), F32)
            c_sc[...] = jnp.zeros_like(c_sc)
            n_sc[...] = jnp.zeros_like(n_sc)
            m_sc[...] = jnp.zeros_like(m_sc)
            nm_sc[...] = jnp.zeros_like(nm_sc)

    r_i = lax.broadcasted_iota(jnp.int32, (L, L), 0)
    c_i = lax.broadcasted_iota(jnp.int32, (L, L), 1)
    causal = r_i >= c_i
    tri_l = jnp.where(causal, 1.0, 0.0).astype(BF16)
    tri_u = jnp.where(r_i <= c_i, 1.0, 0.0).astype(BF16)
    first_tap = SUBLANES - (CONV_W - 1)
    bc = bc_ref[...]
    taps = [wc_ref[w:w + 1, :] for w in range(CONV_W)]
    ones_sq = jnp.ones((LANES, LANES), BF16)

    per_batch = []
    for g in range(group):
        tail_out_ref[g] = raw_ref[g, L - SUBLANES:L, :]
        for c in range(2 * MLSTM_HEADS):
            cl = slice(c * LANES, (c + 1) * LANES)
            raw = raw_ref[g, :, cl]
            xp_ref[g, c, SUBLANES:SUBLANES + L, :] = raw
            if L % (SUBLANES * SUBLANES) == 0:
                n_rows = L // SUBLANES
                slabs = [xp_ref[g, c, pl.ds(first_tap + j, n_rows, stride=SUBLANES), :]
                         for j in range(SUBLANES + CONV_W - 1)]
                for s in range(SUBLANES):
                    conv = bc[:, cl]
                    for w in range(CONV_W):
                        conv = conv + slabs[s + w] * taps[w][:, cl]
                    act_ref[g, c, pl.ds(s, n_rows, stride=SUBLANES), :] = conv * jax.nn.sigmoid(conv)
            else:
                conv = bc[:, cl]
                for w in range(CONV_W):
                    conv = conv + xp_ref[g, c, first_tap + w:first_tap + w + L, :] * taps[w][:, cl]
                act_ref[g, c] = conv * jax.nn.sigmoid(conv)
            xp_ref[g, c, 0:SUBLANES, :] = raw[L - SUBLANES:L, :]

        gates = gate_ref[g]
        lf = _log_sigmoid(gates)
        b_col = _split_dot(tri_l, lf)
        if L == LANES:
            gates_sq = gates
        else:
            gates_sq = jnp.concatenate([gates, jnp.zeros((LANES - L, LANES), F32)], axis=0)
        gates_t = jnp.transpose(gates_sq)[0:SUBLANES, 0:L]
        lf_t = _log_sigmoid(gates_t)
        b_row = _split_dot_rows(lf_t, tri_u)

        per_batch.append((gates, b_col, gates_t, b_row))

    items = [(g, h) for g in range(group) for h in range(MLSTM_HEADS)]
    st = {}
    for g, h in items:
        gates, b_col, gates_t, b_row = per_batch[g]
        ig_c = gates[:, h:h + 1]
        b_c = b_col[:, MLSTM_HEADS + h:MLSTM_HEADS + h + 1]
        ig_r = gates_t[h:h + 1, :]
        b_r = b_row[MLSTM_HEADS + h:MLSTM_HEADS + h + 1, :]
        if mxu_sums:
            b_c_wide = jnp.broadcast_to(b_c, (L, LANES))
            a_c = b_c_wide + m_sc[g, h:h + 1, 0:1]
            dm = jnp.where(causal, b_c_wide - b_r + ig_r, NEG_INF)
            mt = jnp.maximum(a_c, jnp.broadcast_to(jnp.max(dm, axis=1, keepdims=True), (L, LANES)))
        else:
            a_c = b_c + m_sc[g, h:h + 1, 0:1]
            dm = jnp.where(causal, b_c - b_r + ig_r, NEG_INF)
            mt = jnp.maximum(a_c, jnp.max(dm, axis=1, keepdims=True))
        st[g, h] = dict(ig_c=ig_c, b_c=b_c, a_c=a_c, dm=dm, mt=mt)
    for g, h in items:
        d = st[g, h]
        q = act_ref[g, h]
        k = act_ref[g, MLSTM_HEADS + h] * (MLSTM_HD ** -0.5)
        v = vb_ref[g, :, h * MLSTM_HD:(h + 1) * MLSTM_HD]
        qb, kb = q.astype(BF16), k.astype(BF16)
        w_inter = jnp.exp(d["a_c"] - d["mt"])
        wm = jnp.exp(d["dm"] - d["mt"])
        sc = lax.dot_general(qb, kb, (((1,), (1,)), ((), ())), preferred_element_type=F32) * wm
        d.update(q=q, k=k, v=v, qb=qb, vb=v.astype(BF16), w_inter=w_inter, sc=sc)
        del d["dm"]
    for g, h in items:
        d = st[g, h]
        c_old = c_sc[g, h]
        if mxu_sums:
            nm_old = nm_sc[g, h]
            v_ones = jnp.concatenate([d["vb"], ones_sq], axis=1)
            state = jnp.concatenate([c_old, nm_old], axis=1).astype(BF16)
            inter = jnp.dot(d["qb"], state, preferred_element_type=F32)
            intra = jnp.dot(d["sc"].astype(BF16), v_ones, preferred_element_type=F32)
            num = d["w_inter"] * inter[:, :LANES] + intra[:, :LANES]
            den = d["w_inter"] * inter[:, LANES:] + intra[:, LANES:]
            d.update(v_ones=v_ones, nm_old=nm_old)
        else:
            n_old = n_sc[g, h:h + 1, :]
            num = (d["w_inter"] * jnp.dot(d["qb"], c_old.astype(BF16), preferred_element_type=F32)
                   + jnp.dot(d["sc"].astype(BF16), d["vb"], preferred_element_type=F32))
            den = (d["w_inter"] * jnp.sum(d["q"] * n_old, axis=1, keepdims=True)
                   + jnp.sum(d["sc"], axis=1, keepdims=True))
            d.update(n_old=n_old)
        d.update(c_old=c_old, hh=num / jnp.maximum(jnp.abs(den), jnp.exp(-d["mt"])))
        del d["sc"]
    for g, h in items:
        d = st[g, h]
        mt, a_c, b_c = d["mt"], d["a_c"], d["b_c"]
        m_last = mt[L - 1:L, 0:1]
        w_last = jnp.exp(a_c[L - 1:L, 0:1] - m_last)
        w_t = jnp.exp(b_c[L - 1:L, :] - b_c + d["ig_c"] - m_last)
        kw = d["k"] * w_t
        if mxu_sums:
            upd = jnp.dot(jnp.transpose(kw).astype(BF16), d["v_ones"], preferred_element_type=F32)
            c_sc[g, h] = w_last * d["c_old"] + upd[:, :LANES]
            nm_sc[g, h] = w_last * d["nm_old"] + upd[:, LANES:]
        else:
            if L == LANES:
                kw_sq, v_sq = kw, d["v"]
            else:
                pad = jnp.zeros((LANES - L, MLSTM_HD), F32)
                kw_sq = jnp.concatenate([kw, pad], axis=0)
                v_sq = jnp.concatenate([d["v"], pad], axis=0)
            c_sc[g, h] = w_last * d["c_old"] + jnp.dot(jnp.transpose(kw_sq).astype(BF16), v_sq.astype(BF16),
                                                       preferred_element_type=F32)
            n_sc[g, h:h + 1, :] = w_last * d["n_old"] + jnp.sum(kw, axis=0, keepdims=True)
        m_sc[g, h:h + 1, :] = jnp.broadcast_to(m_last, (1, LANES))
    for g, h in items:
        hh = st[g, h]["hh"]
        sl = slice(h * MLSTM_HD, (h + 1) * MLSTM_HD)
        if mxu_sums:
            mean_sq = _split_dot_rows(hh * hh, ones_sq) * (1.0 / MLSTM_HD)
        else:
            mean_sq = jnp.mean(hh * hh, axis=1, keepdims=True)
        hn = hh * lax.rsqrt(mean_sq + EPS)
        hb_ref[g, :, sl] = hn * g_ref[:, sl] * jax.nn.sigmoid(ob_ref[g, :, sl])

    @pl.when(c_idx == pl.num_programs(1) - 1)
    def _():
        if mxu_sums:
            for g in range(group):
                for h in range(MLSTM_HEADS):
                    n_sc[g, h:h + 1, :] = jnp.transpose(nm_sc[g, h])[0:1, :]
        c_out_ref[...] = c_sc[...]
        n_out_ref[...] = n_sc[...]
        m_out_ref[...] = m_sc[...]


def _mlstm(raw, vb, ob, gates, w_conv, b_conv, mh_g, chunk, group, state=None):
    b, t, _ = raw.shape
    nc = t // chunk
    assert nc * chunk == t and chunk % SUBLANES == 0 and b % group == 0
    seq = lambda w: pl.BlockSpec((group, chunk, w), lambda i, j: (i, j, 0))
    per_b = lambda *shape: pl.BlockSpec((group,) + shape, lambda i, j: (i,) + (0,) * len(shape))
    in_specs = [seq(2 * MLSTM_WIDTH), seq(MLSTM_WIDTH), seq(MLSTM_WIDTH), seq(GATE_PAD),
                _const_spec((CONV_W, 2 * MLSTM_WIDTH)), _const_spec((1, 2 * MLSTM_WIDTH)),
                _const_spec((1, MLSTM_WIDTH))]
    args = [raw, vb, ob, gates, w_conv, b_conv, mh_g]
    state_shapes = [(SUBLANES, 2 * MLSTM_WIDTH), (MLSTM_HEADS, MLSTM_HD, MLSTM_HD),
                    (MLSTM_HEADS, MLSTM_HD), (MLSTM_HEADS, LANES)]
    state_specs = [per_b(*sh) for sh in state_shapes]
    if state is not None:
        in_specs += state_specs
        args += list(state)
    return pl.pallas_call(
        functools.partial(_mlstm_kernel, chunk=chunk, group=group, has_state=state is not None),
        grid=(b // group, nc),
        in_specs=in_specs,
        out_specs=[seq(MLSTM_WIDTH)] + state_specs,
        out_shape=[jax.ShapeDtypeStruct((b, t, MLSTM_WIDTH), F32)]
        + [jax.ShapeDtypeStruct((b,) + sh, F32) for sh in state_shapes],
        scratch_shapes=[
            pltpu.VMEM((group, 2 * MLSTM_HEADS, SUBLANES + chunk, LANES), F32),
            pltpu.VMEM((group, 2 * MLSTM_HEADS, chunk, LANES), F32),
        ] + [pltpu.VMEM((group,) + sh, F32) for sh in state_shapes[1:]]
        + ([] if state is not None else [pltpu.VMEM((group,) + state_shapes[1], F32)]),
        compiler_params=_cparams(("parallel", "arbitrary")),
        name="mlstm_state" if state is not None else "mlstm",
    )(*args)


def _outmlp_kernel(x_ref, att_ref, hb_ref, wo_ref, g2_ref, wu_ref, wd_ref, gf_ref, y_ref):
    mix = (jnp.dot(att_ref[...].astype(BF16), wo_ref[0:ATT_WIDTH, :], preferred_element_type=F32)
           + jnp.dot(hb_ref[...].astype(BF16), wo_ref[ATT_WIDTH:D_MODEL, :], preferred_element_type=F32))
    x1 = x_ref[...] + mix
    h2 = _rms(x1, g2_ref[...]).astype(BF16)
    acc = x1
    for c in range(FF_PARTS):
        u = jnp.dot(h2, wu_ref[c], preferred_element_type=F32)
        r = jnp.maximum(u, 0.0)
        acc = acc + jnp.dot((r * r).astype(BF16), wd_ref[c], preferred_element_type=F32)
    y_ref[...] = _rms(acc, gf_ref[...])


def _outmlp(x2d, att2d, hb2d, wo_bf, g2, wu_parts, wd_parts, gf, tm):
    n = x2d.shape[0]
    ffc = D_FF // FF_PARTS
    row = lambda w: pl.BlockSpec((tm, w), lambda i: (i, 0))
    return pl.pallas_call(
        _outmlp_kernel,
        grid=(n // tm,),
        in_specs=[row(D_MODEL), row(ATT_WIDTH), row(MLSTM_WIDTH),
                  _const_spec((D_MODEL, D_MODEL)), _const_spec((1, D_MODEL)),
                  _const_spec((FF_PARTS, D_MODEL, ffc)), _const_spec((FF_PARTS, ffc, D_MODEL)),
                  _const_spec((1, D_MODEL))],
        out_specs=row(D_MODEL),
        out_shape=jax.ShapeDtypeStruct((n, D_MODEL), F32),
        compiler_params=_cparams(("parallel",)),
        name="outmlp",
    )(x2d, att2d, hb2d, wo_bf, g2, wu_parts, wd_parts, gf)


def _project(x, pos, mix_params, tm, tile_tables, window=0):
    g1, w_main_bf, w_gate_bf, bg_pad = mix_params
    b, t, _ = x.shape
    x2d = x.reshape(b * t, D_MODEL)
    tm = min(tm, b * t)
    tables = _rotary_tables(pos)
    if tile_tables:
        tables = tuple(np.tile(a, (tm // t, 1)) for a in tables)
        table_blocks, tiles_per_seq = 1, 1
    else:
        table_blocks = tiles_per_seq = t // tm
    assert window % tm == 0
    outs = _inproj(x2d, g1, w_main_bf, w_gate_bf, bg_pad, tuple(jnp.asarray(a) for a in tables), tm,
                   table_blocks, tiles_per_seq, window // tm)
    return x2d, tm, outs


def kernel(x_prompt, x_sample, cache_win_k, cache_win_v, state_conv, state_C, state_n, state_m, norm1_g, w_in, b_gate, w_conv, b_conv, mh_norm_g, w_out, norm2_g, w_up, w_down, norm_f_g):
    depth = w_in.shape[0]
    assert depth == 1, "the final norm is fused into the (single) layer's MLP kernel"
    l = 0
    n_gate = 2 * MLSTM_HEADS
    w_l = w_in[l]
    w_main_bf = w_l[:, :IN_COLS - n_gate].astype(BF16)
    w_gate_bf = jnp.pad(w_l[:, IN_COLS - n_gate:], ((0, 0), (0, GATE_PAD - n_gate))).astype(BF16)
    bg_pad = jnp.pad(b_gate[l], (0, GATE_PAD - n_gate))[None, :]
    mix_params = (norm1_g[l][None, :], w_main_bf, w_gate_bf, bg_pad)
    conv_params = (w_conv[l], b_conv[l][None, :], mh_norm_g[l][None, :])
    ffc = D_FF // FF_PARTS
    wu_parts = w_up[l].reshape(D_MODEL, FF_PARTS, ffc).transpose(1, 0, 2).astype(BF16)
    wd_parts = w_down[l].astype(BF16).reshape(FF_PARTS, ffc, D_MODEL)
    mlp_params = (w_out[l].astype(BF16), norm2_g[l][None, :], wu_parts, wd_parts, norm_f_g[None, :])

    bp, tp, _ = x_prompt.shape
    n_keep = min(DILATIONS[-1] * SUB_WINDOW, tp)
    xp2d, tm_p, (q, k, v, raw, vb, ob, gates, kt, vt) = _project(x_prompt, np.arange(tp), mix_params, 512, False,
                                                               window=n_keep)
    r3 = lambda a: a.reshape(bp, tp, a.shape[-1])
    att_p = _attn_prompt(r3(q), r3(k), r3(v)).reshape(bp * tp, ATT_WIDTH)
    heads = lambda a: a.reshape(bp, N_ATT_HEADS, HD, n_keep).transpose(0, 3, 1, 2)
    pk, pv = heads(kt), heads(vt)
    hb_p, tail_p, pc, pn, pm = _mlstm(r3(raw), r3(vb), r3(ob), r3(gates), *conv_params,
                                      CHUNK if tp % CHUNK == 0 else tp,
                                      MLSTM_PROMPT_GROUP if bp % MLSTM_PROMPT_GROUP == 0 else 1)
    hb_p = hb_p.reshape(bp * tp, MLSTM_WIDTH)

    bs, ts, _ = x_sample.shape
    xs2d, tm_s, (q, k, v, raw, vb, ob, gates) = _project(x_sample, PAST_LEN + np.arange(ts), mix_params, 512, True)
    r3 = lambda a: a.reshape(bs, ts, a.shape[-1])
    tail0 = jnp.pad(state_conv[l], ((0, 0), (SUBLANES - (CONV_W - 1), 0), (0, 0)))
    m0b = jnp.broadcast_to(state_m[l][:, :, None], (bs, MLSTM_HEADS, LANES))
    hb_s, tail_s, sc, sn, sm = _mlstm(r3(raw), r3(vb), r3(ob), r3(gates), *conv_params, ts,
                                      MLSTM_SAMPLE_GROUP if bs % MLSTM_SAMPLE_GROUP == 0 else 1,
                                      state=(tail0, state_C[l], state_n[l], m0b))
    hb_s = hb_s.reshape(bs * ts, MLSTM_WIDTH)

    kc_t = cache_win_k[l].transpose(0, 2, 3, 1)
    vc_t = cache_win_v[l].transpose(0, 2, 3, 1)
    if _attn_mlp_fits(bs, bp * tp):
        att_s, nk_t, nv_t, y_p = _attn_mlp(q, k, v, kc_t, vc_t, ts, xp2d, att_p, hb_p, *mlp_params)
    else:
        att_s, nk_t, nv_t = _attn_sample(q, k, v, kc_t, vc_t, ts)
        y_p = _outmlp(xp2d, att_p, hb_p, *mlp_params, tm_p)
    sk, sv = nk_t.transpose(0, 3, 1, 2), nv_t.transpose(0, 3, 1, 2)
    y_s = _outmlp(xs2d, att_s, hb_s, *mlp_params, tm_s)

    first_tail = SUBLANES - (CONV_W - 1)
    outs = (y_p.reshape(bp, tp, D_MODEL), y_s.reshape(bs, ts, D_MODEL),
            pk, pv, tail_p[:, first_tail:, :], pc, pn, pm[:, :, 0],
            sk, sv, tail_s[:, first_tail:, :], sc, sn, sm[:, :, 0])
    return outs[:2] + tuple(o[None] for o in outs[2:])
```

```python
import functools

import jax
import jax.numpy as jnp
import numpy as np
from jax import lax
from jax.experimental import pallas as pl
from jax.experimental.pallas import tpu as pltpu

D_MODEL = 1024
HD = 64
N_ATT_HEADS = 8
ATT_WIDTH = N_ATT_HEADS * HD
MLSTM_HEADS = 4
MLSTM_WIDTH = D_MODEL - ATT_WIDTH
MLSTM_HD = MLSTM_WIDTH // MLSTM_HEADS
ROT_DIM = HD // 4
ROT_HALF = ROT_DIM // 2
ROPE_THETA = 500000.0
DILATIONS = (1, 4, 16)
SUB_WINDOW = 128
ATT_BLOCK = 128
ATT_UNROLL = 16
MLSTM_SAMPLE_GROUP = 8
MLSTM_PROMPT_GROUP = 4
FF_PARTS = 4
ATT_HEAD_SPLIT = 2
CONV_W = 4
CHUNK = 128
D_FF = 4 * D_MODEL
EPS = 1e-6
PAST_LEN = 8192
IN_SIZES = (ATT_WIDTH, ATT_WIDTH, ATT_WIDTH, 2 * MLSTM_WIDTH, MLSTM_WIDTH, MLSTM_WIDTH, 2 * MLSTM_HEADS)
IN_COLS = sum(IN_SIZES)

LANES = 128
SUBLANES = 8
GATE_PAD = LANES
VMEM_LIMIT = 56 * 1024 * 1024

F32 = jnp.float32
BF16 = jnp.bfloat16
NEG_INF = float("-inf")
LOG2E = 1.4426950408889634


def _cparams(sem):
    return pltpu.CompilerParams(dimension_semantics=sem, vmem_limit_bytes=VMEM_LIMIT)


def _const_spec(shape):
    nd = len(shape)
    return pl.BlockSpec(shape, lambda *_: (0,) * nd, pipeline_mode=pl.Buffered(1))


def _rms(x, g):
    return x * lax.rsqrt(jnp.mean(x * x, axis=-1, keepdims=True) + EPS) * g


def _inproj_kernel(x_ref, g_ref, w_ref, wg_ref, bg_ref, cos_ref, sa_ref, sb_ref,
                   q_ref, k_ref, v_ref, raw_ref, vb_ref, ob_ref, gate_ref, *win_refs,
                   tiles_per_seq, window_tiles):
    h = _rms(x_ref[...], g_ref[...]).astype(BF16)

    def proj(lo, width):
        return jnp.dot(h, w_ref[:, lo:lo + width], preferred_element_type=F32)

    cos, sa, sb = cos_ref[...], sa_ref[...], sb_ref[...]

    def rotary_store(dst, y, scale):
        for c in range(ATT_WIDTH // LANES):
            yc = y[:, c * LANES:(c + 1) * LANES]
            up = pltpu.roll(yc, LANES - ROT_HALF, 1)
            dn = pltpu.roll(yc, ROT_HALF, 1)
            r = yc * cos + up * sa + dn * sb
            dst[:, c * LANES:(c + 1) * LANES] = r * scale if scale != 1.0 else r

    off = 0
    rotary_store(q_ref, proj(off, ATT_WIDTH), HD ** -0.5)
    off += ATT_WIDTH
    rotary_store(k_ref, proj(off, ATT_WIDTH), 1.0)
    off += ATT_WIDTH
    v_ref[...] = proj(off, ATT_WIDTH)
    off += ATT_WIDTH
    raw_ref[...] = proj(off, 2 * MLSTM_WIDTH)
    off += 2 * MLSTM_WIDTH
    vb_ref[...] = proj(off, MLSTM_WIDTH)
    off += MLSTM_WIDTH
    ob_ref[...] = proj(off, MLSTM_WIDTH)
    gate_ref[...] = jnp.dot(h, wg_ref[...], preferred_element_type=F32) + bg_ref[...]

    if window_tiles:
        kt_ref, vt_ref = win_refs
        kt_ref[...] = jnp.transpose(k_ref[...])
        vt_ref[...] = jnp.transpose(v_ref[...])


def _inproj(x2d, g1, w_main_bf, w_gate_bf, bg_pad, tables, tm, table_blocks, tiles_per_seq=1, window_tiles=0):
    n = x2d.shape[0]
    grid = (n // tm,)
    row = lambda i: (i, 0)
    tab = lambda i: (i % table_blocks, 0)
    widths = (ATT_WIDTH, ATT_WIDTH, ATT_WIDTH, 2 * MLSTM_WIDTH, MLSTM_WIDTH, MLSTM_WIDTH, GATE_PAD)
    out_specs = [pl.BlockSpec((tm, w), row) for w in widths]
    out_shape = [jax.ShapeDtypeStruct((n, w), F32) for w in widths]
    if window_tiles:
        first = tiles_per_seq - window_tiles
        win = pl.BlockSpec((None, ATT_WIDTH, tm),
                           lambda i: (i // tiles_per_seq, 0, jnp.maximum(i % tiles_per_seq - first, 0)))
        out_specs += [win, win]
        out_shape += [jax.ShapeDtypeStruct((n // (tm * tiles_per_seq), ATT_WIDTH, window_tiles * tm), F32)] * 2
    return pl.pallas_call(
        functools.partial(_inproj_kernel, tiles_per_seq=tiles_per_seq, window_tiles=window_tiles),
        grid=grid,
        in_specs=[
            pl.BlockSpec((tm, D_MODEL), row),
            _const_spec((1, D_MODEL)),
            _const_spec((D_MODEL, IN_COLS - 2 * MLSTM_HEADS)),
            _const_spec((D_MODEL, GATE_PAD)),
            _const_spec((1, GATE_PAD)),
            pl.BlockSpec((tm, LANES), tab),
            pl.BlockSpec((tm, LANES), tab),
            pl.BlockSpec((tm, LANES), tab),
        ],
        out_specs=out_specs,
        out_shape=out_shape,
        compiler_params=_cparams(("arbitrary",)),
        name="inproj",
    )(x2d, g1, w_main_bf, w_gate_bf, bg_pad, *tables)


def _rotary_tables(pos):
    half = np.arange(ROT_HALF, dtype=np.float64)
    ang = np.asarray(pos, np.float64)[:, None] * (ROPE_THETA ** (-half / ROT_HALF))[None, :]
    cos, sin = np.cos(ang), np.sin(ang)
    p = ang.shape[0]
    ones = np.ones((p, HD - ROT_DIM))
    zeros = np.zeros((p, HD - ROT_DIM))
    z8 = np.zeros((p, ROT_HALF))
    cos_h = np.concatenate([cos, cos, ones], axis=1)
    sa_h = np.concatenate([-sin, z8, zeros], axis=1)
    sb_h = np.concatenate([z8, sin, zeros], axis=1)
    two = lambda t: np.concatenate([t, t], axis=1).astype(np.float32)
    return two(cos_h), two(sa_h), two(sb_h)


def _attn_prompt_kernel(bias_ref, q_ref, k_ref, v_ref, o_ref,
                        q4_ref, k4_ref, v4_ref, num1_ref, m1_ref, den1_ref, num4_ref, m4_ref, den4_ref,
                        kb_ref, vb_ref, k4b_ref, v4b_ref, *, seq):
    d_mid, d_out = DILATIONS[1], DILATIONS[2]
    d_in = d_out // d_mid
    lane = lax.broadcasted_iota(jnp.int32, (ATT_BLOCK, LANES), 1)
    head0 = lane < HD
    nblk = seq // ATT_BLOCK
    sub_blocks = nblk // d_mid
    ones_cols = jnp.ones((2 * ATT_BLOCK, LANES), BF16)

    def scores(q, kk, first):
        q = q * LOG2E
        q2 = jnp.concatenate([jnp.where(head0, q, 0.0), jnp.where(head0, 0.0, q)], axis=0).astype(BF16)
        s = lax.dot_general(q2, kk.astype(BF16), (((1,), (1,)), ((), ())),
                            preferred_element_type=F32)
        return s + bias_ref[first]

    def weighted_values(p, vv):
        o2 = jnp.dot(p.astype(BF16), jnp.concatenate([vv.astype(BF16), ones_cols], axis=1),
                     preferred_element_type=F32)
        o = jnp.where(head0, o2[:ATT_BLOCK, :LANES], o2[ATT_BLOCK:, :LANES])
        den = jnp.where(head0, o2[:ATT_BLOCK, LANES:], o2[ATT_BLOCK:, LANES:])
        return o, den

    def aligned(start):
        return pl.ds(pl.multiple_of(start, ATT_BLOCK), ATT_BLOCK)

    def split(i, carry):
        r, c = i // sub_blocks, i % sub_blocks
        src = pl.ds(r + c * ATT_BLOCK * d_mid, ATT_BLOCK, stride=d_mid)
        dst = aligned(i * ATT_BLOCK)
        q4_ref[dst, :] = q_ref[src, :]
        k4, v4 = k_ref[src, :], v_ref[src, :]
        k4_ref[dst, :] = k4
        v4_ref[dst, :] = v4
        k4b_ref[dst, :] = k4.astype(BF16)
        v4b_ref[dst, :] = v4.astype(BF16)
        kb_ref[dst, :] = k_ref[dst, :].astype(BF16)
        vb_ref[dst, :] = v_ref[dst, :].astype(BF16)
        return carry

    lax.fori_loop(0, nblk, split, 0)

    def run_group(srcs, blocks_per_class, rows_of, merge, unroll=ATT_UNROLL):
        qs, ks, vs = srcs

        def body(i, carry):
            blocks = []
            for u in range(unroll):
                idx = i * unroll + u
                n = idx % blocks_per_class
                rows = rows_of(idx, n)
                prows = rows_of(idx - jnp.minimum(n, 1), jnp.maximum(n - 1, 0))
                first = jnp.where(n == 0, 1, 0)
                kk = jnp.concatenate([ks[prows, :], ks[rows, :]], axis=0)
                blocks.append([rows, prows, scores(qs[rows, :], kk, first)])
            for blk in blocks:
                s = blk[2]
                mx = jnp.max(s, axis=1, keepdims=True)
                blk[2] = jnp.exp2(s - mx)
                blk.append(jnp.where(head0, mx[:ATT_BLOCK], mx[ATT_BLOCK:]))
            for blk in blocks:
                rows, prows, p, _ = blk
                vv = jnp.concatenate([vs[prows, :], vs[rows, :]], axis=0)
                blk[2:3] = weighted_values(p, vv)
            for rows, _, o, den, mxb in blocks:
                merge(rows, o, mxb, den)
            return carry

        lax.fori_loop(0, nblk // unroll, body, 0)

    def init_state(num_ref, m_ref, den_ref):
        def merge(rows, o, mxb, den):
            num_ref[rows, :] = o
            m_ref[rows, :] = mxb
            den_ref[rows, :] = den
        return merge

    def merge4(rows, o, mxb, den):
        m_old = m4_ref[rows, :]
        m_new = jnp.maximum(m_old, mxb)
        a = jnp.exp2(m_old - m_new)
        b = jnp.exp2(mxb - m_new)
        num4_ref[rows, :] = num4_ref[rows, :] * a + o * b
        den4_ref[rows, :] = den4_ref[rows, :] * a + den * b
        m4_ref[rows, :] = m_new

    run_group((q_ref, kb_ref, vb_ref), nblk, lambda idx, n: aligned(idx * ATT_BLOCK),
              init_state(num1_ref, m1_ref, den1_ref))
    blocks16 = nblk // d_out

    def rows16(idx, n):
        cls = idx // blocks16
        r, j = cls // d_in, cls % d_in
        return pl.ds(r * (sub_blocks * ATT_BLOCK) + j + n * ATT_BLOCK * d_in, ATT_BLOCK, stride=d_in)

    run_group((q4_ref, k4_ref, v4_ref), blocks16, rows16, init_state(num4_ref, m4_ref, den4_ref))
    run_group((q4_ref, k4b_ref, v4b_ref), sub_blocks, lambda idx, n: aligned(idx * ATT_BLOCK), merge4)

    def finish(i, carry):
        r, c = i // sub_blocks, i % sub_blocks
        tok = pl.ds(r + c * ATT_BLOCK * d_mid, ATT_BLOCK, stride=d_mid)
        res = aligned(i * ATT_BLOCK)
        m_a, m_b = m4_ref[res, :], m1_ref[tok, :]
        m_new = jnp.maximum(m_a, m_b)
        wa = jnp.exp2(m_a - m_new)
        wb = jnp.exp2(m_b - m_new)
        num = num4_ref[res, :] * wa + num1_ref[tok, :] * wb
        den = den4_ref[res, :] * wa + den1_ref[tok, :] * wb
        o_ref[tok, :] = num / den
        return carry

    lax.fori_loop(0, nblk, finish, 0, unroll=4)


def _band_bias():
    row = np.arange(2 * ATT_BLOCK)[:, None] % ATT_BLOCK
    ki = np.arange(2 * ATT_BLOCK)[None, :] - ATT_BLOCK
    rel = row - ki
    valid = (rel >= 0) & (rel <= SUB_WINDOW)
    b0 = np.where(valid, 0.0, NEG_INF).astype(np.float32)
    b1 = np.where(valid & (ki >= 0), 0.0, NEG_INF).astype(np.float32)
    return jnp.asarray(np.stack([b0, b1]))


def _attn_prompt(q, k, v):
    b, s, _ = q.shape
    assert DILATIONS[0] == 1 and DILATIONS[2] % DILATIONS[1] == 0
    assert s % (ATT_BLOCK * DILATIONS[-1]) == 0 and (s // ATT_BLOCK) % ATT_UNROLL == 0
    blk = pl.BlockSpec((None, s, LANES), lambda i, j: (i, 0, j))
    return pl.pallas_call(
        functools.partial(_attn_prompt_kernel, seq=s),
        grid=(b, ATT_WIDTH // LANES),
        in_specs=[_const_spec((2, 2 * ATT_BLOCK, 2 * ATT_BLOCK)), blk, blk, blk],
        out_specs=blk,
        out_shape=jax.ShapeDtypeStruct((b, s, ATT_WIDTH), F32),
        scratch_shapes=[pltpu.VMEM((s, LANES), F32)] * 9 + [pltpu.VMEM((s, LANES), BF16)] * 4,
        compiler_params=_cparams(("parallel", "parallel")),
        name="attn_prompt",
    )(_band_bias(), q, k, v)


def _attn_sample_body(mw_ref, mn_ref, q_ref, kn_ref, vn_ref, kc_ref, vc_ref,
                      att_ref, nk_ref, nv_ref, *, win, t_new, heads):
    mult_w = mw_ref[...]
    mult_n = mn_ref[...]
    lane = lax.broadcasted_iota(jnp.int32, (HD, LANES), 1)
    keep = lane < LANES - t_new
    top = jnp.zeros((LANES - t_new, heads * HD), F32)
    knt_all = jnp.transpose(jnp.concatenate([top, kn_ref[...]], axis=0))
    vnt_all = jnp.transpose(jnp.concatenate([top, vn_ref[...]], axis=0))
    head_rows = lambda a, h: a[h * HD:(h + 1) * HD, :]
    for h in range(heads):
        for src_ref, new_all, dst in ((kc_ref, knt_all, nk_ref), (vc_ref, vnt_all, nv_ref)):
            rolled = pltpu.roll(src_ref[h], win - t_new, 1)
            dst[h, :, 0:win - LANES] = rolled[:, 0:win - LANES]
            dst[h, :, win - LANES:win] = jnp.where(keep, rolled[:, win - LANES:win], head_rows(new_all, h))
    sc = []
    for h in range(heads):
        qh = q_ref[:, h * HD:(h + 1) * HD].astype(BF16)
        s_w = jnp.dot(qh, kc_ref[h].astype(BF16), preferred_element_type=F32)
        s_n = jnp.dot(qh, head_rows(knt_all, h).astype(BF16), preferred_element_type=F32)
        sc.append((jnp.where(mult_w > 0, s_w, NEG_INF), jnp.where(mult_n > 0, s_n, NEG_INF)))
    pr = []
    for s_w, s_n in sc:
        mx = jnp.maximum(jnp.max(s_w, axis=1, keepdims=True), jnp.max(s_n, axis=1, keepdims=True))
        p_w = mult_w * jnp.exp(s_w - mx)
        p_n = mult_n * jnp.exp(s_n - mx)
        den = jnp.sum(p_w, axis=1, keepdims=True) + jnp.sum(p_n, axis=1, keepdims=True)
        pr.append((p_w, p_n, den))
    for h, (p_w, p_n, den) in enumerate(pr):
        o = lax.dot_general(p_w.astype(BF16), vc_ref[h].astype(BF16), (((1,), (1,)), ((), ())),
                            preferred_element_type=F32)
        o = o + lax.dot_general(p_n.astype(BF16), head_rows(vnt_all, h).astype(BF16),
                                (((1,), (1,)), ((), ())), preferred_element_type=F32)
        att_ref[:, h * HD:(h + 1) * HD] = o / den


def _attn_sample_kernel(*refs, win, t_new):
    _attn_sample_body(*refs, win=win, t_new=t_new, heads=N_ATT_HEADS)


def _attn_mlp_kernel(mw_ref, mn_ref, q_ref, kn_ref, vn_ref, kc_ref, vc_ref,
                     x_ref, attp_ref, hbp_ref, wo_ref, g2_ref, wu_ref, wd_ref, gf_ref,
                     att_ref, nk_ref, nv_ref, y_ref, h2_sc, acc_sc, *, win, t_new, heads):
    c = pl.program_id(0) % FF_PARTS

    @pl.when(c == 0)
    def _():
        mix = (jnp.dot(attp_ref[...].astype(BF16), wo_ref[0:ATT_WIDTH, :], preferred_element_type=F32)
               + jnp.dot(hbp_ref[...].astype(BF16), wo_ref[ATT_WIDTH:D_MODEL, :], preferred_element_type=F32))
        x1 = x_ref[...] + mix
        h2_sc[...] = _rms(x1, g2_ref[...]).astype(BF16)
        acc_sc[...] = x1

    u = jnp.dot(h2_sc[...], wu_ref[c], preferred_element_type=F32)
    r = jnp.maximum(u, 0.0)
    acc_sc[...] += jnp.dot((r * r).astype(BF16), wd_ref[c], preferred_element_type=F32)
    _attn_sample_body(mw_ref, mn_ref, q_ref, kn_ref, vn_ref, kc_ref, vc_ref, att_ref, nk_ref, nv_ref,
                      win=win, t_new=t_new, heads=heads)

    @pl.when(c == FF_PARTS - 1)
    def _():
        y_ref[...] = _rms(acc_sc[...], gf_ref[...])


def _sample_multiplicity(win, t_new):
    t = np.arange(t_new)[:, None]
    idx = np.arange(win + t_new)[None, :]
    back = win + t - idx
    mult = np.zeros((t_new, win + t_new), np.float32)
    for dil in DILATIONS:
        mult += ((back >= 0) & (back % dil == 0) & (back // dil <= SUB_WINDOW)).astype(np.float32)
    mw = mult[:, :win]
    mn = np.zeros((t_new, LANES), np.float32)
    mn[:, LANES - t_new:] = mult[:, win:]
    return jnp.asarray(mw), jnp.asarray(mn)


def _attn_sample(q, k_new, v_new, kc_t, vc_t, t_new):
    b, nh, _, win = kc_t.shape
    assert win == DILATIONS[-1] * SUB_WINDOW and t_new == SUBLANES and q.shape[0] == b * t_new
    mw, mn = _sample_multiplicity(win, t_new)
    rows = pl.BlockSpec((t_new, ATT_WIDTH), lambda i: (i, 0))
    per_b = lambda *shape: pl.BlockSpec((None,) + shape, lambda i: (i,) + (0,) * len(shape))
    return pl.pallas_call(
        functools.partial(_attn_sample_kernel, win=win, t_new=t_new),
        grid=(b,),
        in_specs=[
            _const_spec((t_new, win)), _const_spec((t_new, LANES)),
            rows, rows, rows, per_b(nh, HD, win), per_b(nh, HD, win),
        ],
        out_specs=[rows, per_b(nh, HD, win), per_b(nh, HD, win)],
        out_shape=[
            jax.ShapeDtypeStruct((b * t_new, ATT_WIDTH), F32),
            jax.ShapeDtypeStruct((b, nh, HD, win), F32),
            jax.ShapeDtypeStruct((b, nh, HD, win), F32),
        ],
        compiler_params=_cparams(("parallel",)),
        name="attn_sample",
    )(mw, mn, q, k_new, v_new, kc_t, vc_t)


def _attn_mlp_fits(b_s, n_p):
    steps = b_s * ATT_HEAD_SPLIT
    if steps % FF_PARTS:
        return False
    tiles = steps // FF_PARTS
    return n_p % tiles == 0 and (n_p // tiles) % SUBLANES == 0 and n_p // tiles <= 512


def _attn_mlp(q, k_new, v_new, kc_t, vc_t, t_new, x2d, attp, hbp, wo_bf, g2, wu_parts, wd_parts, gf):
    b, nh, _, win = kc_t.shape
    n_p = x2d.shape[0]
    assert win == DILATIONS[-1] * SUB_WINDOW and t_new == SUBLANES and q.shape[0] == b * t_new
    heads = nh // ATT_HEAD_SPLIT
    steps = b * ATT_HEAD_SPLIT
    tm = n_p // (steps // FF_PARTS)
    ffc = D_FF // FF_PARTS
    mw, mn = _sample_multiplicity(win, t_new)
    rows = pl.BlockSpec((t_new, heads * HD), lambda i: (i // ATT_HEAD_SPLIT, i % ATT_HEAD_SPLIT))
    win_blk = pl.BlockSpec((None, heads, HD, win), lambda i: (i // ATT_HEAD_SPLIT, i % ATT_HEAD_SPLIT, 0, 0))
    tile = lambda w: pl.BlockSpec((tm, w), lambda i: (i // FF_PARTS, 0))
    return pl.pallas_call(
        functools.partial(_attn_mlp_kernel, win=win, t_new=t_new, heads=heads),
        grid=(steps,),
        in_specs=[
            _const_spec((t_new, win)), _const_spec((t_new, LANES)),
            rows, rows, rows, win_blk, win_blk,
            tile(D_MODEL), tile(ATT_WIDTH), tile(MLSTM_WIDTH),
            _const_spec((D_MODEL, D_MODEL)), _const_spec((1, D_MODEL)),
            _const_spec((FF_PARTS, D_MODEL, ffc)), _const_spec((FF_PARTS, ffc, D_MODEL)),
            _const_spec((1, D_MODEL)),
        ],
        out_specs=[rows, win_blk, win_blk, tile(D_MODEL)],
        out_shape=[
            jax.ShapeDtypeStruct((b * t_new, ATT_WIDTH), F32),
            jax.ShapeDtypeStruct((b, nh, HD, win), F32),
            jax.ShapeDtypeStruct((b, nh, HD, win), F32),
            jax.ShapeDtypeStruct((n_p, D_MODEL), F32),
        ],
        scratch_shapes=[pltpu.VMEM((tm, D_MODEL), BF16), pltpu.VMEM((tm, D_MODEL), F32)],
        compiler_params=_cparams(("arbitrary",)),
        name="attn_mlp",
    )(mw, mn, q, k_new, v_new, kc_t, vc_t, x2d, attp, hbp, wo_bf, g2, wu_parts, wd_parts, gf)


def _split_dot(tri, x):
    hi = x.astype(BF16)
    lo = (x - hi.astype(F32)).astype(BF16)
    return (jnp.dot(tri, hi, preferred_element_type=F32)
            + jnp.dot(tri, lo, preferred_element_type=F32))


def _split_dot_rows(x, tri):
    hi = x.astype(BF16)
    lo = (x - hi.astype(F32)).astype(BF16)
    return (jnp.dot(hi, tri, preferred_element_type=F32)
            + jnp.dot(lo, tri, preferred_element_type=F32))


def _log_sigmoid(x):
    return jnp.minimum(x, 0.0) - jnp.log1p(jnp.exp(-jnp.abs(x)))


def _mlstm_kernel(*refs, chunk, group, has_state):
    if has_state:
        (raw_ref, vb_ref, ob_ref, gate_ref, wc_ref, bc_ref, g_ref, tail_ref, c0_ref, n0_ref, m0_ref,
         hb_ref, tail_out_ref, c_out_ref, n_out_ref, m_out_ref, xp_ref, act_ref, c_sc, n_sc, m_sc) = refs
        nm_sc = None
    else:
        (raw_ref, vb_ref, ob_ref, gate_ref, wc_ref, bc_ref, g_ref,
         hb_ref, tail_out_ref, c_out_ref, n_out_ref, m_out_ref, xp_ref, act_ref, c_sc, n_sc, m_sc, nm_sc) = refs
    L = chunk
    c_idx = pl.program_id(1)
    mxu_sums = nm_sc is not None and L == LANES

    @pl.when(c_idx == 0)
    def _():
        if has_state:
            for c in range(2 * MLSTM_HEADS):
                xp_ref[:, c, 0:SUBLANES, :] = tail_ref[:, :, c * LANES:(c + 1) * LANES]
            c_sc[...] = c0_ref[...]
            n_sc[...] = n0_ref[...]
            m_sc[...] = m0_ref[...]
        else:
            xp_ref[:, :, 0:SUBLANES, :] = jnp.zeros((group, 2 * MLSTM_HEADS, SUBLANES, LANES), F32)
            c_sc[...] = jnp.zeros_like(c_sc)
            n_sc[...] = jnp.zeros_like(n_sc)
            m_sc[...] = jnp.zeros_like(m_sc)
            nm_sc[...] = jnp.zeros_like(nm_sc)

    r_i = lax.broadcasted_iota(jnp.int32, (L, L), 0)
    c_i = lax.broadcasted_iota(jnp.int32, (L, L), 1)
    causal = r_i >= c_i
    tri_l = jnp.where(causal, 1.0, 0.0).astype(BF16)
    tri_u = jnp.where(r_i <= c_i, 1.0, 0.0).astype(BF16)
    first_tap = SUBLANES - (CONV_W - 1)
    bc = bc_ref[...]
    taps = [wc_ref[w:w + 1, :] for w in range(CONV_W)]
    ones_sq = jnp.ones((LANES, LANES), BF16)

    per_batch = []
    for g in range(group):
        tail_out_ref[g] = raw_ref[g, L - SUBLANES:L, :]
        for c in range(2 * MLSTM_HEADS):
            cl = slice(c * LANES, (c + 1) * LANES)
            raw = raw_ref[g, :, cl]
            xp_ref[g, c, SUBLANES:SUBLANES + L, :] = raw
            if L % (SUBLANES * SUBLANES) == 0:
                n_rows = L // SUBLANES
                slabs = [xp_ref[g, c, pl.ds(first_tap + j, n_rows, stride=SUBLANES), :]
                         for j in range(SUBLANES + CONV_W - 1)]
                for s in range(SUBLANES):
                    conv = bc[:, cl]
                    for w in range(CONV_W):
                        conv = conv + slabs[s + w] * taps[w][:, cl]
                    act_ref[g, c, pl.ds(s, n_rows, stride=SUBLANES), :] = conv * jax.nn.sigmoid(conv)
            else:
                conv = bc[:, cl]
                for w in range(CONV_W):
                    conv = conv + xp_ref[g, c, first_tap + w:first_tap + w + L, :] * taps[w][:, cl]
                act_ref[g, c] = conv * jax.nn.sigmoid(conv)
            xp_ref[g, c, 0:SUBLANES, :] = raw[L - SUBLANES:L, :]

        gates = gate_ref[g]
        lf = _log_sigmoid(gates)
        b_col = _split_dot(tri_l, lf)
        if L == LANES:
            gates_sq = gates
        else:
            gates_sq = jnp.concatenate([gates, jnp.zeros((LANES - L, LANES), F32)], axis=0)
        gates_t = jnp.transpose(gates_sq)[0:SUBLANES, 0:L]
        lf_t = _log_sigmoid(gates_t)
        b_row = _split_dot_rows(lf_t, tri_u)

        per_batch.append((gates, b_col, gates_t, b_row))

    items = [(g, h) for g in range(group) for h in range(MLSTM_HEADS)]
    st = {}
    for g, h in items:
        gates, b_col, gates_t, b_row = per_batch[g]
        ig_c = gates[:, h:h + 1]
        b_c = b_col[:, MLSTM_HEADS + h:MLSTM_HEADS + h + 1]
        ig_r = gates_t[h:h + 1, :]
        b_r = b_row[MLSTM_HEADS + h:MLSTM_HEADS + h + 1, :]
        if mxu_sums:
            b_c_wide = jnp.broadcast_to(b_c, (L, LANES))
            a_c = b_c_wide + m_sc[g, h:h + 1, 0:1]
            dm = jnp.where(causal, b_c_wide - b_r + ig_r, NEG_INF)
            mt = jnp.maximum(a_c, jnp.broadcast_to(jnp.max(dm, axis=1, keepdims=True), (L, LANES)))
        else:
            a_c = b_c + m_sc[g, h:h + 1, 0:1]
            dm = jnp.where(causal, b_c - b_r + ig_r, NEG_INF)
            mt = jnp.maximum(a_c, jnp.max(dm, axis=1, keepdims=True))
        st[g, h] = dict(ig_c=ig_c, b_c=b_c, a_c=a_c, dm=dm, mt=mt)
    for g, h in items:
        d = st[g, h]
        q = act_ref[g, h]
        k = act_ref[g, MLSTM_HEADS + h] * (MLSTM_HD ** -0.5)
        v = vb_ref[g, :, h * MLSTM_HD:(h + 1) * MLSTM_HD]
        qb, kb = q.astype(BF16), k.astype(BF16)
        w_inter = jnp.exp(d["a_c"] - d["mt"])
        wm = jnp.exp(d["dm"] - d["mt"])
        sc = lax.dot_general(qb, kb, (((1,), (1,)), ((), ())), preferred_element_type=F32) * wm
        d.update(q=q, k=k, v=v, qb=qb, vb=v.astype(BF16), w_inter=w_inter, sc=sc)
        del d["dm"]
    for g, h in items:
        d = st[g, h]
        c_old = c_sc[g, h]
        if mxu_sums:
            nm_old = nm_sc[g, h]
            v_ones = jnp.concatenate([d["vb"], ones_sq], axis=1)
            state = jnp.concatenate([c_old, nm_old], axis=1).astype(BF16)
            inter = jnp.dot(d["qb"], state, preferred_element_type=F32)
            intra = jnp.dot(d["sc"].astype(BF16), v_ones, preferred_element_type=F32)
            num = d["w_inter"] * inter[:, :LANES] + intra[:, :LANES]
            den = d["w_inter"] * inter[:, LANES:] + intra[:, LANES:]
            d.update(v_ones=v_ones, nm_old=nm_old)
        else:
            n_old = n_sc[g, h:h + 1, :]
            num = (d["w_inter"] * jnp.dot(d["qb"], c_old.astype(BF16), preferred_element_type=F32)
                   + jnp.dot(d["sc"].astype(BF16), d["vb"], preferred_element_type=F32))
            den = (d["w_inter"] * jnp.sum(d["q"] * n_old, axis=1, keepdims=True)
                   + jnp.sum(d["sc"], axis=1, keepdims=True))
            d.update(n_old=n_old)
        d.update(c_old=c_old, hh=num / jnp.maximum(jnp.abs(den), jnp.exp(-d["mt"])))
        del d["sc"]
    for g, h in items:
        d = st[g, h]
        mt, a_c, b_c = d["mt"], d["a_c"], d["b_c"]
        m_last = mt[L - 1:L, 0:1]
        w_last = jnp.exp(a_c[L - 1:L, 0:1] - m_last)
        w_t = jnp.exp(b_c[L - 1:L, :] - b_c + d["ig_c"] - m_last)
        kw = d["k"] * w_t
        if mxu_sums:
            upd = jnp.dot(jnp.transpose(kw).astype(BF16), d["v_ones"], preferred_element_type=F32)
            c_sc[g, h] = w_last * d["c_old"] + upd[:, :LANES]
            nm_sc[g, h] = w_last * d["nm_old"] + upd[:, LANES:]
        else:
            if L == LANES:
                kw_sq, v_sq = kw, d["v"]
            else:
                pad = jnp.zeros((LANES - L, MLSTM_HD), F32)
                kw_sq = jnp.concatenate([kw, pad], axis=0)
                v_sq = jnp.concatenate([d["v"], pad], axis=0)
            c_sc[g, h] = w_last * d["c_old"] + jnp.dot(jnp.transpose(kw_sq).astype(BF16), v_sq.astype(BF16),
                                                       preferred_element_type=F32)
            n_sc[g, h:h + 1, :] = w_last * d["n_old"] + jnp.sum(kw, axis=0, keepdims=True)
        m_sc[g, h:h + 1, :] = jnp.broadcast_to(m_last, (1, LANES))
    for g, h in items:
        hh = st[g, h]["hh"]
        sl = slice(h * MLSTM_HD, (h + 1) * MLSTM_HD)
        if mxu_sums:
            mean_sq = _split_dot_rows(hh * hh, ones_sq) * (1.0 / MLSTM_HD)
        else:
            mean_sq = jnp.mean(hh * hh, axis=1, keepdims=True)
        hn = hh * lax.rsqrt(mean_sq + EPS)
        hb_ref[g, :, sl] = hn * g_ref[:, sl] * jax.nn.sigmoid(ob_ref[g, :, sl])

    @pl.when(c_idx == pl.num_programs(1) - 1)
    def _():
        if mxu_sums:
            for g in range(group):
                for h in range(MLSTM_HEADS):
                    n_sc[g, h:h + 1, :] = jnp.transpose(nm_sc[g, h])[0:1, :]
        c_out_ref[...] = c_sc[...]
        n_out_ref[...] = n_sc[...]
        m_out_ref[...] = m_sc[...]


def _mlstm(raw, vb, ob, gates, w_conv, b_conv, mh_g, chunk, group, state=None):
    b, t, _ = raw.shape
    nc = t // chunk
    assert nc * chunk == t and chunk % SUBLANES == 0 and b % group == 0
    seq = lambda w: pl.BlockSpec((group, chunk, w), lambda i, j: (i, j, 0))
    per_b = lambda *shape: pl.BlockSpec((group,) + shape, lambda i, j: (i,) + (0,) * len(shape))
    in_specs = [seq(2 * MLSTM_WIDTH), seq(MLSTM_WIDTH), seq(MLSTM_WIDTH), seq(GATE_PAD),
                _const_spec((CONV_W, 2 * MLSTM_WIDTH)), _const_spec((1, 2 * MLSTM_WIDTH)),
                _const_spec((1, MLSTM_WIDTH))]
    args = [raw, vb, ob, gates, w_conv, b_conv, mh_g]
    state_shapes = [(SUBLANES, 2 * MLSTM_WIDTH), (MLSTM_HEADS, MLSTM_HD, MLSTM_HD),
                    (MLSTM_HEADS, MLSTM_HD), (MLSTM_HEADS, LANES)]
    state_specs = [per_b(*sh) for sh in state_shapes]
    if state is not None:
        in_specs += state_specs
        args += list(state)
    return pl.pallas_call(
        functools.partial(_mlstm_kernel, chunk=chunk, group=group, has_state=state is not None),
        grid=(b // group, nc),
        in_specs=in_specs,
        out_specs=[seq(MLSTM_WIDTH)] + state_specs,
        out_shape=[jax.ShapeDtypeStruct((b, t, MLSTM_WIDTH), F32)]
        + [jax.ShapeDtypeStruct((b,) + sh, F32) for sh in state_shapes],
        scratch_shapes=[
            pltpu.VMEM((group, 2 * MLSTM_HEADS, SUBLANES + chunk, LANES), F32),
            pltpu.VMEM((group, 2 * MLSTM_HEADS, chunk, LANES), F32),
        ] + [pltpu.VMEM((group,) + sh, F32) for sh in state_shapes[1:]]
        + ([] if state is not None else [pltpu.VMEM((group,) + state_shapes[1], F32)]),
        compiler_params=_cparams(("parallel", "arbitrary")),
        name="mlstm_state" if state is not None else "mlstm",
    )(*args)


def _outmlp_kernel(x_ref, att_ref, hb_ref, wo_ref, g2_ref, wu_ref, wd_ref, gf_ref, y_ref):
    mix = (jnp.dot(att_ref[...].astype(BF16), wo_ref[0:ATT_WIDTH, :], preferred_element_type=F32)
           + jnp.dot(hb_ref[...].astype(BF16), wo_ref[ATT_WIDTH:D_MODEL, :], preferred_element_type=F32))
    x1 = x_ref[...] + mix
    h2 = _rms(x1, g2_ref[...]).astype(BF16)
    acc = x1
    for c in range(FF_PARTS):
        u = jnp.dot(h2, wu_ref[c], preferred_element_type=F32)
        r = jnp.maximum(u, 0.0)
        acc = acc + jnp.dot((r * r).astype(BF16), wd_ref[c], preferred_element_type=F32)
    y_ref[...] = _rms(acc, gf_ref[...])


def _outmlp(x2d, att2d, hb2d, wo_bf, g2, wu_parts, wd_parts, gf, tm):
    n = x2d.shape[0]
    ffc = D_FF // FF_PARTS
    row = lambda w: pl.BlockSpec((tm, w), lambda i: (i, 0))
    return pl.pallas_call(
        _outmlp_kernel,
        grid=(n // tm,),
        in_specs=[row(D_MODEL), row(ATT_WIDTH), row(MLSTM_WIDTH),
                  _const_spec((D_MODEL, D_MODEL)), _const_spec((1, D_MODEL)),
                  _const_spec((FF_PARTS, D_MODEL, ffc)), _const_spec((FF_PARTS, ffc, D_MODEL)),
                  _const_spec((1, D_MODEL))],
        out_specs=row(D_MODEL),
        out_shape=jax.ShapeDtypeStruct((n, D_MODEL), F32),
        compiler_params=_cparams(("parallel",)),
        name="outmlp",
    )(x2d, att2d, hb2d, wo_bf, g2, wu_parts, wd_parts, gf)


def _project(x, pos, mix_params, tm, tile_tables, window=0):
    g1, w_main_bf, w_gate_bf, bg_pad = mix_params
    b, t, _ = x.shape
    x2d = x.reshape(b * t, D_MODEL)
    tm = min(tm, b * t)
    tables = _rotary_tables(pos)
    if tile_tables:
        tables = tuple(np.tile(a, (tm // t, 1)) for a in tables)
        table_blocks, tiles_per_seq = 1, 1
    else:
        table_blocks = tiles_per_seq = t // tm
    assert window % tm == 0
    outs = _inproj(x2d, g1, w_main_bf, w_gate_bf, bg_pad, tuple(jnp.asarray(a) for a in tables), tm,
                   table_blocks, tiles_per_seq, window // tm)
    return x2d, tm, outs


def kernel(x_prompt, x_sample, cache_win_k, cache_win_v, state_conv, state_C, state_n, state_m, norm1_g, w_in, b_gate, w_conv, b_conv, mh_norm_g, w_out, norm2_g, w_up, w_down, norm_f_g):
    depth = w_in.shape[0]
    assert depth == 1, "the final norm is fused into the (single) layer's MLP kernel"
    l = 0
    n_gate = 2 * MLSTM_HEADS
    w_l = w_in[l]
    w_main_bf = w_l[:, :IN_COLS - n_gate].astype(BF16)
    w_gate_bf = jnp.pad(w_l[:, IN_COLS - n_gate:], ((0, 0), (0, GATE_PAD - n_gate))).astype(BF16)
    bg_pad = jnp.pad(b_gate[l], (0, GATE_PAD - n_gate))[None, :]
    mix_params = (norm1_g[l][None, :], w_main_bf, w_gate_bf, bg_pad)
    conv_params = (w_conv[l], b_conv[l][None, :], mh_norm_g[l][None, :])
    ffc = D_FF // FF_PARTS
    wu_parts = w_up[l].reshape(D_MODEL, FF_PARTS, ffc).transpose(1, 0, 2).astype(BF16)
    wd_parts = w_down[l].astype(BF16).reshape(FF_PARTS, ffc, D_MODEL)
    mlp_params = (w_out[l].astype(BF16), norm2_g[l][None, :], wu_parts, wd_parts, norm_f_g[None, :])

    bp, tp, _ = x_prompt.shape
    n_keep = min(DILATIONS[-1] * SUB_WINDOW, tp)
    xp2d, tm_p, (q, k, v, raw, vb, ob, gates, kt, vt) = _project(x_prompt, np.arange(tp), mix_params, 512, False,
                                                               window=n_keep)
    r3 = lambda a: a.reshape(bp, tp, a.shape[-1])
    att_p = _attn_prompt(r3(q), r3(k), r3(v)).reshape(bp * tp, ATT_WIDTH)
    heads = lambda a: a.reshape(bp, N_ATT_HEADS, HD, n_keep).transpose(0, 3, 1, 2)
    pk, pv = heads(kt), heads(vt)
    hb_p, tail_p, pc, pn, pm = _mlstm(r3(raw), r3(vb), r3(ob), r3(gates), *conv_params,
                                      CHUNK if tp % CHUNK == 0 else tp,
                                      MLSTM_PROMPT_GROUP if bp % MLSTM_PROMPT_GROUP == 0 else 1)
    hb_p = hb_p.reshape(bp * tp, MLSTM_WIDTH)

    bs, ts, _ = x_sample.shape
    xs2d, tm_s, (q, k, v, raw, vb, ob, gates) = _project(x_sample, PAST_LEN + np.arange(ts), mix_params, 512, True)
    r3 = lambda a: a.reshape(bs, ts, a.shape[-1])
    tail0 = jnp.pad(state_conv[l], ((0, 0), (SUBLANES - (CONV_W - 1), 0), (0, 0)))
    m0b = jnp.broadcast_to(state_m[l][:, :, None], (bs, MLSTM_HEADS, LANES))
    hb_s, tail_s, sc, sn, sm = _mlstm(r3(raw), r3(vb), r3(ob), r3(gates), *conv_params, ts,
                                      MLSTM_SAMPLE_GROUP if bs % MLSTM_SAMPLE_GROUP == 0 else 1,
                                      state=(tail0, state_C[l], state_n[l], m0b))
    hb_s = hb_s.reshape(bs * ts, MLSTM_WIDTH)

    kc_t = cache_win_k[l].transpose(0, 2, 3, 1)
    vc_t = cache_win_v[l].transpose(0, 2, 3, 1)
    if _attn_mlp_fits(bs, bp * tp):
        att_s, nk_t, nv_t, y_p = _attn_mlp(q, k, v, kc_t, vc_t, ts, xp2d, att_p, hb_p, *mlp_params)
    else:
        att_s, nk_t, nv_t = _attn_sample(q, k, v, kc_t, vc_t, ts)
        y_p = _outmlp(xp2d, att_p, hb_p, *mlp_params, tm_p)
    sk, sv = nk_t.transpose(0, 3, 1, 2), nv_t.transpose(0, 3, 1, 2)
    y_s = _outmlp(xs2d, att_s, hb_s, *mlp_params, tm_s)

    first_tail = SUBLANES - (CONV_W - 1)
    outs = (y_p.reshape(bp, tp, D_MODEL), y_s.reshape(bs, ts, D_MODEL),
            pk, pv, tail_p[:, first_tail:, :], pc, pn, pm[:, :, 0],
            sk, sv, tail_s[:, first_tail:, :], sc, sn, sm[:, :, 0])
    return outs[:2] + tuple(o[None] for o in outs[2:])
```

```python
import functools

import jax
import jax.numpy as jnp
import numpy as np
from jax import lax
from jax.experimental import pallas as pl
from jax.experimental.pallas import tpu as pltpu

D_MODEL = 1024
HD = 64
N_ATT_HEADS = 8
ATT_WIDTH = N_ATT_HEADS * HD
MLSTM_HEADS = 4
MLSTM_WIDTH = D_MODEL - ATT_WIDTH
MLSTM_HD = MLSTM_WIDTH // MLSTM_HEADS
ROT_DIM = HD // 4
ROT_HALF = ROT_DIM // 2
ROPE_THETA = 500000.0
DILATIONS = (1, 4, 16)
SUB_WINDOW = 128
ATT_BLOCK = 128
ROW_TILE = 512
ATT_UNROLL = 16
MLSTM_SAMPLE_GROUP = 8
MLSTM_PROMPT_GROUP = 4
FF_PARTS = 4
ATT_HEAD_SPLIT = 2
CONV_W = 4
CHUNK = 128
D_FF = 4 * D_MODEL
EPS = 1e-6
PAST_LEN = 8192
IN_SIZES = (ATT_WIDTH, ATT_WIDTH, ATT_WIDTH, 2 * MLSTM_WIDTH, MLSTM_WIDTH, MLSTM_WIDTH, 2 * MLSTM_HEADS)
IN_COLS = sum(IN_SIZES)

LANES = 128
SUBLANES = 8
GATE_PAD = LANES
VMEM_LIMIT = 56 * 1024 * 1024

F32 = jnp.float32
BF16 = jnp.bfloat16
NEG_INF = float("-inf")
LOG2E = 1.4426950408889634


def _cparams(sem):
    return pltpu.CompilerParams(dimension_semantics=sem, vmem_limit_bytes=VMEM_LIMIT)


def _const_spec(shape):
    nd = len(shape)
    return pl.BlockSpec(shape, lambda *_: (0,) * nd, pipeline_mode=pl.Buffered(1))


def _rms(x, g):
    return x * lax.rsqrt(jnp.mean(x * x, axis=-1, keepdims=True) + EPS) * g


def _inproj_kernel(x_ref, g_ref, w_ref, wg_ref, bg_ref, cos_ref, sa_ref, sb_ref,
                   q_ref, k_ref, v_ref, raw_ref, vb_ref, ob_ref, gate_ref, *win_refs,
                   tiles_per_seq, window_tiles):
    h = _rms(x_ref[...], g_ref[...]).astype(BF16)

    def proj(lo, width):
        return jnp.dot(h, w_ref[:, lo:lo + width], preferred_element_type=F32)

    cos, sa, sb = cos_ref[...], sa_ref[...], sb_ref[...]

    def rotary_store(dst, y, scale):
        for c in range(ATT_WIDTH // LANES):
            yc = y[:, c * LANES:(c + 1) * LANES]
            up = pltpu.roll(yc, LANES - ROT_HALF, 1)
            dn = pltpu.roll(yc, ROT_HALF, 1)
            r = yc * cos + up * sa + dn * sb
            dst[:, c * LANES:(c + 1) * LANES] = r * scale if scale != 1.0 else r

    off = 0
    rotary_store(q_ref, proj(off, ATT_WIDTH), HD ** -0.5)
    off += ATT_WIDTH
    rotary_store(k_ref, proj(off, ATT_WIDTH), 1.0)
    off += ATT_WIDTH
    v_ref[...] = proj(off, ATT_WIDTH)
    off += ATT_WIDTH
    raw_ref[...] = proj(off, 2 * MLSTM_WIDTH)
    off += 2 * MLSTM_WIDTH
    vb_ref[...] = proj(off, MLSTM_WIDTH)
    off += MLSTM_WIDTH
    ob_ref[...] = proj(off, MLSTM_WIDTH)
    gate_ref[...] = jnp.dot(h, wg_ref[...], preferred_element_type=F32) + bg_ref[...]

    if window_tiles:
        kt_ref, vt_ref = win_refs
        kt_ref[...] = jnp.transpose(k_ref[...])
        vt_ref[...] = jnp.transpose(v_ref[...])


def _inproj(x2d, g1, w_main_bf, w_gate_bf, bg_pad, tables, tm, table_blocks, tiles_per_seq=1, window_tiles=0):
    n = x2d.shape[0]
    grid = (n // tm,)
    row = lambda i: (i, 0)
    tab = lambda i: (i % table_blocks, 0)
    widths = (ATT_WIDTH, ATT_WIDTH, ATT_WIDTH, 2 * MLSTM_WIDTH, MLSTM_WIDTH, MLSTM_WIDTH, GATE_PAD)
    out_specs = [pl.BlockSpec((tm, w), row) for w in widths]
    out_shape = [jax.ShapeDtypeStruct((n, w), F32) for w in widths]
    if window_tiles:
        first = tiles_per_seq - window_tiles
        win = pl.BlockSpec((None, ATT_WIDTH, tm),
                           lambda i: (i // tiles_per_seq, 0, jnp.maximum(i % tiles_per_seq - first, 0)))
        out_specs += [win, win]
        out_shape += [jax.ShapeDtypeStruct((n // (tm * tiles_per_seq), ATT_WIDTH, window_tiles * tm), F32)] * 2
    return pl.pallas_call(
        functools.partial(_inproj_kernel, tiles_per_seq=tiles_per_seq, window_tiles=window_tiles),
        grid=grid,
        in_specs=[
            pl.BlockSpec((tm, D_MODEL), row),
            _const_spec((1, D_MODEL)),
            _const_spec((D_MODEL, IN_COLS - 2 * MLSTM_HEADS)),
            _const_spec((D_MODEL, GATE_PAD)),
            _const_spec((1, GATE_PAD)),
            pl.BlockSpec((tm, LANES), tab),
            pl.BlockSpec((tm, LANES), tab),
            pl.BlockSpec((tm, LANES), tab),
        ],
        out_specs=out_specs,
        out_shape=out_shape,
        compiler_params=_cparams(("arbitrary",)),
        name="inproj",
    )(x2d, g1, w_main_bf, w_gate_bf, bg_pad, *tables)


def _rotary_tables(pos):
    half = np.arange(ROT_HALF, dtype=np.float64)
    ang = np.asarray(pos, np.float64)[:, None] * (ROPE_THETA ** (-half / ROT_HALF))[None, :]
    cos, sin = np.cos(ang), np.sin(ang)
    p = ang.shape[0]
    ones = np.ones((p, HD - ROT_DIM))
    zeros = np.zeros((p, HD - ROT_DIM))
    z8 = np.zeros((p, ROT_HALF))
    cos_h = np.concatenate([cos, cos, ones], axis=1)
    sa_h = np.concatenate([-sin, z8, zeros], axis=1)
    sb_h = np.concatenate([z8, sin, zeros], axis=1)
    two = lambda t: np.concatenate([t, t], axis=1).astype(np.float32)
    return two(cos_h), two(sa_h), two(sb_h)


def _attn_prompt_kernel(bias_ref, q_ref, k_ref, v_ref, o_ref,
                        q4_ref, k4_ref, v4_ref, num1_ref, m1_ref, den1_ref, num4_ref, m4_ref, den4_ref,
                        kb_ref, vb_ref, k4b_ref, v4b_ref, *, seq):
    d_mid, d_out = DILATIONS[1], DILATIONS[2]
    d_in = d_out // d_mid
    lane = lax.broadcasted_iota(jnp.int32, (ATT_BLOCK, LANES), 1)
    head0 = lane < HD
    nblk = seq // ATT_BLOCK
    sub_blocks = nblk // d_mid
    ones_cols = jnp.ones((2 * ATT_BLOCK, LANES), BF16)

    def scores(q, kk, first):
        q = q * LOG2E
        q2 = jnp.concatenate([jnp.where(head0, q, 0.0), jnp.where(head0, 0.0, q)], axis=0).astype(BF16)
        s = lax.dot_general(q2, kk.astype(BF16), (((1,), (1,)), ((), ())),
                            preferred_element_type=F32)
        return s + bias_ref[first]

    def weighted_values(p, vv):
        o2 = jnp.dot(p.astype(BF16), jnp.concatenate([vv.astype(BF16), ones_cols], axis=1),
                     preferred_element_type=F32)
        o = jnp.where(head0, o2[:ATT_BLOCK, :LANES], o2[ATT_BLOCK:, :LANES])
        den = jnp.where(head0, o2[:ATT_BLOCK, LANES:], o2[ATT_BLOCK:, LANES:])
        return o, den

    def aligned(start):
        return pl.ds(pl.multiple_of(start, ATT_BLOCK), ATT_BLOCK)

    def split(i, carry):
        r, c = i // sub_blocks, i % sub_blocks
        src = pl.ds(r + c * ATT_BLOCK * d_mid, ATT_BLOCK, stride=d_mid)
        dst = aligned(i * ATT_BLOCK)
        q4_ref[dst, :] = q_ref[src, :]
        k4, v4 = k_ref[src, :], v_ref[src, :]
        k4_ref[dst, :] = k4
        v4_ref[dst, :] = v4
        k4b_ref[dst, :] = k4.astype(BF16)
        v4b_ref[dst, :] = v4.astype(BF16)
        kb_ref[dst, :] = k_ref[dst, :].astype(BF16)
        vb_ref[dst, :] = v_ref[dst, :].astype(BF16)
        return carry

    lax.fori_loop(0, nblk, split, 0)

    def run_group(srcs, blocks_per_class, rows_of, merge, unroll=ATT_UNROLL):
        qs, ks, vs = srcs

        def body(i, carry):
            blocks = []
            for u in range(unroll):
                idx = i * unroll + u
                n = idx % blocks_per_class
                rows = rows_of(idx, n)
                prows = rows_of(idx - jnp.minimum(n, 1), jnp.maximum(n - 1, 0))
                first = jnp.where(n == 0, 1, 0)
                kk = jnp.concatenate([ks[prows, :], ks[rows, :]], axis=0)
                blocks.append([rows, prows, scores(qs[rows, :], kk, first)])
            for blk in blocks:
                s = blk[2]
                mx = jnp.max(s, axis=1, keepdims=True)
                blk[2] = jnp.exp2(s - mx)
                blk.append(jnp.where(head0, mx[:ATT_BLOCK], mx[ATT_BLOCK:]))
            for blk in blocks:
                rows, prows, p, _ = blk
                vv = jnp.concatenate([vs[prows, :], vs[rows, :]], axis=0)
                blk[2:3] = weighted_values(p, vv)
            for rows, _, o, den, mxb in blocks:
                merge(rows, o, mxb, den)
            return carry

        lax.fori_loop(0, nblk // unroll, body, 0)

    def init_state(num_ref, m_ref, den_ref):
        def merge(rows, o, mxb, den):
            num_ref[rows, :] = o
            m_ref[rows, :] = mxb
            den_ref[rows, :] = den
        return merge

    def merge4(rows, o, mxb, den):
        m_old = m4_ref[rows, :]
        m_new = jnp.maximum(m_old, mxb)
        a = jnp.exp2(m_old - m_new)
        b = jnp.exp2(mxb - m_new)
        num4_ref[rows, :] = num4_ref[rows, :] * a + o * b
        den4_ref[rows, :] = den4_ref[rows, :] * a + den * b
        m4_ref[rows, :] = m_new

    run_group((q_ref, kb_ref, vb_ref), nblk, lambda idx, n: aligned(idx * ATT_BLOCK),
              init_state(num1_ref, m1_ref, den1_ref))
    blocks16 = nblk // d_out

    def rows16(idx, n):
        cls = idx // blocks16
        r, j = cls // d_in, cls % d_in
        return pl.ds(r * (sub_blocks * ATT_BLOCK) + j + n * ATT_BLOCK * d_in, ATT_BLOCK, stride=d_in)

    run_group((q4_ref, k4_ref, v4_ref), blocks16, rows16, init_state(num4_ref, m4_ref, den4_ref))
    run_group((q4_ref, k4b_ref, v4b_ref), sub_blocks, lambda idx, n: aligned(idx * ATT_BLOCK), merge4)

    def finish(i, carry):
        r, c = i // sub_blocks, i % sub_blocks
        tok = pl.ds(r + c * ATT_BLOCK * d_mid, ATT_BLOCK, stride=d_mid)
        res = aligned(i * ATT_BLOCK)
        m_a, m_b = m4_ref[res, :], m1_ref[tok, :]
        m_new = jnp.maximum(m_a, m_b)
        wa = jnp.exp2(m_a - m_new)
        wb = jnp.exp2(m_b - m_new)
        num = num4_ref[res, :] * wa + num1_ref[tok, :] * wb
        den = den4_ref[res, :] * wa + den1_ref[tok, :] * wb
        o_ref[tok, :] = num / den
        return carry

    lax.fori_loop(0, nblk, finish, 0, unroll=4)


def _band_bias():
    row = np.arange(2 * ATT_BLOCK)[:, None] % ATT_BLOCK
    ki = np.arange(2 * ATT_BLOCK)[None, :] - ATT_BLOCK
    rel = row - ki
    valid = (rel >= 0) & (rel <= SUB_WINDOW)
    b0 = np.where(valid, 0.0, NEG_INF).astype(np.float32)
    b1 = np.where(valid & (ki >= 0), 0.0, NEG_INF).astype(np.float32)
    return jnp.asarray(np.stack([b0, b1]))


def _attn_prompt(q, k, v):
    b, s, _ = q.shape
    assert DILATIONS[0] == 1 and DILATIONS[2] % DILATIONS[1] == 0
    assert s % (ATT_BLOCK * DILATIONS[-1]) == 0 and (s // ATT_BLOCK) % ATT_UNROLL == 0
    blk = pl.BlockSpec((None, s, LANES), lambda i, j: (i, 0, j))
    return pl.pallas_call(
        functools.partial(_attn_prompt_kernel, seq=s),
        grid=(b, ATT_WIDTH // LANES),
        in_specs=[_const_spec((2, 2 * ATT_BLOCK, 2 * ATT_BLOCK)), blk, blk, blk],
        out_specs=blk,
        out_shape=jax.ShapeDtypeStruct((b, s, ATT_WIDTH), F32),
        scratch_shapes=[pltpu.VMEM((s, LANES), F32)] * 9 + [pltpu.VMEM((s, LANES), BF16)] * 4,
        compiler_params=_cparams(("parallel", "parallel")),
        name="attn_prompt",
    )(_band_bias(), q, k, v)


def _attn_sample_body(mw_ref, mn_ref, q_ref, kn_ref, vn_ref, kc_ref, vc_ref,
                      att_ref, nk_ref, nv_ref, *, win, t_new, heads):
    mult_w = mw_ref[...]
    mult_n = mn_ref[...]
    lane = lax.broadcasted_iota(jnp.int32, (HD, LANES), 1)
    keep = lane < LANES - t_new
    top = jnp.zeros((LANES - t_new, heads * HD), F32)
    knt_all = jnp.transpose(jnp.concatenate([top, kn_ref[...]], axis=0))
    vnt_all = jnp.transpose(jnp.concatenate([top, vn_ref[...]], axis=0))
    head_rows = lambda a, h: a[h * HD:(h + 1) * HD, :]
    for h in range(heads):
        for src_ref, new_all, dst in ((kc_ref, knt_all, nk_ref), (vc_ref, vnt_all, nv_ref)):
            rolled = pltpu.roll(src_ref[h], win - t_new, 1)
            dst[h, :, 0:win - LANES] = rolled[:, 0:win - LANES]
            dst[h, :, win - LANES:win] = jnp.where(keep, rolled[:, win - LANES:win], head_rows(new_all, h))
    sc = []
    for h in range(heads):
        qh = q_ref[:, h * HD:(h + 1) * HD].astype(BF16)
        s_w = jnp.dot(qh, kc_ref[h].astype(BF16), preferred_element_type=F32)
        s_n = jnp.dot(qh, head_rows(knt_all, h).astype(BF16), preferred_element_type=F32)
        sc.append((jnp.where(mult_w > 0, s_w, NEG_INF), jnp.where(mult_n > 0, s_n, NEG_INF)))
    pr = []
    for s_w, s_n in sc:
        mx = jnp.maximum(jnp.max(s_w, axis=1, keepdims=True), jnp.max(s_n, axis=1, keepdims=True))
        p_w = mult_w * jnp.exp(s_w - mx)
        p_n = mult_n * jnp.exp(s_n - mx)
        den = jnp.sum(p_w, axis=1, keepdims=True) + jnp.sum(p_n, axis=1, keepdims=True)
        pr.append((p_w, p_n, den))
    for h, (p_w, p_n, den) in enumerate(pr):
        o = lax.dot_general(p_w.astype(BF16), vc_ref[h].astype(BF16), (((1,), (1,)), ((), ())),
                            preferred_element_type=F32)
        o = o + lax.dot_general(p_n.astype(BF16), head_rows(vnt_all, h).astype(BF16),
                                (((1,), (1,)), ((), ())), preferred_element_type=F32)
        att_ref[:, h * HD:(h + 1) * HD] = o / den


def _attn_sample_kernel(*refs, win, t_new):
    _attn_sample_body(*refs, win=win, t_new=t_new, heads=N_ATT_HEADS)


def _attn_mlp_kernel(mw_ref, mn_ref, q_ref, kn_ref, vn_ref, kc_ref, vc_ref,
                     x_ref, attp_ref, hbp_ref, wo_ref, g2_ref, wu_ref, wd_ref, gf_ref,
                     att_ref, nk_ref, nv_ref, y_ref, h2_sc, acc_sc, *, win, t_new, heads):
    c = pl.program_id(0) % FF_PARTS

    @pl.when(c == 0)
    def _():
        mix = (jnp.dot(attp_ref[...].astype(BF16), wo_ref[0:ATT_WIDTH, :], preferred_element_type=F32)
               + jnp.dot(hbp_ref[...].astype(BF16), wo_ref[ATT_WIDTH:D_MODEL, :], preferred_element_type=F32))
        x1 = x_ref[...] + mix
        h2_sc[...] = _rms(x1, g2_ref[...]).astype(BF16)
        acc_sc[...] = x1

    u = jnp.dot(h2_sc[...], wu_ref[c], preferred_element_type=F32)
    r = jnp.maximum(u, 0.0)
    acc_sc[...] += jnp.dot((r * r).astype(BF16), wd_ref[c], preferred_element_type=F32)
    _attn_sample_body(mw_ref, mn_ref, q_ref, kn_ref, vn_ref, kc_ref, vc_ref, att_ref, nk_ref, nv_ref,
                      win=win, t_new=t_new, heads=heads)

    @pl.when(c == FF_PARTS - 1)
    def _():
        y_ref[...] = _rms(acc_sc[...], gf_ref[...])


def _sample_multiplicity(win, t_new):
    t = np.arange(t_new)[:, None]
    idx = np.arange(win + t_new)[None, :]
    back = win + t - idx
    mult = np.zeros((t_new, win + t_new), np.float32)
    for dil in DILATIONS:
        mult += ((back >= 0) & (back % dil == 0) & (back // dil <= SUB_WINDOW)).astype(np.float32)
    mw = mult[:, :win]
    mn = np.zeros((t_new, LANES), np.float32)
    mn[:, LANES - t_new:] = mult[:, win:]
    return jnp.asarray(mw), jnp.asarray(mn)


def _attn_sample(q, k_new, v_new, kc_t, vc_t, t_new):
    b, nh, _, win = kc_t.shape
    assert win == DILATIONS[-1] * SUB_WINDOW and t_new == SUBLANES and q.shape[0] == b * t_new
    mw, mn = _sample_multiplicity(win, t_new)
    rows = pl.BlockSpec((t_new, ATT_WIDTH), lambda i: (i, 0))
    per_b = lambda *shape: pl.BlockSpec((None,) + shape, lambda i: (i,) + (0,) * len(shape))
    return pl.pallas_call(
        functools.partial(_attn_sample_kernel, win=win, t_new=t_new),
        grid=(b,),
        in_specs=[
            _const_spec((t_new, win)), _const_spec((t_new, LANES)),
            rows, rows, rows, per_b(nh, HD, win), per_b(nh, HD, win),
        ],
        out_specs=[rows, per_b(nh, HD, win), per_b(nh, HD, win)],
        out_shape=[
            jax.ShapeDtypeStruct((b * t_new, ATT_WIDTH), F32),
            jax.ShapeDtypeStruct((b, nh, HD, win), F32),
            jax.ShapeDtypeStruct((b, nh, HD, win), F32),
        ],
        compiler_params=_cparams(("parallel",)),
        name="attn_sample",
    )(mw, mn, q, k_new, v_new, kc_t, vc_t)


def _attn_mlp_fits(b_s, n_p):
    steps = b_s * ATT_HEAD_SPLIT
    if steps % FF_PARTS:
        return False
    tiles = steps // FF_PARTS
    return n_p % tiles == 0 and (n_p // tiles) % SUBLANES == 0 and n_p // tiles <= ROW_TILE


def _attn_mlp(q, k_new, v_new, kc_t, vc_t, t_new, x2d, attp, hbp, wo_bf, g2, wu_parts, wd_parts, gf):
    b, nh, _, win = kc_t.shape
    n_p = x2d.shape[0]
    assert win == DILATIONS[-1] * SUB_WINDOW and t_new == SUBLANES and q.shape[0] == b * t_new
    heads = nh // ATT_HEAD_SPLIT
    steps = b * ATT_HEAD_SPLIT
    tm = n_p // (steps // FF_PARTS)
    ffc = D_FF // FF_PARTS
    mw, mn = _sample_multiplicity(win, t_new)
    rows = pl.BlockSpec((t_new, heads * HD), lambda i: (i // ATT_HEAD_SPLIT, i % ATT_HEAD_SPLIT))
    win_blk = pl.BlockSpec((None, heads, HD, win), lambda i: (i // ATT_HEAD_SPLIT, i % ATT_HEAD_SPLIT, 0, 0))
    tile = lambda w: pl.BlockSpec((tm, w), lambda i: (i // FF_PARTS, 0))
    return pl.pallas_call(
        functools.partial(_attn_mlp_kernel, win=win, t_new=t_new, heads=heads),
        grid=(steps,),
        in_specs=[
            _const_spec((t_new, win)), _const_spec((t_new, LANES)),
            rows, rows, rows, win_blk, win_blk,
            tile(D_MODEL), tile(ATT_WIDTH), tile(MLSTM_WIDTH),
            _const_spec((D_MODEL, D_MODEL)), _const_spec((1, D_MODEL)),
            _const_spec((FF_PARTS, D_MODEL, ffc)), _const_spec((FF_PARTS, ffc, D_MODEL)),
            _const_spec((1, D_MODEL)),
        ],
        out_specs=[rows, win_blk, win_blk, tile(D_MODEL)],
        out_shape=[
            jax.ShapeDtypeStruct((b * t_new, ATT_WIDTH), F32),
            jax.ShapeDtypeStruct((b, nh, HD, win), F32),
            jax.ShapeDtypeStruct((b, nh, HD, win), F32),
            jax.ShapeDtypeStruct((n_p, D_MODEL), F32),
        ],
        scratch_shapes=[pltpu.VMEM((tm, D_MODEL), BF16), pltpu.VMEM((tm, D_MODEL), F32)],
        compiler_params=_cparams(("arbitrary",)),
        name="attn_mlp",
    )(mw, mn, q, k_new, v_new, kc_t, vc_t, x2d, attp, hbp, wo_bf, g2, wu_parts, wd_parts, gf)


def _split_dot(tri, x):
    hi = x.astype(BF16)
    lo = (x - hi.astype(F32)).astype(BF16)
    return (jnp.dot(tri, hi, preferred_element_type=F32)
            + jnp.dot(tri, lo, preferred_element_type=F32))


def _split_dot_rows(x, tri):
    hi = x.astype(BF16)
    lo = (x - hi.astype(F32)).astype(BF16)
    return (jnp.dot(hi, tri, preferred_element_type=F32)
            + jnp.dot(lo, tri, preferred_element_type=F32))


def _log_sigmoid(x):
    return jnp.minimum(x, 0.0) - jnp.log1p(jnp.exp(-jnp.abs(x)))


def _mlstm_kernel(*refs, chunk, group, has_state):
    if has_state:
        (raw_ref, vb_ref, ob_ref, gate_ref, wc_ref, bc_ref, g_ref, tail_ref, c0_ref, n0_ref, m0_ref,
         hb_ref, tail_out_ref, c_out_ref, n_out_ref, m_out_ref, xp_ref, act_ref, c_sc, n_sc, m_sc) = refs
        nm_sc = None
    else:
        (raw_ref, vb_ref, ob_ref, gate_ref, wc_ref, bc_ref, g_ref,
         hb_ref, tail_out_ref, c_out_ref, n_out_ref, m_out_ref, xp_ref, act_ref, c_sc, n_sc, m_sc, nm_sc) = refs
    L = chunk
    c_idx = pl.program_id(1)
    mxu_sums = nm_sc is not None and L == LANES

    @pl.when(c_idx == 0)
    def _():
        if has_state:
            for c in range(2 * MLSTM_HEADS):
                xp_ref[:, c, 0:SUBLANES, :] = tail_ref[:, :, c * LANES:(c + 1) * LANES]
            c_sc[...] = c0_ref[...]
            n_sc[...] = n0_ref[...]
            m_sc[...] = m0_ref[...]
        else:
            xp_ref[:, :, 0:SUBLANES, :] = jnp.zeros((group, 2 * MLSTM_HEADS, SUBLANES, LANES), F32)
            c_sc[...] = jnp.zeros_like(c_sc)
            n_sc[...] = jnp.zeros_like(n_sc)
            m_sc[...] = jnp.zeros_like(m_sc)
            nm_sc[...] = jnp.zeros_like(nm_sc)

    r_i = lax.broadcasted_iota(jnp.int32, (L, L), 0)
    c_i = lax.broadcasted_iota(jnp.int32, (L, L), 1)
    causal = r_i >= c_i
    tri_l = jnp.where(causal, 1.0, 0.0).astype(BF16)
    tri_u = jnp.where(r_i <= c_i, 1.0, 0.0).astype(BF16)
    first_tap = SUBLANES - (CONV_W - 1)
    bc = bc_ref[...]
    taps = [wc_ref[w:w + 1, :] for w in range(CONV_W)]
    ones_sq = jnp.ones((LANES, LANES), BF16)

    per_batch = []
    for g in range(group):
        tail_out_ref[g] = raw_ref[g, L - SUBLANES:L, :]
        for c in range(2 * MLSTM_HEADS):
            cl = slice(c * LANES, (c + 1) * LANES)
            raw = raw_ref[g, :, cl]
            xp_ref[g, c, SUBLANES:SUBLANES + L, :] = raw
            if L % (SUBLANES * SUBLANES) == 0:
                n_rows = L // SUBLANES
                slabs = [xp_ref[g, c, pl.ds(first_tap + j, n_rows, stride=SUBLANES), :]
                         for j in range(SUBLANES + CONV_W - 1)]
                for s in range(SUBLANES):
                    conv = bc[:, cl]
                    for w in range(CONV_W):
                        conv = conv + slabs[s + w] * taps[w][:, cl]
                    act_ref[g, c, pl.ds(s, n_rows, stride=SUBLANES), :] = conv * jax.nn.sigmoid(conv)
            else:
                conv = bc[:, cl]
                for w in range(CONV_W):
                    conv = conv + xp_ref[g, c, first_tap + w:first_tap + w + L, :] * taps[w][:, cl]
                act_ref[g, c] = conv * jax.nn.sigmoid(conv)
            xp_ref[g, c, 0:SUBLANES, :] = raw[L - SUBLANES:L, :]

        gates = gate_ref[g]
        lf = _log_sigmoid(gates)
        b_col = _split_dot(tri_l, lf)
        if L == LANES:
            gates_sq = gates
        else:
            gates_sq = jnp.concatenate([gates, jnp.zeros((LANES - L, LANES), F32)], axis=0)
        gates_t = jnp.transpose(gates_sq)[0:SUBLANES, 0:L]
        lf_t = _log_sigmoid(gates_t)
        b_row = _split_dot_rows(lf_t, tri_u)

        per_batch.append((gates, b_col, gates_t, b_row))

    items = [(g, h) for g in range(group) for h in range(MLSTM_HEADS)]
    st = {}
    for g, h in items:
        gates, b_col, gates_t, b_row = per_batch[g]
        ig_c = gates[:, h:h + 1]
        b_c = b_col[:, MLSTM_HEADS + h:MLSTM_HEADS + h + 1]
        ig_r = gates_t[h:h + 1, :]
        b_r = b_row[MLSTM_HEADS + h:MLSTM_HEADS + h + 1, :]
        if mxu_sums:
            b_c_wide = jnp.broadcast_to(b_c, (L, LANES))
            a_c = b_c_wide + m_sc[g, h:h + 1, 0:1]
            dm = jnp.where(causal, b_c_wide - b_r + ig_r, NEG_INF)
            mt = jnp.maximum(a_c, jnp.broadcast_to(jnp.max(dm, axis=1, keepdims=True), (L, LANES)))
        else:
            a_c = b_c + m_sc[g, h:h + 1, 0:1]
            dm = jnp.where(causal, b_c - b_r + ig_r, NEG_INF)
            mt = jnp.maximum(a_c, jnp.max(dm, axis=1, keepdims=True))
        st[g, h] = dict(ig_c=ig_c, b_c=b_c, a_c=a_c, dm=dm, mt=mt)
    for g, h in items:
        d = st[g, h]
        q = act_ref[g, h]
        k = act_ref[g, MLSTM_HEADS + h] * (MLSTM_HD ** -0.5)
        v = vb_ref[g, :, h * MLSTM_HD:(h + 1) * MLSTM_HD]
        qb, kb = q.astype(BF16), k.astype(BF16)
        w_inter = jnp.exp(d["a_c"] - d["mt"])
        wm = jnp.exp(d["dm"] - d["mt"])
        sc = lax.dot_general(qb, kb, (((1,), (1,)), ((), ())), preferred_element_type=F32) * wm
        d.update(q=q, k=k, v=v, qb=qb, vb=v.astype(BF16), w_inter=w_inter, sc=sc)
        del d["dm"]
    for g, h in items:
        d = st[g, h]
        c_old = c_sc[g, h]
        if mxu_sums:
            nm_old = nm_sc[g, h]
            v_ones = jnp.concatenate([d["vb"], ones_sq], axis=1)
            state = jnp.concatenate([c_old, nm_old], axis=1).astype(BF16)
            inter = jnp.dot(d["qb"], state, preferred_element_type=F32)
            intra = jnp.dot(d["sc"].astype(BF16), v_ones, preferred_element_type=F32)
            num = d["w_inter"] * inter[:, :LANES] + intra[:, :LANES]
            den = d["w_inter"] * inter[:, LANES:] + intra[:, LANES:]
            d.update(v_ones=v_ones, nm_old=nm_old)
        else:
            n_old = n_sc[g, h:h + 1, :]
            num = (d["w_inter"] * jnp.dot(d["qb"], c_old.astype(BF16), preferred_element_type=F32)
                   + jnp.dot(d["sc"].astype(BF16), d["vb"], preferred_element_type=F32))
            den = (d["w_inter"] * jnp.sum(d["q"] * n_old, axis=1, keepdims=True)
                   + jnp.sum(d["sc"], axis=1, keepdims=True))
            d.update(n_old=n_old)
        d.update(c_old=c_old, hh=num / jnp.maximum(jnp.abs(den), jnp.exp(-d["mt"])))
        del d["sc"]
    for g, h in items:
        d = st[g, h]
        mt, a_c, b_c = d["mt"], d["a_c"], d["b_c"]
        m_last = mt[L - 1:L, 0:1]
        w_last = jnp.exp(a_c[L - 1:L, 0:1] - m_last)
        w_t = jnp.exp(b_c[L - 1:L, :] - b_c + d["ig_c"] - m_last)
        kw = d["k"] * w_t
        if mxu_sums:
            upd = jnp.dot(jnp.transpose(kw).astype(BF16), d["v_ones"], preferred_element_type=F32)
            c_sc[g, h] = w_last * d["c_old"] + upd[:, :LANES]
            nm_sc[g, h] = w_last * d["nm_old"] + upd[:, LANES:]
        else:
            if L == LANES:
                kw_sq, v_sq = kw, d["v"]
            else:
                pad = jnp.zeros((LANES - L, MLSTM_HD), F32)
                kw_sq = jnp.concatenate([kw, pad], axis=0)
                v_sq = jnp.concatenate([d["v"], pad], axis=0)
            c_sc[g, h] = w_last * d["c_old"] + jnp.dot(jnp.transpose(kw_sq).astype(BF16), v_sq.astype(BF16),
                                                       preferred_element_type=F32)
            n_sc[g, h:h + 1, :] = w_last * d["n_old"] + jnp.sum(kw, axis=0, keepdims=True)
        m_sc[g, h:h + 1, :] = jnp.broadcast_to(m_last, (1, LANES))
    for g, h in items:
        hh = st[g, h]["hh"]
        sl = slice(h * MLSTM_HD, (h + 1) * MLSTM_HD)
        if mxu_sums:
            mean_sq = _split_dot_rows(hh * hh, ones_sq) * (1.0 / MLSTM_HD)
        else:
            mean_sq = jnp.mean(hh * hh, axis=1, keepdims=True)
        hn = hh * lax.rsqrt(mean_sq + EPS)
        hb_ref[g, :, sl] = hn * g_ref[:, sl] * jax.nn.sigmoid(ob_ref[g, :, sl])

    @pl.when(c_idx == pl.num_programs(1) - 1)
    def _():
        if mxu_sums:
            for g in range(group):
                for h in range(MLSTM_HEADS):
                    n_sc[g, h:h + 1, :] = jnp.transpose(nm_sc[g, h])[0:1, :]
        c_out_ref[...] = c_sc[...]
        n_out_ref[...] = n_sc[...]
        m_out_ref[...] = m_sc[...]


def _mlstm(raw, vb, ob, gates, w_conv, b_conv, mh_g, chunk, group, state=None):
    b, t, _ = raw.shape
    nc = t // chunk
    assert nc * chunk == t and chunk % SUBLANES == 0 and b % group == 0
    seq = lambda w: pl.BlockSpec((group, chunk, w), lambda i, j: (i, j, 0))
    per_b = lambda *shape: pl.BlockSpec((group,) + shape, lambda i, j: (i,) + (0,) * len(shape))
    in_specs = [seq(2 * MLSTM_WIDTH), seq(MLSTM_WIDTH), seq(MLSTM_WIDTH), seq(GATE_PAD),
                _const_spec((CONV_W, 2 * MLSTM_WIDTH)), _const_spec((1, 2 * MLSTM_WIDTH)),
                _const_spec((1, MLSTM_WIDTH))]
    args = [raw, vb, ob, gates, w_conv, b_conv, mh_g]
    state_shapes = [(SUBLANES, 2 * MLSTM_WIDTH), (MLSTM_HEADS, MLSTM_HD, MLSTM_HD),
                    (MLSTM_HEADS, MLSTM_HD), (MLSTM_HEADS, LANES)]
    state_specs = [per_b(*sh) for sh in state_shapes]
    if state is not None:
        in_specs += state_specs
        args += list(state)
    return pl.pallas_call(
        functools.partial(_mlstm_kernel, chunk=chunk, group=group, has_state=state is not None),
        grid=(b // group, nc),
        in_specs=in_specs,
        out_specs=[seq(MLSTM_WIDTH)] + state_specs,
        out_shape=[jax.ShapeDtypeStruct((b, t, MLSTM_WIDTH), F32)]
        + [jax.ShapeDtypeStruct((b,) + sh, F32) for sh in state_shapes],
        scratch_shapes=[
            pltpu.VMEM((group, 2 * MLSTM_HEADS, SUBLANES + chunk, LANES), F32),
            pltpu.VMEM((group, 2 * MLSTM_HEADS, chunk, LANES), F32),
        ] + [pltpu.VMEM((group,) + sh, F32) for sh in state_shapes[1:]]
        + ([] if state is not None else [pltpu.VMEM((group,) + state_shapes[1], F32)]),
        compiler_params=_cparams(("parallel", "arbitrary")),
        name="mlstm_state" if state is not None else "mlstm",
    )(*args)


def _outmlp_kernel(x_ref, att_ref, hb_ref, wo_ref, g2_ref, wu_ref, wd_ref, gf_ref, y_ref):
    mix = (jnp.dot(att_ref[...].astype(BF16), wo_ref[0:ATT_WIDTH, :], preferred_element_type=F32)
           + jnp.dot(hb_ref[...].astype(BF16), wo_ref[ATT_WIDTH:D_MODEL, :], preferred_element_type=F32))
    x1 = x_ref[...] + mix
    h2 = _rms(x1, g2_ref[...]).astype(BF16)
    acc = x1
    for c in range(FF_PARTS):
        u = jnp.dot(h2, wu_ref[c], preferred_element_type=F32)
        r = jnp.maximum(u, 0.0)
        acc = acc + jnp.dot((r * r).astype(BF16), wd_ref[c], preferred_element_type=F32)
    y_ref[...] = _rms(acc, gf_ref[...])


def _outmlp(x2d, att2d, hb2d, wo_bf, g2, wu_parts, wd_parts, gf, tm):
    n = x2d.shape[0]
    ffc = D_FF // FF_PARTS
    row = lambda w: pl.BlockSpec((tm, w), lambda i: (i, 0))
    return pl.pallas_call(
        _outmlp_kernel,
        grid=(n // tm,),
        in_specs=[row(D_MODEL), row(ATT_WIDTH), row(MLSTM_WIDTH),
                  _const_spec((D_MODEL, D_MODEL)), _const_spec((1, D_MODEL)),
                  _const_spec((FF_PARTS, D_MODEL, ffc)), _const_spec((FF_PARTS, ffc, D_MODEL)),
                  _const_spec((1, D_MODEL))],
        out_specs=row(D_MODEL),
        out_shape=jax.ShapeDtypeStruct((n, D_MODEL), F32),
        compiler_params=_cparams(("parallel",)),
        name="outmlp",
    )(x2d, att2d, hb2d, wo_bf, g2, wu_parts, wd_parts, gf)


def _project(x, pos, mix_params, tm, tile_tables, window=0):
    g1, w_main_bf, w_gate_bf, bg_pad = mix_params
    b, t, _ = x.shape
    x2d = x.reshape(b * t, D_MODEL)
    tm = min(tm, b * t)
    tables = _rotary_tables(pos)
    if tile_tables:
        tables = tuple(np.tile(a, (tm // t, 1)) for a in tables)
        table_blocks, tiles_per_seq = 1, 1
    else:
        table_blocks = tiles_per_seq = t // tm
    assert window % tm == 0
    outs = _inproj(x2d, g1, w_main_bf, w_gate_bf, bg_pad, tuple(jnp.asarray(a) for a in tables), tm,
                   table_blocks, tiles_per_seq, window // tm)
    return x2d, tm, outs


def kernel(x_prompt, x_sample, cache_win_k, cache_win_v, state_conv, state_C, state_n, state_m, norm1_g, w_in, b_gate, w_conv, b_conv, mh_norm_g, w_out, norm2_g, w_up, w_down, norm_f_g):
    depth = w_in.shape[0]
    assert depth == 1, "the final norm is fused into the (single) layer's MLP kernel"
    l = 0
    n_gate = 2 * MLSTM_HEADS
    w_l = w_in[l]
    w_main_bf = w_l[:, :IN_COLS - n_gate].astype(BF16)
    w_gate_bf = jnp.pad(w_l[:, IN_COLS - n_gate:], ((0, 0), (0, GATE_PAD - n_gate))).astype(BF16)
    bg_pad = jnp.pad(b_gate[l], (0, GATE_PAD - n_gate))[None, :]
    mix_params = (norm1_g[l][None, :], w_main_bf, w_gate_bf, bg_pad)
    conv_params = (w_conv[l], b_conv[l][None, :], mh_norm_g[l][None, :])
    ffc = D_FF // FF_PARTS
    wu_parts = w_up[l].reshape(D_MODEL, FF_PARTS, ffc).transpose(1, 0, 2).astype(BF16)
    wd_parts = w_down[l].astype(BF16).reshape(FF_PARTS, ffc, D_MODEL)
    mlp_params = (w_out[l].astype(BF16), norm2_g[l][None, :], wu_parts, wd_parts, norm_f_g[None, :])

    bp, tp, _ = x_prompt.shape
    n_keep = min(DILATIONS[-1] * SUB_WINDOW, tp)
    xp2d, tm_p, (q, k, v, raw, vb, ob, gates, kt, vt) = _project(x_prompt, np.arange(tp), mix_params, ROW_TILE, False,
                                                               window=n_keep)
    r3 = lambda a: a.reshape(bp, tp, a.shape[-1])
    att_p = _attn_prompt(r3(q), r3(k), r3(v)).reshape(bp * tp, ATT_WIDTH)
    heads = lambda a: a.reshape(bp, N_ATT_HEADS, HD, n_keep).transpose(0, 3, 1, 2)
    pk, pv = heads(kt), heads(vt)
    hb_p, tail_p, pc, pn, pm = _mlstm(r3(raw), r3(vb), r3(ob), r3(gates), *conv_params,
                                      CHUNK if tp % CHUNK == 0 else tp,
                                      MLSTM_PROMPT_GROUP if bp % MLSTM_PROMPT_GROUP == 0 else 1)
    hb_p = hb_p.reshape(bp * tp, MLSTM_WIDTH)

    bs, ts, _ = x_sample.shape
    xs2d, tm_s, (q, k, v, raw, vb, ob, gates) = _project(x_sample, PAST_LEN + np.arange(ts), mix_params, ROW_TILE, True)
    r3 = lambda a: a.reshape(bs, ts, a.shape[-1])
    tail0 = jnp.pad(state_conv[l], ((0, 0), (SUBLANES - (CONV_W - 1), 0), (0, 0)))
    m0b = jnp.broadcast_to(state_m[l][:, :, None], (bs, MLSTM_HEADS, LANES))
    hb_s, tail_s, sc, sn, sm = _mlstm(r3(raw), r3(vb), r3(ob), r3(gates), *conv_params, ts,
                                      MLSTM_SAMPLE_GROUP if bs % MLSTM_SAMPLE_GROUP == 0 else 1,
                                      state=(tail0, state_C[l], state_n[l], m0b))
    hb_s = hb_s.reshape(bs * ts, MLSTM_WIDTH)

    kc_t = cache_win_k[l].transpose(0, 2, 3, 1)
    vc_t = cache_win_v[l].transpose(0, 2, 3, 1)
    if _attn_mlp_fits(bs, bp * tp):
        att_s, nk_t, nv_t, y_p = _attn_mlp(q, k, v, kc_t, vc_t, ts, xp2d, att_p, hb_p, *mlp_params)
    else:
        att_s, nk_t, nv_t = _attn_sample(q, k, v, kc_t, vc_t, ts)
        y_p = _outmlp(xp2d, att_p, hb_p, *mlp_params, tm_p)
    sk, sv = nk_t.transpose(0, 3, 1, 2), nv_t.transpose(0, 3, 1, 2)
    y_s = _outmlp(xs2d, att_s, hb_s, *mlp_params, tm_s)

    first_tail = SUBLANES - (CONV_W - 1)
    outs = (y_p.reshape(bp, tp, D_MODEL), y_s.reshape(bs, ts, D_MODEL),
            pk, pv, tail_p[:, first_tail:, :], pc, pn, pm[:, :, 0],
            sk, sv, tail_s[:, first_tail:, :], sc, sn, sm[:, :, 0])
    return outs[:2] + tuple(o[None] for o in outs[2:])
```

```python
import functools

import jax
import jax.numpy as jnp
import numpy as np
from jax import lax
from jax.experimental import pallas as pl
from jax.experimental.pallas import tpu as pltpu

D_MODEL = 1024
HD = 64
N_ATT_HEADS = 8
ATT_WIDTH = N_ATT_HEADS * HD
MLSTM_HEADS = 4
MLSTM_WIDTH = D_MODEL - ATT_WIDTH
MLSTM_HD = MLSTM_WIDTH // MLSTM_HEADS
ROT_DIM = HD // 4
ROT_HALF = ROT_DIM // 2
ROPE_THETA = 500000.0
DILATIONS = (1, 4, 16)
SUB_WINDOW = 128
ATT_BLOCK = 128
ROW_TILE = 512
ATT_UNROLL = 16
MLSTM_SAMPLE_GROUP = 8
MLSTM_PROMPT_GROUP = 4
FF_PARTS = 4
ATT_HEAD_SPLIT = 2
CONV_W = 4
CHUNK = 128
D_FF = 4 * D_MODEL
EPS = 1e-6
PAST_LEN = 8192
IN_SIZES = (ATT_WIDTH, ATT_WIDTH, ATT_WIDTH, 2 * MLSTM_WIDTH, MLSTM_WIDTH, MLSTM_WIDTH, 2 * MLSTM_HEADS)
IN_COLS = sum(IN_SIZES)

LANES = 128
SUBLANES = 8
GATE_PAD = LANES
VMEM_LIMIT = 56 * 1024 * 1024

F32 = jnp.float32
BF16 = jnp.bfloat16
NEG_INF = float("-inf")
LOG2E = 1.4426950408889634


def _cparams(sem):
    return pltpu.CompilerParams(dimension_semantics=sem, vmem_limit_bytes=VMEM_LIMIT)


def _const_spec(shape):
    nd = len(shape)
    return pl.BlockSpec(shape, lambda *_: (0,) * nd, pipeline_mode=pl.Buffered(1))


def _rms(x, g):
    return x * lax.rsqrt(jnp.mean(x * x, axis=-1, keepdims=True) + EPS) * g


def _inproj_kernel(x_ref, g_ref, w_ref, wg_ref, bg_ref, cos_ref, sa_ref, sb_ref,
                   q_ref, k_ref, v_ref, raw_ref, vb_ref, ob_ref, gate_ref, *win_refs,
                   tiles_per_seq, window_tiles):
    h = _rms(x_ref[...], g_ref[...]).astype(BF16)

    def proj(lo, width):
        return jnp.dot(h, w_ref[:, lo:lo + width], preferred_element_type=F32)

    cos, sa, sb = cos_ref[...], sa_ref[...], sb_ref[...]

    def rotary_store(dst, y, scale):
        for c in range(ATT_WIDTH // LANES):
            yc = y[:, c * LANES:(c + 1) * LANES]
            up = pltpu.roll(yc, LANES - ROT_HALF, 1)
            dn = pltpu.roll(yc, ROT_HALF, 1)
            r = yc * cos + up * sa + dn * sb
            dst[:, c * LANES:(c + 1) * LANES] = r * scale if scale != 1.0 else r

    off = 0
    rotary_store(q_ref, proj(off, ATT_WIDTH), HD ** -0.5)
    off += ATT_WIDTH
    rotary_store(k_ref, proj(off, ATT_WIDTH), 1.0)
    off += ATT_WIDTH
    v_ref[...] = proj(off, ATT_WIDTH)
    off += ATT_WIDTH
    raw_ref[...] = proj(off, 2 * MLSTM_WIDTH)
    off += 2 * MLSTM_WIDTH
    vb_ref[...] = proj(off, MLSTM_WIDTH)
    off += MLSTM_WIDTH
    ob_ref[...] = proj(off, MLSTM_WIDTH)
    gate_ref[...] = jnp.dot(h, wg_ref[...], preferred_element_type=F32) + bg_ref[...]

    if window_tiles:
        kt_ref, vt_ref = win_refs
        kt_ref[...] = jnp.transpose(k_ref[...])
        vt_ref[...] = jnp.transpose(v_ref[...])


def _inproj(x2d, g1, w_main_bf, w_gate_bf, bg_pad, tables, tm, table_blocks, tiles_per_seq=1, window_tiles=0):
    n = x2d.shape[0]
    grid = (n // tm,)
    row = lambda i: (i, 0)
    tab = lambda i: (i % table_blocks, 0)
    widths = (ATT_WIDTH, ATT_WIDTH, ATT_WIDTH, 2 * MLSTM_WIDTH, MLSTM_WIDTH, MLSTM_WIDTH, GATE_PAD)
    out_specs = [pl.BlockSpec((tm, w), row) for w in widths]
    out_shape = [jax.ShapeDtypeStruct((n, w), F32) for w in widths]
    if window_tiles:
        first = tiles_per_seq - window_tiles
        win = pl.BlockSpec((None, ATT_WIDTH, tm),
                           lambda i: (i // tiles_per_seq, 0, jnp.maximum(i % tiles_per_seq - first, 0)))
        out_specs += [win, win]
        out_shape += [jax.ShapeDtypeStruct((n // (tm * tiles_per_seq), ATT_WIDTH, window_tiles * tm), F32)] * 2
    return pl.pallas_call(
        functools.partial(_inproj_kernel, tiles_per_seq=tiles_per_seq, window_tiles=window_tiles),
        grid=grid,
        in_specs=[
            pl.BlockSpec((tm, D_MODEL), row),
            _const_spec((1, D_MODEL)),
            _const_spec((D_MODEL, IN_COLS - 2 * MLSTM_HEADS)),
            _const_spec((D_MODEL, GATE_PAD)),
            _const_spec((1, GATE_PAD)),
            pl.BlockSpec((tm, LANES), tab),
            pl.BlockSpec((tm, LANES), tab),
            pl.BlockSpec((tm, LANES), tab),
        ],
        out_specs=out_specs,
        out_shape=out_shape,
        compiler_params=_cparams(("arbitrary",)),
        name="inproj",
    )(x2d, g1, w_main_bf, w_gate_bf, bg_pad, *tables)


def _rotary_tables(pos):
    half = np.arange(ROT_HALF, dtype=np.float64)
    ang = np.asarray(pos, np.float64)[:, None] * (ROPE_THETA ** (-half / ROT_HALF))[None, :]
    cos, sin = np.cos(ang), np.sin(ang)
    p = ang.shape[0]
    ones = np.ones((p, HD - ROT_DIM))
    zeros = np.zeros((p, HD - ROT_DIM))
    z8 = np.zeros((p, ROT_HALF))
    cos_h = np.concatenate([cos, cos, ones], axis=1)
    sa_h = np.concatenate([-sin, z8, zeros], axis=1)
    sb_h = np.concatenate([z8, sin, zeros], axis=1)
    two = lambda t: np.concatenate([t, t], axis=1).astype(np.float32)
    return two(cos_h), two(sa_h), two(sb_h)


def _attn_prompt_kernel(bias_ref, q_ref, k_ref, v_ref, o_ref,
                        q4_ref, k4_ref, v4_ref, num1_ref, m1_ref, den1_ref, num4_ref, m4_ref, den4_ref,
                        kb_ref, vb_ref, k4b_ref, v4b_ref, *, seq):
    d_mid, d_out = DILATIONS[1], DILATIONS[2]
    d_in = d_out // d_mid
    lane = lax.broadcasted_iota(jnp.int32, (ATT_BLOCK, LANES), 1)
    head0 = lane < HD
    nblk = seq // ATT_BLOCK
    sub_blocks = nblk // d_mid
    ones_cols = jnp.ones((2 * ATT_BLOCK, LANES), BF16)

    def scores(q, kk, first):
        q = q * LOG2E
        q2 = jnp.concatenate([jnp.where(head0, q, 0.0), jnp.where(head0, 0.0, q)], axis=0).astype(BF16)
        s = lax.dot_general(q2, kk.astype(BF16), (((1,), (1,)), ((), ())),
                            preferred_element_type=F32)
        return s + bias_ref[first]

    def weighted_values(p, vv):
        o2 = jnp.dot(p.astype(BF16), jnp.concatenate([vv.astype(BF16), ones_cols], axis=1),
                     preferred_element_type=F32)
        o = jnp.where(head0, o2[:ATT_BLOCK, :LANES], o2[ATT_BLOCK:, :LANES])
        den = jnp.where(head0, o2[:ATT_BLOCK, LANES:], o2[ATT_BLOCK:, LANES:])
        return o, den

    def aligned(start):
        return pl.ds(pl.multiple_of(start, ATT_BLOCK), ATT_BLOCK)

    def split(i, carry):
        r, c = i // sub_blocks, i % sub_blocks
        src = pl.ds(r + c * ATT_BLOCK * d_mid, ATT_BLOCK, stride=d_mid)
        dst = aligned(i * ATT_BLOCK)
        q4_ref[dst, :] = q_ref[src, :]
        k4, v4 = k_ref[src, :], v_ref[src, :]
        k4_ref[dst, :] = k4
        v4_ref[dst, :] = v4
        k4b_ref[dst, :] = k4.astype(BF16)
        v4b_ref[dst, :] = v4.astype(BF16)
        kb_ref[dst, :] = k_ref[dst, :].astype(BF16)
        vb_ref[dst, :] = v_ref[dst, :].astype(BF16)
        return carry

    lax.fori_loop(0, nblk, split, 0)

    def run_group(srcs, blocks_per_class, rows_of, merge, unroll=ATT_UNROLL):
        qs, ks, vs = srcs

        def body(i, carry):
            blocks = []
            for u in range(unroll):
                idx = i * unroll + u
                n = idx % blocks_per_class
                rows = rows_of(idx, n)
                prows = rows_of(idx - jnp.minimum(n, 1), jnp.maximum(n - 1, 0))
                first = jnp.where(n == 0, 1, 0)
                kk = jnp.concatenate([ks[prows, :], ks[rows, :]], axis=0)
                blocks.append([rows, prows, scores(qs[rows, :], kk, first)])
            for blk in blocks:
                s = blk[2]
                mx = jnp.max(s, axis=1, keepdims=True)
                blk[2] = jnp.exp2(s - mx)
                blk.append(jnp.where(head0, mx[:ATT_BLOCK], mx[ATT_BLOCK:]))
            for blk in blocks:
                rows, prows, p, _ = blk
                vv = jnp.concatenate([vs[prows, :], vs[rows, :]], axis=0)
                blk[2:3] = weighted_values(p, vv)
            for rows, _, o, den, mxb in blocks:
                merge(rows, o, mxb, den)
            return carry

        lax.fori_loop(0, nblk // unroll, body, 0)

    def init_state(num_ref, m_ref, den_ref):
        def merge(rows, o, mxb, den):
            num_ref[rows, :] = o
            m_ref[rows, :] = mxb
            den_ref[rows, :] = den
        return merge

    def merge4(rows, o, mxb, den):
        m_old = m4_ref[rows, :]
        m_new = jnp.maximum(m_old, mxb)
        a = jnp.exp2(m_old - m_new)
        b = jnp.exp2(mxb - m_new)
        num4_ref[rows, :] = num4_ref[rows, :] * a + o * b
        den4_ref[rows, :] = den4_ref[rows, :] * a + den * b
        m4_ref[rows, :] = m_new

    run_group((q_ref, kb_ref, vb_ref), nblk, lambda idx, n: aligned(idx * ATT_BLOCK),
              init_state(num1_ref, m1_ref, den1_ref))
    blocks16 = nblk // d_out

    def rows16(idx, n):
        cls = idx // blocks16
        r, j = cls // d_in, cls % d_in
        return pl.ds(r * (sub_blocks * ATT_BLOCK) + j + n * ATT_BLOCK * d_in, ATT_BLOCK, stride=d_in)

    run_group((q4_ref, k4_ref, v4_ref), blocks16, rows16, init_state(num4_ref, m4_ref, den4_ref))
    run_group((q4_ref, k4b_ref, v4b_ref), sub_blocks, lambda idx, n: aligned(idx * ATT_BLOCK), merge4)

    def finish(i, carry):
        r, c = i // sub_blocks, i % sub_blocks
        tok = pl.ds(r + c * ATT_BLOCK * d_mid, ATT_BLOCK, stride=d_mid)
        res = aligned(i * ATT_BLOCK)
        m_a, m_b = m4_ref[res, :], m1_ref[tok, :]
        m_new = jnp.maximum(m_a, m_b)
        wa = jnp.exp2(m_a - m_new)
        wb = jnp.exp2(m_b - m_new)
        num = num4_ref[res, :] * wa + num1_ref[tok, :] * wb
        den = den4_ref[res, :] * wa + den1_ref[tok, :] * wb
        o_ref[tok, :] = num / den
        return carry

    lax.fori_loop(0, nblk, finish, 0, unroll=4)


def _band_bias():
    row = np.arange(2 * ATT_BLOCK)[:, None] % ATT_BLOCK
    ki = np.arange(2 * ATT_BLOCK)[None, :] - ATT_BLOCK
    rel = row - ki
    valid = (rel >= 0) & (rel <= SUB_WINDOW)
    b0 = np.where(valid, 0.0, NEG_INF).astype(np.float32)
    b1 = np.where(valid & (ki >= 0), 0.0, NEG_INF).astype(np.float32)
    return jnp.asarray(np.stack([b0, b1]))


def _attn_prompt(q, k, v):
    b, s, _ = q.shape
    assert DILATIONS[0] == 1 and DILATIONS[2] % DILATIONS[1] == 0
    assert s % (ATT_BLOCK * DILATIONS[-1]) == 0 and (s // ATT_BLOCK) % ATT_UNROLL == 0
    blk = pl.BlockSpec((None, s, LANES), lambda i, j: (i, 0, j))
    return pl.pallas_call(
        functools.partial(_attn_prompt_kernel, seq=s),
        grid=(b, ATT_WIDTH // LANES),
        in_specs=[_const_spec((2, 2 * ATT_BLOCK, 2 * ATT_BLOCK)), blk, blk, blk],
        out_specs=blk,
        out_shape=jax.ShapeDtypeStruct((b, s, ATT_WIDTH), F32),
        scratch_shapes=[pltpu.VMEM((s, LANES), F32)] * 9 + [pltpu.VMEM((s, LANES), BF16)] * 4,
        compiler_params=_cparams(("parallel", "parallel")),
        name="attn_prompt",
    )(_band_bias(), q, k, v)


def _attn_sample_body(mw_ref, mn_ref, q_ref, kn_ref, vn_ref, kc_ref, vc_ref,
                      att_ref, nk_ref, nv_ref, *, win, t_new, heads):
    mult_w = mw_ref[...]
    mult_n = mn_ref[...]
    lane = lax.broadcasted_iota(jnp.int32, (HD, LANES), 1)
    keep = lane < LANES - t_new
    top = jnp.zeros((LANES - t_new, heads * HD), F32)
    knt_all = jnp.transpose(jnp.concatenate([top, kn_ref[...]], axis=0))
    vnt_all = jnp.transpose(jnp.concatenate([top, vn_ref[...]], axis=0))
    head_rows = lambda a, h: a[h * HD:(h + 1) * HD, :]
    for h in range(heads):
        for src_ref, new_all, dst in ((kc_ref, knt_all, nk_ref), (vc_ref, vnt_all, nv_ref)):
            rolled = pltpu.roll(src_ref[h], win - t_new, 1)
            dst[h, :, 0:win - LANES] = rolled[:, 0:win - LANES]
            dst[h, :, win - LANES:win] = jnp.where(keep, rolled[:, win - LANES:win], head_rows(new_all, h))
    sc = []
    for h in range(heads):
        qh = q_ref[:, h * HD:(h + 1) * HD].astype(BF16)
        s_w = jnp.dot(qh, kc_ref[h].astype(BF16), preferred_element_type=F32)
        s_n = jnp.dot(qh, head_rows(knt_all, h).astype(BF16), preferred_element_type=F32)
        sc.append((jnp.where(mult_w > 0, s_w, NEG_INF), jnp.where(mult_n > 0, s_n, NEG_INF)))
    pr = []
    for s_w, s_n in sc:
        mx = jnp.maximum(jnp.max(s_w, axis=1, keepdims=True), jnp.max(s_n, axis=1, keepdims=True))
        p_w = mult_w * jnp.exp(s_w - mx)
        p_n = mult_n * jnp.exp(s_n - mx)
        den = jnp.sum(p_w, axis=1, keepdims=True) + jnp.sum(p_n, axis=1, keepdims=True)
        pr.append((p_w, p_n, den))
    for h, (p_w, p_n, den) in enumerate(pr):
        o = lax.dot_general(p_w.astype(BF16), vc_ref[h].astype(BF16), (((1,), (1,)), ((), ())),
                            preferred_element_type=F32)
        o = o + lax.dot_general(p_n.astype(BF16), head_rows(vnt_all, h).astype(BF16),
                                (((1,), (1,)), ((), ())), preferred_element_type=F32)
        att_ref[:, h * HD:(h + 1) * HD] = o / den


def _attn_sample_kernel(*refs, win, t_new):
    _attn_sample_body(*refs, win=win, t_new=t_new, heads=N_ATT_HEADS)


def _attn_mlp_kernel(mw_ref, mn_ref, q_ref, kn_ref, vn_ref, kc_ref, vc_ref,
                     x_ref, attp_ref, hbp_ref, wo_ref, g2_ref, wu_ref, wd_ref, gf_ref,
                     att_ref, nk_ref, nv_ref, y_ref, h2_sc, acc_sc, *, win, t_new, heads):
    c = pl.program_id(0) % FF_PARTS

    @pl.when(c == 0)
    def _():
        mix = (jnp.dot(attp_ref[...].astype(BF16), wo_ref[0:ATT_WIDTH, :], preferred_element_type=F32)
               + jnp.dot(hbp_ref[...].astype(BF16), wo_ref[ATT_WIDTH:D_MODEL, :], preferred_element_type=F32))
        x1 = x_ref[...] + mix
        h2_sc[...] = _rms(x1, g2_ref[...]).astype(BF16)
        acc_sc[...] = x1

    u = jnp.dot(h2_sc[...], wu_ref[c], preferred_element_type=F32)
    r = jnp.maximum(u, 0.0)
    acc_sc[...] += jnp.dot((r * r).astype(BF16), wd_ref[c], preferred_element_type=F32)
    _attn_sample_body(mw_ref, mn_ref, q_ref, kn_ref, vn_ref, kc_ref, vc_ref, att_ref, nk_ref, nv_ref,
                      win=win, t_new=t_new, heads=heads)

    @pl.when(c == FF_PARTS - 1)
    def _():
        y_ref[...] = _rms(acc_sc[...], gf_ref[...])


def _sample_multiplicity(win, t_new):
    t = np.arange(t_new)[:, None]
    idx = np.arange(win + t_new)[None, :]
    back = win + t - idx
    mult = np.zeros((t_new, win + t_new), np.float32)
    for dil in DILATIONS:
        mult += ((back >= 0) & (back % dil == 0) & (back // dil <= SUB_WINDOW)).astype(np.float32)
    mw = mult[:, :win]
    mn = np.zeros((t_new, LANES), np.float32)
    mn[:, LANES - t_new:] = mult[:, win:]
    return jnp.asarray(mw), jnp.asarray(mn)


def _attn_sample(q, k_new, v_new, kc_t, vc_t, t_new):
    b, nh, _, win = kc_t.shape
    assert win == DILATIONS[-1] * SUB_WINDOW and t_new == SUBLANES and q.shape[0] == b * t_new
    mw, mn = _sample_multiplicity(win, t_new)
    rows = pl.BlockSpec((t_new, ATT_WIDTH), lambda i: (i, 0))
    per_b = lambda *shape: pl.BlockSpec((None,) + shape, lambda i: (i,) + (0,) * len(shape))
    return pl.pallas_call(
        functools.partial(_attn_sample_kernel, win=win, t_new=t_new),
        grid=(b,),
        in_specs=[
            _const_spec((t_new, win)), _const_spec((t_new, LANES)),
            rows, rows, rows, per_b(nh, HD, win), per_b(nh, HD, win),
        ],
        out_specs=[rows, per_b(nh, HD, win), per_b(nh, HD, win)],
        out_shape=[
            jax.ShapeDtypeStruct((b * t_new, ATT_WIDTH), F32),
            jax.ShapeDtypeStruct((b, nh, HD, win), F32),
            jax.ShapeDtypeStruct((b, nh, HD, win), F32),
        ],
        compiler_params=_cparams(("parallel",)),
        name="attn_sample",
    )(mw, mn, q, k_new, v_new, kc_t, vc_t)


def _attn_mlp_fits(b_s, n_p):
    steps = b_s * ATT_HEAD_SPLIT
    if steps % FF_PARTS:
        return False
    tiles = steps // FF_PARTS
    return n_p % tiles == 0 and (n_p // tiles) % SUBLANES == 0 and n_p // tiles <= ROW_TILE


def _attn_mlp(q, k_new, v_new, kc_t, vc_t, t_new, x2d, attp, hbp, wo_bf, g2, wu_parts, wd_parts, gf):
    b, nh, _, win = kc_t.shape
    n_p = x2d.shape[0]
    assert win == DILATIONS[-1] * SUB_WINDOW and t_new == SUBLANES and q.shape[0] == b * t_new
    heads = nh // ATT_HEAD_SPLIT
    steps = b * ATT_HEAD_SPLIT
    tm = n_p // (steps // FF_PARTS)
    ffc = D_FF // FF_PARTS
    mw, mn = _sample_multiplicity(win, t_new)
    rows = pl.BlockSpec((t_new, heads * HD), lambda i: (i // ATT_HEAD_SPLIT, i % ATT_HEAD_SPLIT))
    win_blk = pl.BlockSpec((None, heads, HD, win), lambda i: (i // ATT_HEAD_SPLIT, i % ATT_HEAD_SPLIT, 0, 0))
    tile = lambda w: pl.BlockSpec((tm, w), lambda i: (i // FF_PARTS, 0))
    return pl.pallas_call(
        functools.partial(_attn_mlp_kernel, win=win, t_new=t_new, heads=heads),
        grid=(steps,),
        in_specs=[
            _const_spec((t_new, win)), _const_spec((t_new, LANES)),
            rows, rows, rows, win_blk, win_blk,
            tile(D_MODEL), tile(ATT_WIDTH), tile(MLSTM_WIDTH),
            _const_spec((D_MODEL, D_MODEL)), _const_spec((1, D_MODEL)),
            _const_spec((FF_PARTS, D_MODEL, ffc)), _const_spec((FF_PARTS, ffc, D_MODEL)),
            _const_spec((1, D_MODEL)),
        ],
        out_specs=[rows, win_blk, win_blk, tile(D_MODEL)],
        out_shape=[
            jax.ShapeDtypeStruct((b * t_new, ATT_WIDTH), F32),
            jax.ShapeDtypeStruct((b, nh, HD, win), F32),
            jax.ShapeDtypeStruct((b, nh, HD, win), F32),
            jax.ShapeDtypeStruct((n_p, D_MODEL), F32),
        ],
        scratch_shapes=[pltpu.VMEM((tm, D_MODEL), BF16), pltpu.VMEM((tm, D_MODEL), F32)],
        compiler_params=_cparams(("arbitrary",)),
        name="attn_mlp",
    )(mw, mn, q, k_new, v_new, kc_t, vc_t, x2d, attp, hbp, wo_bf, g2, wu_parts, wd_parts, gf)


def _split_dot(tri, x):
    hi = x.astype(BF16)
    lo = (x - hi.astype(F32)).astype(BF16)
    return (jnp.dot(tri, hi, preferred_element_type=F32)
            + jnp.dot(tri, lo, preferred_element_type=F32))


def _split_dot_rows(x, tri):
    hi = x.astype(BF16)
    lo = (x - hi.astype(F32)).astype(BF16)
    return (jnp.dot(hi, tri, preferred_element_type=F32)
            + jnp.dot(lo, tri, preferred_element_type=F32))


def _log_sigmoid(x):
    return jnp.minimum(x, 0.0) - jnp.log1p(jnp.exp(-jnp.abs(x)))


def _mlstm_init(xp_ref, c_sc, n_sc, m_sc, nm_sc, group, state_refs=None):
    if state_refs is not None:
        tail_ref, c0_ref, n0_ref, m0_ref = state_refs
        for c in range(2 * MLSTM_HEADS):
            xp_ref[:, c, 0:SUBLANES, :] = tail_ref[:, :, c * LANES:(c + 1) * LANES]
        c_sc[...] = c0_ref[...]
        n_sc[...] = n0_ref[...]
        m_sc[...] = m0_ref[...]
    else:
        xp_ref[:, :, 0:SUBLANES, :] = jnp.zeros((group, 2 * MLSTM_HEADS, SUBLANES, LANES), F32)
        c_sc[...] = jnp.zeros_like(c_sc)
        n_sc[...] = jnp.zeros_like(n_sc)
        m_sc[...] = jnp.zeros_like(m_sc)
        nm_sc[...] = jnp.zeros_like(nm_sc)


def _mlstm_emit(c_out_ref, n_out_ref, m_out_ref, c_sc, n_sc, m_sc, nm_sc, group, mxu_sums):
    if mxu_sums:
        for g in range(group):
            for h in range(MLSTM_HEADS):
                n_sc[g, h:h + 1, :] = jnp.transpose(nm_sc[g, h])[0:1, :]
    c_out_ref[...] = c_sc[...]
    n_out_ref[...] = n_sc[...]
    m_out_ref[...] = m_sc[...]


def _mlstm_chunk(raws, vbs, obs, gate_list, hbs, tails, wc_ref, bc_ref, g_ref,
                 xp_ref, act_ref, c_sc, n_sc, m_sc, nm_sc, *, chunk, mxu_sums):
    L = chunk
    group = len(raws)
    r_i = lax.broadcasted_iota(jnp.int32, (L, L), 0)
    c_i = lax.broadcasted_iota(jnp.int32, (L, L), 1)
    causal = r_i >= c_i
    tri_l = jnp.where(causal, 1.0, 0.0).astype(BF16)
    tri_u = jnp.where(r_i <= c_i, 1.0, 0.0).astype(BF16)
    first_tap = SUBLANES - (CONV_W - 1)
    bc = bc_ref[...]
    taps = [wc_ref[w:w + 1, :] for w in range(CONV_W)]
    ones_sq = jnp.ones((LANES, LANES), BF16)

    per_batch = []
    for g in range(group):
        tails[g][...] = raws[g][L - SUBLANES:L, :]
        for c in range(2 * MLSTM_HEADS):
            cl = slice(c * LANES, (c + 1) * LANES)
            raw = raws[g][:, cl]
            xp_ref[g, c, SUBLANES:SUBLANES + L, :] = raw
            if L % (SUBLANES * SUBLANES) == 0:
                n_rows = L // SUBLANES
                slabs = [xp_ref[g, c, pl.ds(first_tap + j, n_rows, stride=SUBLANES), :]
                         for j in range(SUBLANES + CONV_W - 1)]
                for s in range(SUBLANES):
                    conv = bc[:, cl]
                    for w in range(CONV_W):
                        conv = conv + slabs[s + w] * taps[w][:, cl]
                    act_ref[g, c, pl.ds(s, n_rows, stride=SUBLANES), :] = conv * jax.nn.sigmoid(conv)
            else:
                conv = bc[:, cl]
                for w in range(CONV_W):
                    conv = conv + xp_ref[g, c, first_tap + w:first_tap + w + L, :] * taps[w][:, cl]
                act_ref[g, c] = conv * jax.nn.sigmoid(conv)
            xp_ref[g, c, 0:SUBLANES, :] = raw[L - SUBLANES:L, :]

        gates = gate_list[g][...]
        lf = _log_sigmoid(gates)
        b_col = _split_dot(tri_l, lf)
        if L == LANES:
            gates_sq = gates
        else:
            gates_sq = jnp.concatenate([gates, jnp.zeros((LANES - L, LANES), F32)], axis=0)
        gates_t = jnp.transpose(gates_sq)[0:SUBLANES, 0:L]
        lf_t = _log_sigmoid(gates_t)
        b_row = _split_dot_rows(lf_t, tri_u)

        per_batch.append((gates, b_col, gates_t, b_row))

    items = [(g, h) for g in range(group) for h in range(MLSTM_HEADS)]
    st = {}
    for g, h in items:
        gates, b_col, gates_t, b_row = per_batch[g]
        ig_c = gates[:, h:h + 1]
        b_c = b_col[:, MLSTM_HEADS + h:MLSTM_HEADS + h + 1]
        ig_r = gates_t[h:h + 1, :]
        b_r = b_row[MLSTM_HEADS + h:MLSTM_HEADS + h + 1, :]
        if mxu_sums:
            b_c_wide = jnp.broadcast_to(b_c, (L, LANES))
            a_c = b_c_wide + m_sc[g, h:h + 1, 0:1]
            dm = jnp.where(causal, b_c_wide - b_r + ig_r, NEG_INF)
            mt = jnp.maximum(a_c, jnp.broadcast_to(jnp.max(dm, axis=1, keepdims=True), (L, LANES)))
        else:
            a_c = b_c + m_sc[g, h:h + 1, 0:1]
            dm = jnp.where(causal, b_c - b_r + ig_r, NEG_INF)
            mt = jnp.maximum(a_c, jnp.max(dm, axis=1, keepdims=True))
        st[g, h] = dict(ig_c=ig_c, b_c=b_c, a_c=a_c, dm=dm, mt=mt)
    for g, h in items:
        d = st[g, h]
        q = act_ref[g, h]
        k = act_ref[g, MLSTM_HEADS + h] * (MLSTM_HD ** -0.5)
        v = vbs[g][:, h * MLSTM_HD:(h + 1) * MLSTM_HD]
        qb, kb = q.astype(BF16), k.astype(BF16)
        w_inter = jnp.exp(d["a_c"] - d["mt"])
        wm = jnp.exp(d["dm"] - d["mt"])
        sc = lax.dot_general(qb, kb, (((1,), (1,)), ((), ())), preferred_element_type=F32) * wm
        d.update(q=q, k=k, v=v, qb=qb, vb=v.astype(BF16), w_inter=w_inter, sc=sc)
        del d["dm"]
    for g, h in items:
        d = st[g, h]
        c_old = c_sc[g, h]
        if mxu_sums:
            nm_old = nm_sc[g, h]
            v_ones = jnp.concatenate([d["vb"], ones_sq], axis=1)
            state = jnp.concatenate([c_old, nm_old], axis=1).astype(BF16)
            inter = jnp.dot(d["qb"], state, preferred_element_type=F32)
            intra = jnp.dot(d["sc"].astype(BF16), v_ones, preferred_element_type=F32)
            num = d["w_inter"] * inter[:, :LANES] + intra[:, :LANES]
            den = d["w_inter"] * inter[:, LANES:] + intra[:, LANES:]
            d.update(v_ones=v_ones, nm_old=nm_old)
        else:
            n_old = n_sc[g, h:h + 1, :]
            num = (d["w_inter"] * jnp.dot(d["qb"], c_old.astype(BF16), preferred_element_type=F32)
                   + jnp.dot(d["sc"].astype(BF16), d["vb"], preferred_element_type=F32))
            den = (d["w_inter"] * jnp.sum(d["q"] * n_old, axis=1, keepdims=True)
                   + jnp.sum(d["sc"], axis=1, keepdims=True))
            d.update(n_old=n_old)
        d.update(c_old=c_old, hh=num / jnp.maximum(jnp.abs(den), jnp.exp(-d["mt"])))
        del d["sc"]
    for g, h in items:
        d = st[g, h]
        mt, a_c, b_c = d["mt"], d["a_c"], d["b_c"]
        m_last = mt[L - 1:L, 0:1]
        w_last = jnp.exp(a_c[L - 1:L, 0:1] - m_last)
        w_t = jnp.exp(b_c[L - 1:L, :] - b_c + d["ig_c"] - m_last)
        kw = d["k"] * w_t
        if mxu_sums:
            upd = jnp.dot(jnp.transpose(kw).astype(BF16), d["v_ones"], preferred_element_type=F32)
            c_sc[g, h] = w_last * d["c_old"] + upd[:, :LANES]
            nm_sc[g, h] = w_last * d["nm_old"] + upd[:, LANES:]
        else:
            if L == LANES:
                kw_sq, v_sq = kw, d["v"]
            else:
                pad = jnp.zeros((LANES - L, MLSTM_HD), F32)
                kw_sq = jnp.concatenate([kw, pad], axis=0)
                v_sq = jnp.concatenate([d["v"], pad], axis=0)
            c_sc[g, h] = w_last * d["c_old"] + jnp.dot(jnp.transpose(kw_sq).astype(BF16), v_sq.astype(BF16),
                                                       preferred_element_type=F32)
            n_sc[g, h:h + 1, :] = w_last * d["n_old"] + jnp.sum(kw, axis=0, keepdims=True)
        m_sc[g, h:h + 1, :] = jnp.broadcast_to(m_last, (1, LANES))
    for g, h in items:
        hh = st[g, h]["hh"]
        sl = slice(h * MLSTM_HD, (h + 1) * MLSTM_HD)
        if mxu_sums:
            mean_sq = _split_dot_rows(hh * hh, ones_sq) * (1.0 / MLSTM_HD)
        else:
            mean_sq = jnp.mean(hh * hh, axis=1, keepdims=True)
        hn = hh * lax.rsqrt(mean_sq + EPS)
        hbs[g][:, sl] = hn * g_ref[:, sl] * jax.nn.sigmoid(obs[g][:, sl])


def _mlstm_kernel(*refs, chunk, group, has_state):
    if has_state:
        (raw_ref, vb_ref, ob_ref, gate_ref, wc_ref, bc_ref, g_ref, tail_ref, c0_ref, n0_ref, m0_ref,
         hb_ref, tail_out_ref, c_out_ref, n_out_ref, m_out_ref, xp_ref, act_ref, c_sc, n_sc, m_sc) = refs
        nm_sc, state_refs = None, (tail_ref, c0_ref, n0_ref, m0_ref)
    else:
        (raw_ref, vb_ref, ob_ref, gate_ref, wc_ref, bc_ref, g_ref,
         hb_ref, tail_out_ref, c_out_ref, n_out_ref, m_out_ref, xp_ref, act_ref, c_sc, n_sc, m_sc, nm_sc) = refs
        state_refs = None
    c_idx = pl.program_id(1)
    mxu_sums = nm_sc is not None and chunk == LANES

    @pl.when(c_idx == 0)
    def _():
        _mlstm_init(xp_ref, c_sc, n_sc, m_sc, nm_sc, group, state_refs)

    per_g = lambda ref: [ref.at[g] for g in range(group)]
    _mlstm_chunk(per_g(raw_ref), per_g(vb_ref), per_g(ob_ref), per_g(gate_ref), per_g(hb_ref), per_g(tail_out_ref),
                 wc_ref, bc_ref, g_ref, xp_ref, act_ref, c_sc, n_sc, m_sc, nm_sc, chunk=chunk, mxu_sums=mxu_sums)

    @pl.when(c_idx == pl.num_programs(1) - 1)
    def _():
        _mlstm_emit(c_out_ref, n_out_ref, m_out_ref, c_sc, n_sc, m_sc, nm_sc, group, mxu_sums)


def _mlstm(raw, vb, ob, gates, w_conv, b_conv, mh_g, chunk, group, state=None):
    b, t, _ = raw.shape
    nc = t // chunk
    assert nc * chunk == t and chunk % SUBLANES == 0 and b % group == 0
    seq = lambda w: pl.BlockSpec((group, chunk, w), lambda i, j: (i, j, 0))
    per_b = lambda *shape: pl.BlockSpec((group,) + shape, lambda i, j: (i,) + (0,) * len(shape))
    in_specs = [seq(2 * MLSTM_WIDTH), seq(MLSTM_WIDTH), seq(MLSTM_WIDTH), seq(GATE_PAD),
                _const_spec((CONV_W, 2 * MLSTM_WIDTH)), _const_spec((1, 2 * MLSTM_WIDTH)),
                _const_spec((1, MLSTM_WIDTH))]
    args = [raw, vb, ob, gates, w_conv, b_conv, mh_g]
    state_shapes = [(SUBLANES, 2 * MLSTM_WIDTH), (MLSTM_HEADS, MLSTM_HD, MLSTM_HD),
                    (MLSTM_HEADS, MLSTM_HD), (MLSTM_HEADS, LANES)]
    state_specs = [per_b(*sh) for sh in state_shapes]
    if state is not None:
        in_specs += state_specs
        args += list(state)
    return pl.pallas_call(
        functools.partial(_mlstm_kernel, chunk=chunk, group=group, has_state=state is not None),
        grid=(b // group, nc),
        in_specs=in_specs,
        out_specs=[seq(MLSTM_WIDTH)] + state_specs,
        out_shape=[jax.ShapeDtypeStruct((b, t, MLSTM_WIDTH), F32)]
        + [jax.ShapeDtypeStruct((b,) + sh, F32) for sh in state_shapes],
        scratch_shapes=[
            pltpu.VMEM((group, 2 * MLSTM_HEADS, SUBLANES + chunk, LANES), F32),
            pltpu.VMEM((group, 2 * MLSTM_HEADS, chunk, LANES), F32),
        ] + [pltpu.VMEM((group,) + sh, F32) for sh in state_shapes[1:]]
        + ([] if state is not None else [pltpu.VMEM((group,) + state_shapes[1], F32)]),
        compiler_params=_cparams(("parallel", "arbitrary")),
        name="mlstm_state" if state is not None else "mlstm",
    )(*args)


def _inproj_mlstm_kernel(x_ref, g_ref, w_ref, wg_ref, bg_ref, cos_ref, sa_ref, sb_ref, wc_ref, bc_ref, mhg_ref,
                         q_ref, k_ref, v_ref, kt_ref, vt_ref, hb_ref, tail_ref, c_out_ref, n_out_ref, m_out_ref,
                         *scratch, tiles_per_seq, window_tiles):
    stage_a, stage_b = scratch[0:4], scratch[4:8]
    xp_ref, act_ref, c_sc, n_sc, m_sc, nm_sc = scratch[8:]
    i = pl.program_id(0)
    tm = hb_ref.shape[0]

    @pl.when(i == 0)
    def _():
        for st in stage_b:
            st[...] = jnp.zeros(st.shape, F32)

    prev_tile_in_seq = jnp.maximum(i - 1, 0) % tiles_per_seq

    @pl.when(prev_tile_in_seq == 0)
    def _():
        _mlstm_init(xp_ref, c_sc, n_sc, m_sc, nm_sc, 1)

    def step(write, read):
        _inproj_kernel(x_ref, g_ref, w_ref, wg_ref, bg_ref, cos_ref, sa_ref, sb_ref, q_ref, k_ref, v_ref,
                       *write, kt_ref, vt_ref, tiles_per_seq=tiles_per_seq, window_tiles=window_tiles)
        raw, vb, ob, gate = read
        for cc in range(tm // CHUNK):
            rows = pl.ds(cc * CHUNK, CHUNK)
            _mlstm_chunk([raw.at[rows, :]], [vb.at[rows, :]], [ob.at[rows, :]], [gate.at[rows, :]],
                         [hb_ref.at[rows, :]], [tail_ref], wc_ref, bc_ref, mhg_ref,
                         xp_ref, act_ref, c_sc, n_sc, m_sc, nm_sc, chunk=CHUNK, mxu_sums=True)

    @pl.when(i % 2 == 0)
    def _():
        step(stage_a, stage_b)

    @pl.when(i % 2 == 1)
    def _():
        step(stage_b, stage_a)

    @pl.when(prev_tile_in_seq == tiles_per_seq - 1)
    def _():
        _mlstm_emit(c_out_ref, n_out_ref, m_out_ref, c_sc, n_sc, m_sc, nm_sc, 1, True)


def _inproj_mlstm(x2d, g1, w_main_bf, w_gate_bf, bg_pad, tables, w_conv, b_conv, mh_g, tm, tiles_per_seq,
                  window_tiles):
    n = x2d.shape[0]
    n_tiles = n // tm
    b = n_tiles // tiles_per_seq
    assert window_tiles >= 1 and tm % CHUNK == 0 and CHUNK == LANES
    cur = lambda i: jnp.minimum(i, n_tiles - 1)
    prv = lambda i: jnp.maximum(i - 1, 0)
    row = lambda w: pl.BlockSpec((tm, w), lambda i: (cur(i), 0))
    tab = pl.BlockSpec((tm, LANES), lambda i: (cur(i) % tiles_per_seq, 0))
    first = tiles_per_seq - window_tiles
    win = pl.BlockSpec((None, ATT_WIDTH, tm),
                       lambda i: (cur(i) // tiles_per_seq, 0, jnp.maximum(cur(i) % tiles_per_seq - first, 0)))
    seq_state = lambda *shape: pl.BlockSpec((1,) + shape, lambda i: (prv(i) // tiles_per_seq,) + (0,) * len(shape))
    state_shapes = [(MLSTM_HEADS, MLSTM_HD, MLSTM_HD), (MLSTM_HEADS, MLSTM_HD), (MLSTM_HEADS, LANES)]
    return pl.pallas_call(
        functools.partial(_inproj_mlstm_kernel, tiles_per_seq=tiles_per_seq, window_tiles=window_tiles),
        grid=(n_tiles + 1,),
        in_specs=[
            row(D_MODEL), _const_spec((1, D_MODEL)),
            _const_spec((D_MODEL, IN_COLS - 2 * MLSTM_HEADS)), _const_spec((D_MODEL, GATE_PAD)),
            _const_spec((1, GATE_PAD)), tab, tab, tab,
            _const_spec((CONV_W, 2 * MLSTM_WIDTH)), _const_spec((1, 2 * MLSTM_WIDTH)), _const_spec((1, MLSTM_WIDTH)),
        ],
        out_specs=[row(ATT_WIDTH), row(ATT_WIDTH), row(ATT_WIDTH), win, win,
                   pl.BlockSpec((tm, MLSTM_WIDTH), lambda i: (prv(i), 0)),
                   pl.BlockSpec((None, SUBLANES, 2 * MLSTM_WIDTH), lambda i: (prv(i) // tiles_per_seq, 0, 0))]
        + [seq_state(*sh) for sh in state_shapes],
        out_shape=[jax.ShapeDtypeStruct((n, ATT_WIDTH), F32)] * 3
        + [jax.ShapeDtypeStruct((b, ATT_WIDTH, window_tiles * tm), F32)] * 2
        + [jax.ShapeDtypeStruct((n, MLSTM_WIDTH), F32), jax.ShapeDtypeStruct((b, SUBLANES, 2 * MLSTM_WIDTH), F32)]
        + [jax.ShapeDtypeStruct((b,) + sh, F32) for sh in state_shapes],
        scratch_shapes=[pltpu.VMEM((tm, w), F32) for w in (2 * MLSTM_WIDTH, MLSTM_WIDTH, MLSTM_WIDTH, GATE_PAD)] * 2
        + [pltpu.VMEM((1, 2 * MLSTM_HEADS, SUBLANES + CHUNK, LANES), F32),
           pltpu.VMEM((1, 2 * MLSTM_HEADS, CHUNK, LANES), F32)]
        + [pltpu.VMEM((1,) + sh, F32) for sh in state_shapes]
        + [pltpu.VMEM((1,) + state_shapes[0], F32)],
        compiler_params=_cparams(("arbitrary",)),
        name="inproj_mlstm",
    )(x2d, g1, w_main_bf, w_gate_bf, bg_pad, *tables, w_conv, b_conv, mh_g)


def _outmlp_kernel(x_ref, att_ref, hb_ref, wo_ref, g2_ref, wu_ref, wd_ref, gf_ref, y_ref):
    mix = (jnp.dot(att_ref[...].astype(BF16), wo_ref[0:ATT_WIDTH, :], preferred_element_type=F32)
           + jnp.dot(hb_ref[...].astype(BF16), wo_ref[ATT_WIDTH:D_MODEL, :], preferred_element_type=F32))
    x1 = x_ref[...] + mix
    h2 = _rms(x1, g2_ref[...]).astype(BF16)
    acc = x1
    for c in range(FF_PARTS):
        u = jnp.dot(h2, wu_ref[c], preferred_element_type=F32)
        r = jnp.maximum(u, 0.0)
        acc = acc + jnp.dot((r * r).astype(BF16), wd_ref[c], preferred_element_type=F32)
    y_ref[...] = _rms(acc, gf_ref[...])


def _outmlp(x2d, att2d, hb2d, wo_bf, g2, wu_parts, wd_parts, gf, tm):
    n = x2d.shape[0]
    ffc = D_FF // FF_PARTS
    row = lambda w: pl.BlockSpec((tm, w), lambda i: (i, 0))
    return pl.pallas_call(
        _outmlp_kernel,
        grid=(n // tm,),
        in_specs=[row(D_MODEL), row(ATT_WIDTH), row(MLSTM_WIDTH),
                  _const_spec((D_MODEL, D_MODEL)), _const_spec((1, D_MODEL)),
                  _const_spec((FF_PARTS, D_MODEL, ffc)), _const_spec((FF_PARTS, ffc, D_MODEL)),
                  _const_spec((1, D_MODEL))],
        out_specs=row(D_MODEL),
        out_shape=jax.ShapeDtypeStruct((n, D_MODEL), F32),
        compiler_params=_cparams(("parallel",)),
        name="outmlp",
    )(x2d, att2d, hb2d, wo_bf, g2, wu_parts, wd_parts, gf)


def _project(x, pos, mix_params, tm, tile_tables, window=0):
    g1, w_main_bf, w_gate_bf, bg_pad = mix_params
    b, t, _ = x.shape
    x2d = x.reshape(b * t, D_MODEL)
    tm = min(tm, b * t)
    tables = _rotary_tables(pos)
    if tile_tables:
        tables = tuple(np.tile(a, (tm // t, 1)) for a in tables)
        table_blocks, tiles_per_seq = 1, 1
    else:
        table_blocks = tiles_per_seq = t // tm
    assert window % tm == 0
    outs = _inproj(x2d, g1, w_main_bf, w_gate_bf, bg_pad, tuple(jnp.asarray(a) for a in tables), tm,
                   table_blocks, tiles_per_seq, window // tm)
    return x2d, tm, outs


def kernel(x_prompt, x_sample, cache_win_k, cache_win_v, state_conv, state_C, state_n, state_m, norm1_g, w_in, b_gate, w_conv, b_conv, mh_norm_g, w_out, norm2_g, w_up, w_down, norm_f_g):
    depth = w_in.shape[0]
    assert depth == 1, "the final norm is fused into the (single) layer's MLP kernel"
    l = 0
    n_gate = 2 * MLSTM_HEADS
    w_l = w_in[l]
    w_main_bf = w_l[:, :IN_COLS - n_gate].astype(BF16)
    w_gate_bf = jnp.pad(w_l[:, IN_COLS - n_gate:], ((0, 0), (0, GATE_PAD - n_gate))).astype(BF16)
    bg_pad = jnp.pad(b_gate[l], (0, GATE_PAD - n_gate))[None, :]
    mix_params = (norm1_g[l][None, :], w_main_bf, w_gate_bf, bg_pad)
    conv_params = (w_conv[l], b_conv[l][None, :], mh_norm_g[l][None, :])
    ffc = D_FF // FF_PARTS
    wu_parts = w_up[l].reshape(D_MODEL, FF_PARTS, ffc).transpose(1, 0, 2).astype(BF16)
    wd_parts = w_down[l].astype(BF16).reshape(FF_PARTS, ffc, D_MODEL)
    mlp_params = (w_out[l].astype(BF16), norm2_g[l][None, :], wu_parts, wd_parts, norm_f_g[None, :])

    bp, tp, _ = x_prompt.shape
    n_keep = min(DILATIONS[-1] * SUB_WINDOW, tp)
    r3 = lambda a: a.reshape(bp, tp, a.shape[-1])
    if tp % ROW_TILE == 0 and n_keep % ROW_TILE == 0 and ROW_TILE % CHUNK == 0:
        xp2d, tm_p = x_prompt.reshape(bp * tp, D_MODEL), ROW_TILE
        tables = tuple(jnp.asarray(a) for a in _rotary_tables(np.arange(tp)))
        q, k, v, kt, vt, hb_p, tail_p, pc, pn, pm = _inproj_mlstm(
            xp2d, *mix_params, tables, *conv_params, ROW_TILE, tp // ROW_TILE, n_keep // ROW_TILE)
    else:
        xp2d, tm_p, (q, k, v, raw, vb, ob, gates, kt, vt) = _project(x_prompt, np.arange(tp), mix_params, ROW_TILE,
                                                                   False, window=n_keep)
        hb_p, tail_p, pc, pn, pm = _mlstm(r3(raw), r3(vb), r3(ob), r3(gates), *conv_params,
                                          CHUNK if tp % CHUNK == 0 else tp,
                                          MLSTM_PROMPT_GROUP if bp % MLSTM_PROMPT_GROUP == 0 else 1)
        hb_p = hb_p.reshape(bp * tp, MLSTM_WIDTH)
    att_p = _attn_prompt(r3(q), r3(k), r3(v)).reshape(bp * tp, ATT_WIDTH)
    heads = lambda a: a.reshape(bp, N_ATT_HEADS, HD, n_keep).transpose(0, 3, 1, 2)
    pk, pv = heads(kt), heads(vt)

    bs, ts, _ = x_sample.shape
    xs2d, tm_s, (q, k, v, raw, vb, ob, gates) = _project(x_sample, PAST_LEN + np.arange(ts), mix_params, ROW_TILE, True)
    r3 = lambda a: a.reshape(bs, ts, a.shape[-1])
    tail0 = jnp.pad(state_conv[l], ((0, 0), (SUBLANES - (CONV_W - 1), 0), (0, 0)))
    m0b = jnp.broadcast_to(state_m[l][:, :, None], (bs, MLSTM_HEADS, LANES))
    hb_s, tail_s, sc, sn, sm = _mlstm(r3(raw), r3(vb), r3(ob), r3(gates), *conv_params, ts,
                                      MLSTM_SAMPLE_GROUP if bs % MLSTM_SAMPLE_GROUP == 0 else 1,
                                      state=(tail0, state_C[l], state_n[l], m0b))
    hb_s = hb_s.reshape(bs * ts, MLSTM_WIDTH)

    kc_t = cache_win_k[l].transpose(0, 2, 3, 1)
    vc_t = cache_win_v[l].transpose(0, 2, 3, 1)
    if _attn_mlp_fits(bs, bp * tp):
        att_s, nk_t, nv_t, y_p = _attn_mlp(q, k, v, kc_t, vc_t, ts, xp2d, att_p, hb_p, *mlp_params)
    else:
        att_s, nk_t, nv_t = _attn_sample(q, k, v, kc_t, vc_t, ts)
        y_p = _outmlp(xp2d, att_p, hb_p, *mlp_params, tm_p)
    sk, sv = nk_t.transpose(0, 3, 1, 2), nv_t.transpose(0, 3, 1, 2)
    y_s = _outmlp(xs2d, att_s, hb_s, *mlp_params, tm_s)

    first_tail = SUBLANES - (CONV_W - 1)
    outs = (y_p.reshape(bp, tp, D_MODEL), y_s.reshape(bs, ts, D_MODEL),
            pk, pv, tail_p[:, first_tail:, :], pc, pn, pm[:, :, 0],
            sk, sv, tail_s[:, first_tail:, :], sc, sn, sm[:, :, 0])
    return outs[:2] + tuple(o[None] for o in outs[2:])
```

```python
import functools

import jax
import jax.numpy as jnp
import numpy as np
from jax import lax
from jax.experimental import pallas as pl
from jax.experimental.pallas import tpu as pltpu

D_MODEL = 1024
HD = 64
N_ATT_HEADS = 8
ATT_WIDTH = N_ATT_HEADS * HD
MLSTM_HEADS = 4
MLSTM_WIDTH = D_MODEL - ATT_WIDTH
MLSTM_HD = MLSTM_WIDTH // MLSTM_HEADS
ROT_DIM = HD // 4
ROT_HALF = ROT_DIM // 2
ROPE_THETA = 500000.0
DILATIONS = (1, 4, 16)
SUB_WINDOW = 128
ATT_BLOCK = 128
ROW_TILE = 512
ATT_UNROLL = 16
MLSTM_SAMPLE_GROUP = 8
MLSTM_PROMPT_GROUP = 2
FF_PARTS = 4
ATT_HEAD_SPLIT = 2
CONV_W = 4
CHUNK = 128
D_FF = 4 * D_MODEL
EPS = 1e-6
PAST_LEN = 8192
IN_SIZES = (ATT_WIDTH, ATT_WIDTH, ATT_WIDTH, 2 * MLSTM_WIDTH, MLSTM_WIDTH, MLSTM_WIDTH, 2 * MLSTM_HEADS)
IN_COLS = sum(IN_SIZES)

LANES = 128
SUBLANES = 8
GATE_PAD = LANES
VMEM_LIMIT = 56 * 1024 * 1024

F32 = jnp.float32
BF16 = jnp.bfloat16
NEG_INF = float("-inf")
LOG2E = 1.4426950408889634


def _cparams(sem):
    return pltpu.CompilerParams(dimension_semantics=sem, vmem_limit_bytes=VMEM_LIMIT)


def _const_spec(shape):
    nd = len(shape)
    return pl.BlockSpec(shape, lambda *_: (0,) * nd, pipeline_mode=pl.Buffered(1))


def _rms(x, g):
    return x * lax.rsqrt(jnp.mean(x * x, axis=-1, keepdims=True) + EPS) * g


def _inproj_kernel(x_ref, g_ref, w_ref, wg_ref, bg_ref, cos_ref, sa_ref, sb_ref,
                   q_ref, k_ref, v_ref, raw_ref, vb_ref, ob_ref, gate_ref, *win_refs,
                   tiles_per_seq, window_tiles):
    stacked = len(x_ref.shape) == 3
    n_seq = x_ref.shape[0] if stacked else 1
    tm = x_ref.shape[-2]
    flat = lambda a: a.reshape(n_seq * tm, a.shape[-1]) if stacked else a
    unflat = lambda a: a.reshape(n_seq, tm, a.shape[-1]) if stacked else a
    per_seq = lambda t: jnp.concatenate([t] * n_seq, axis=0) if stacked else t
    h = _rms(flat(x_ref[...]), g_ref[...]).astype(BF16)

    def proj(lo, width):
        return jnp.dot(h, w_ref[:, lo:lo + width], preferred_element_type=F32)

    cos, sa, sb = (per_seq(r[...]) for r in (cos_ref, sa_ref, sb_ref))

    def rotary_store(dst, y, scale):
        for c in range(ATT_WIDTH // LANES):
            yc = y[:, c * LANES:(c + 1) * LANES]
            up = pltpu.roll(yc, LANES - ROT_HALF, 1)
            dn = pltpu.roll(yc, ROT_HALF, 1)
            r = yc * cos + up * sa + dn * sb
            dst[..., c * LANES:(c + 1) * LANES] = unflat(r * scale if scale != 1.0 else r)

    off = 0
    rotary_store(q_ref, proj(off, ATT_WIDTH), HD ** -0.5)
    off += ATT_WIDTH
    rotary_store(k_ref, proj(off, ATT_WIDTH), 1.0)
    off += ATT_WIDTH
    v_ref[...] = unflat(proj(off, ATT_WIDTH))
    off += ATT_WIDTH
    raw_ref[...] = unflat(proj(off, 2 * MLSTM_WIDTH))
    off += 2 * MLSTM_WIDTH
    vb_ref[...] = unflat(proj(off, MLSTM_WIDTH))
    off += MLSTM_WIDTH
    ob_ref[...] = unflat(proj(off, MLSTM_WIDTH))
    gate_ref[...] = unflat(jnp.dot(h, wg_ref[...], preferred_element_type=F32) + bg_ref[...])

    if window_tiles:
        kt_ref, vt_ref = win_refs
        if stacked:
            for s in range(n_seq):
                kt_ref[s] = jnp.transpose(k_ref[s])
                vt_ref[s] = jnp.transpose(v_ref[s])
        else:
            kt_ref[...] = jnp.transpose(k_ref[...])
            vt_ref[...] = jnp.transpose(v_ref[...])


def _inproj(x2d, g1, w_main_bf, w_gate_bf, bg_pad, tables, tm, table_blocks, tiles_per_seq=1, window_tiles=0):
    n = x2d.shape[0]
    grid = (n // tm,)
    row = lambda i: (i, 0)
    tab = lambda i: (i % table_blocks, 0)
    widths = (ATT_WIDTH, ATT_WIDTH, ATT_WIDTH, 2 * MLSTM_WIDTH, MLSTM_WIDTH, MLSTM_WIDTH, GATE_PAD)
    out_specs = [pl.BlockSpec((tm, w), row) for w in widths]
    out_shape = [jax.ShapeDtypeStruct((n, w), F32) for w in widths]
    if window_tiles:
        first = tiles_per_seq - window_tiles
        win = pl.BlockSpec((None, ATT_WIDTH, tm),
                           lambda i: (i // tiles_per_seq, 0, jnp.maximum(i % tiles_per_seq - first, 0)))
        out_specs += [win, win]
        out_shape += [jax.ShapeDtypeStruct((n // (tm * tiles_per_seq), ATT_WIDTH, window_tiles * tm), F32)] * 2
    return pl.pallas_call(
        functools.partial(_inproj_kernel, tiles_per_seq=tiles_per_seq, window_tiles=window_tiles),
        grid=grid,
        in_specs=[
            pl.BlockSpec((tm, D_MODEL), row),
            _const_spec((1, D_MODEL)),
            _const_spec((D_MODEL, IN_COLS - 2 * MLSTM_HEADS)),
            _const_spec((D_MODEL, GATE_PAD)),
            _const_spec((1, GATE_PAD)),
            pl.BlockSpec((tm, LANES), tab),
            pl.BlockSpec((tm, LANES), tab),
            pl.BlockSpec((tm, LANES), tab),
        ],
        out_specs=out_specs,
        out_shape=out_shape,
        compiler_params=_cparams(("arbitrary",)),
        name="inproj",
    )(x2d, g1, w_main_bf, w_gate_bf, bg_pad, *tables)


def _rotary_tables(pos):
    half = np.arange(ROT_HALF, dtype=np.float64)
    ang = np.asarray(pos, np.float64)[:, None] * (ROPE_THETA ** (-half / ROT_HALF))[None, :]
    cos, sin = np.cos(ang), np.sin(ang)
    p = ang.shape[0]
    ones = np.ones((p, HD - ROT_DIM))
    zeros = np.zeros((p, HD - ROT_DIM))
    z8 = np.zeros((p, ROT_HALF))
    cos_h = np.concatenate([cos, cos, ones], axis=1)
    sa_h = np.concatenate([-sin, z8, zeros], axis=1)
    sb_h = np.concatenate([z8, sin, zeros], axis=1)
    two = lambda t: np.concatenate([t, t], axis=1).astype(np.float32)
    return two(cos_h), two(sa_h), two(sb_h)


def _attn_prompt_kernel(bias_ref, q_ref, k_ref, v_ref, o_ref,
                        q4_ref, k4_ref, v4_ref, num1_ref, m1_ref, den1_ref, num4_ref, m4_ref, den4_ref,
                        kb_ref, vb_ref, k4b_ref, v4b_ref, *, seq):
    d_mid, d_out = DILATIONS[1], DILATIONS[2]
    d_in = d_out // d_mid
    lane = lax.broadcasted_iota(jnp.int32, (ATT_BLOCK, LANES), 1)
    head0 = lane < HD
    nblk = seq // ATT_BLOCK
    sub_blocks = nblk // d_mid
    ones_cols = jnp.ones((2 * ATT_BLOCK, LANES), BF16)

    def scores(q, kk, first):
        q = q * LOG2E
        q2 = jnp.concatenate([jnp.where(head0, q, 0.0), jnp.where(head0, 0.0, q)], axis=0).astype(BF16)
        s = lax.dot_general(q2, kk.astype(BF16), (((1,), (1,)), ((), ())),
                            preferred_element_type=F32)
        return s + bias_ref[first]

    def weighted_values(p, vv):
        o2 = jnp.dot(p.astype(BF16), jnp.concatenate([vv.astype(BF16), ones_cols], axis=1),
                     preferred_element_type=F32)
        o = jnp.where(head0, o2[:ATT_BLOCK, :LANES], o2[ATT_BLOCK:, :LANES])
        den = jnp.where(head0, o2[:ATT_BLOCK, LANES:], o2[ATT_BLOCK:, LANES:])
        return o, den

    def aligned(start):
        return pl.ds(pl.multiple_of(start, ATT_BLOCK), ATT_BLOCK)

    def split(i, carry):
        r, c = i // sub_blocks, i % sub_blocks
        src = pl.ds(r + c * ATT_BLOCK * d_mid, ATT_BLOCK, stride=d_mid)
        dst = aligned(i * ATT_BLOCK)
        q4_ref[dst, :] = q_ref[src, :]
        k4, v4 = k_ref[src, :], v_ref[src, :]
        k4_ref[dst, :] = k4
        v4_ref[dst, :] = v4
        k4b_ref[dst, :] = k4.astype(BF16)
        v4b_ref[dst, :] = v4.astype(BF16)
        kb_ref[dst, :] = k_ref[dst, :].astype(BF16)
        vb_ref[dst, :] = v_ref[dst, :].astype(BF16)
        return carry

    lax.fori_loop(0, nblk, split, 0)

    def run_group(srcs, blocks_per_class, rows_of, merge, unroll=ATT_UNROLL):
        qs, ks, vs = srcs

        def body(i, carry):
            blocks = []
            for u in range(unroll):
                idx = i * unroll + u
                n = idx % blocks_per_class
                rows = rows_of(idx, n)
                prows = rows_of(idx - jnp.minimum(n, 1), jnp.maximum(n - 1, 0))
                first = jnp.where(n == 0, 1, 0)
                kk = jnp.concatenate([ks[prows, :], ks[rows, :]], axis=0)
                blocks.append([rows, prows, scores(qs[rows, :], kk, first)])
            for blk in blocks:
                s = blk[2]
                mx = jnp.max(s, axis=1, keepdims=True)
                blk[2] = jnp.exp2(s - mx)
                blk.append(jnp.where(head0, mx[:ATT_BLOCK], mx[ATT_BLOCK:]))
            for blk in blocks:
                rows, prows, p, _ = blk
                vv = jnp.concatenate([vs[prows, :], vs[rows, :]], axis=0)
                blk[2:3] = weighted_values(p, vv)
            for rows, _, o, den, mxb in blocks:
                merge(rows, o, mxb, den)
            return carry

        lax.fori_loop(0, nblk // unroll, body, 0)

    def init_state(num_ref, m_ref, den_ref):
        def merge(rows, o, mxb, den):
            num_ref[rows, :] = o
            m_ref[rows, :] = mxb
            den_ref[rows, :] = den
        return merge

    def merge4(rows, o, mxb, den):
        m_old = m4_ref[rows, :]
        m_new = jnp.maximum(m_old, mxb)
        a = jnp.exp2(m_old - m_new)
        b = jnp.exp2(mxb - m_new)
        num4_ref[rows, :] = num4_ref[rows, :] * a + o * b
        den4_ref[rows, :] = den4_ref[rows, :] * a + den * b
        m4_ref[rows, :] = m_new

    run_group((q_ref, kb_ref, vb_ref), nblk, lambda idx, n: aligned(idx * ATT_BLOCK),
              init_state(num1_ref, m1_ref, den1_ref))
    blocks16 = nblk // d_out

    def rows16(idx, n):
        cls = idx // blocks16
        r, j = cls // d_in, cls % d_in
        return pl.ds(r * (sub_blocks * ATT_BLOCK) + j + n * ATT_BLOCK * d_in, ATT_BLOCK, stride=d_in)

    run_group((q4_ref, k4_ref, v4_ref), blocks16, rows16, init_state(num4_ref, m4_ref, den4_ref))
    run_group((q4_ref, k4b_ref, v4b_ref), sub_blocks, lambda idx, n: aligned(idx * ATT_BLOCK), merge4)

    def finish(i, carry):
        r, c = i // sub_blocks, i % sub_blocks
        tok = pl.ds(r + c * ATT_BLOCK * d_mid, ATT_BLOCK, stride=d_mid)
        res = aligned(i * ATT_BLOCK)
        m_a, m_b = m4_ref[res, :], m1_ref[tok, :]
        m_new = jnp.maximum(m_a, m_b)
        wa = jnp.exp2(m_a - m_new)
        wb = jnp.exp2(m_b - m_new)
        num = num4_ref[res, :] * wa + num1_ref[tok, :] * wb
        den = den4_ref[res, :] * wa + den1_ref[tok, :] * wb
        o_ref[tok, :] = num / den
        return carry

    lax.fori_loop(0, nblk, finish, 0, unroll=4)


def _band_bias():
    row = np.arange(2 * ATT_BLOCK)[:, None] % ATT_BLOCK
    ki = np.arange(2 * ATT_BLOCK)[None, :] - ATT_BLOCK
    rel = row - ki
    valid = (rel >= 0) & (rel <= SUB_WINDOW)
    b0 = np.where(valid, 0.0, NEG_INF).astype(np.float32)
    b1 = np.where(valid & (ki >= 0), 0.0, NEG_INF).astype(np.float32)
    return jnp.asarray(np.stack([b0, b1]))


def _attn_prompt(q, k, v):
    b, s, _ = q.shape
    assert DILATIONS[0] == 1 and DILATIONS[2] % DILATIONS[1] == 0
    assert s % (ATT_BLOCK * DILATIONS[-1]) == 0 and (s // ATT_BLOCK) % ATT_UNROLL == 0
    blk = pl.BlockSpec((None, s, LANES), lambda i, j: (i, 0, j))
    return pl.pallas_call(
        functools.partial(_attn_prompt_kernel, seq=s),
        grid=(b, ATT_WIDTH // LANES),
        in_specs=[_const_spec((2, 2 * ATT_BLOCK, 2 * ATT_BLOCK)), blk, blk, blk],
        out_specs=blk,
        out_shape=jax.ShapeDtypeStruct((b, s, ATT_WIDTH), F32),
        scratch_shapes=[pltpu.VMEM((s, LANES), F32)] * 9 + [pltpu.VMEM((s, LANES), BF16)] * 4,
        compiler_params=_cparams(("parallel", "parallel")),
        name="attn_prompt",
    )(_band_bias(), q, k, v)


def _attn_sample_body(mw_ref, mn_ref, q_ref, kn_ref, vn_ref, kc_ref, vc_ref,
                      att_ref, nk_ref, nv_ref, *, win, t_new, heads):
    mult_w = mw_ref[...]
    mult_n = mn_ref[...]
    lane = lax.broadcasted_iota(jnp.int32, (HD, LANES), 1)
    keep = lane < LANES - t_new
    top = jnp.zeros((LANES - t_new, heads * HD), F32)
    knt_all = jnp.transpose(jnp.concatenate([top, kn_ref[...]], axis=0))
    vnt_all = jnp.transpose(jnp.concatenate([top, vn_ref[...]], axis=0))
    head_rows = lambda a, h: a[h * HD:(h + 1) * HD, :]
    for h in range(heads):
        for src_ref, new_all, dst in ((kc_ref, knt_all, nk_ref), (vc_ref, vnt_all, nv_ref)):
            rolled = pltpu.roll(src_ref[h], win - t_new, 1)
            dst[h, :, 0:win - LANES] = rolled[:, 0:win - LANES]
            dst[h, :, win - LANES:win] = jnp.where(keep, rolled[:, win - LANES:win], head_rows(new_all, h))
    sc = []
    for h in range(heads):
        qh = q_ref[:, h * HD:(h + 1) * HD].astype(BF16)
        s_w = jnp.dot(qh, kc_ref[h].astype(BF16), preferred_element_type=F32)
        s_n = jnp.dot(qh, head_rows(knt_all, h).astype(BF16), preferred_element_type=F32)
        sc.append((jnp.where(mult_w > 0, s_w, NEG_INF), jnp.where(mult_n > 0, s_n, NEG_INF)))
    pr = []
    for s_w, s_n in sc:
        mx = jnp.maximum(jnp.max(s_w, axis=1, keepdims=True), jnp.max(s_n, axis=1, keepdims=True))
        p_w = mult_w * jnp.exp(s_w - mx)
        p_n = mult_n * jnp.exp(s_n - mx)
        den = jnp.sum(p_w, axis=1, keepdims=True) + jnp.sum(p_n, axis=1, keepdims=True)
        pr.append((p_w, p_n, den))
    for h, (p_w, p_n, den) in enumerate(pr):
        o = lax.dot_general(p_w.astype(BF16), vc_ref[h].astype(BF16), (((1,), (1,)), ((), ())),
                            preferred_element_type=F32)
        o = o + lax.dot_general(p_n.astype(BF16), head_rows(vnt_all, h).astype(BF16),
                                (((1,), (1,)), ((), ())), preferred_element_type=F32)
        att_ref[:, h * HD:(h + 1) * HD] = o / den


def _attn_sample_kernel(*refs, win, t_new):
    _attn_sample_body(*refs, win=win, t_new=t_new, heads=N_ATT_HEADS)


def _attn_mlp_kernel(mw_ref, mn_ref, q_ref, kn_ref, vn_ref, kc_ref, vc_ref,
                     x_ref, attp_ref, hbp_ref, wo_ref, g2_ref, wu_ref, wd_ref, gf_ref,
                     att_ref, nk_ref, nv_ref, y_ref, h2_sc, acc_sc, *, win, t_new, heads):
    c = pl.program_id(0) % FF_PARTS

    @pl.when(c == 0)
    def _():
        mix = (jnp.dot(attp_ref[...].astype(BF16), wo_ref[0:ATT_WIDTH, :], preferred_element_type=F32)
               + jnp.dot(hbp_ref[...].astype(BF16), wo_ref[ATT_WIDTH:D_MODEL, :], preferred_element_type=F32))
        x1 = x_ref[...] + mix
        h2_sc[...] = _rms(x1, g2_ref[...]).astype(BF16)
        acc_sc[...] = x1

    u = jnp.dot(h2_sc[...], wu_ref[c], preferred_element_type=F32)
    r = jnp.maximum(u, 0.0)
    acc_sc[...] += jnp.dot((r * r).astype(BF16), wd_ref[c], preferred_element_type=F32)
    _attn_sample_body(mw_ref, mn_ref, q_ref, kn_ref, vn_ref, kc_ref, vc_ref, att_ref, nk_ref, nv_ref,
                      win=win, t_new=t_new, heads=heads)

    @pl.when(c == FF_PARTS - 1)
    def _():
        y_ref[...] = _rms(acc_sc[...], gf_ref[...])


def _sample_multiplicity(win, t_new):
    t = np.arange(t_new)[:, None]
    idx = np.arange(win + t_new)[None, :]
    back = win + t - idx
    mult = np.zeros((t_new, win + t_new), np.float32)
    for dil in DILATIONS:
        mult += ((back >= 0) & (back % dil == 0) & (back // dil <= SUB_WINDOW)).astype(np.float32)
    mw = mult[:, :win]
    mn = np.zeros((t_new, LANES), np.float32)
    mn[:, LANES - t_new:] = mult[:, win:]
    return jnp.asarray(mw), jnp.asarray(mn)


def _attn_sample(q, k_new, v_new, kc_t, vc_t, t_new):
    b, nh, _, win = kc_t.shape
    assert win == DILATIONS[-1] * SUB_WINDOW and t_new == SUBLANES and q.shape[0] == b * t_new
    mw, mn = _sample_multiplicity(win, t_new)
    rows = pl.BlockSpec((t_new, ATT_WIDTH), lambda i: (i, 0))
    per_b = lambda *shape: pl.BlockSpec((None,) + shape, lambda i: (i,) + (0,) * len(shape))
    return pl.pallas_call(
        functools.partial(_attn_sample_kernel, win=win, t_new=t_new),
        grid=(b,),
        in_specs=[
            _const_spec((t_new, win)), _const_spec((t_new, LANES)),
            rows, rows, rows, per_b(nh, HD, win), per_b(nh, HD, win),
        ],
        out_specs=[rows, per_b(nh, HD, win), per_b(nh, HD, win)],
        out_shape=[
            jax.ShapeDtypeStruct((b * t_new, ATT_WIDTH), F32),
            jax.ShapeDtypeStruct((b, nh, HD, win), F32),
            jax.ShapeDtypeStruct((b, nh, HD, win), F32),
        ],
        compiler_params=_cparams(("parallel",)),
        name="attn_sample",
    )(mw, mn, q, k_new, v_new, kc_t, vc_t)


def _attn_mlp_fits(b_s, n_p):
    steps = b_s * ATT_HEAD_SPLIT
    if steps % FF_PARTS:
        return False
    tiles = steps // FF_PARTS
    return n_p % tiles == 0 and (n_p // tiles) % SUBLANES == 0 and n_p // tiles <= ROW_TILE


def _attn_mlp(q, k_new, v_new, kc_t, vc_t, t_new, x2d, attp, hbp, wo_bf, g2, wu_parts, wd_parts, gf):
    b, nh, _, win = kc_t.shape
    n_p = x2d.shape[0]
    assert win == DILATIONS[-1] * SUB_WINDOW and t_new == SUBLANES and q.shape[0] == b * t_new
    heads = nh // ATT_HEAD_SPLIT
    steps = b * ATT_HEAD_SPLIT
    tm = n_p // (steps // FF_PARTS)
    ffc = D_FF // FF_PARTS
    mw, mn = _sample_multiplicity(win, t_new)
    rows = pl.BlockSpec((t_new, heads * HD), lambda i: (i // ATT_HEAD_SPLIT, i % ATT_HEAD_SPLIT))
    win_blk = pl.BlockSpec((None, heads, HD, win), lambda i: (i // ATT_HEAD_SPLIT, i % ATT_HEAD_SPLIT, 0, 0))
    tile = lambda w: pl.BlockSpec((tm, w), lambda i: (i // FF_PARTS, 0))
    return pl.pallas_call(
        functools.partial(_attn_mlp_kernel, win=win, t_new=t_new, heads=heads),
        grid=(steps,),
        in_specs=[
            _const_spec((t_new, win)), _const_spec((t_new, LANES)),
            rows, rows, rows, win_blk, win_blk,
            tile(D_MODEL), tile(ATT_WIDTH), tile(MLSTM_WIDTH),
            _const_spec((D_MODEL, D_MODEL)), _const_spec((1, D_MODEL)),
            _const_spec((FF_PARTS, D_MODEL, ffc)), _const_spec((FF_PARTS, ffc, D_MODEL)),
            _const_spec((1, D_MODEL)),
        ],
        out_specs=[rows, win_blk, win_blk, tile(D_MODEL)],
        out_shape=[
            jax.ShapeDtypeStruct((b * t_new, ATT_WIDTH), F32),
            jax.ShapeDtypeStruct((b, nh, HD, win), F32),
            jax.ShapeDtypeStruct((b, nh, HD, win), F32),
            jax.ShapeDtypeStruct((n_p, D_MODEL), F32),
        ],
        scratch_shapes=[pltpu.VMEM((tm, D_MODEL), BF16), pltpu.VMEM((tm, D_MODEL), F32)],
        compiler_params=_cparams(("arbitrary",)),
        name="attn_mlp",
    )(mw, mn, q, k_new, v_new, kc_t, vc_t, x2d, attp, hbp, wo_bf, g2, wu_parts, wd_parts, gf)


def _split_dot(tri, x):
    hi = x.astype(BF16)
    lo = (x - hi.astype(F32)).astype(BF16)
    return (jnp.dot(tri, hi, preferred_element_type=F32)
            + jnp.dot(tri, lo, preferred_element_type=F32))


def _split_dot_rows(x, tri):
    hi = x.astype(BF16)
    lo = (x - hi.astype(F32)).astype(BF16)
    return (jnp.dot(hi, tri, preferred_element_type=F32)
            + jnp.dot(lo, tri, preferred_element_type=F32))


def _log_sigmoid(x):
    return jnp.minimum(x, 0.0) - jnp.log1p(jnp.exp(-jnp.abs(x)))


def _mlstm_init(xp_ref, c_sc, n_sc, m_sc, nm_sc, group, state_refs=None):
    if state_refs is not None:
        tail_ref, c0_ref, n0_ref, m0_ref = state_refs
        for c in range(2 * MLSTM_HEADS):
            xp_ref[:, c, 0:SUBLANES, :] = tail_ref[:, :, c * LANES:(c + 1) * LANES]
        c_sc[...] = c0_ref[...]
        n_sc[...] = n0_ref[...]
        m_sc[...] = m0_ref[...]
    else:
        xp_ref[:, :, 0:SUBLANES, :] = jnp.zeros((group, 2 * MLSTM_HEADS, SUBLANES, LANES), F32)
        c_sc[...] = jnp.zeros_like(c_sc)
        n_sc[...] = jnp.zeros_like(n_sc)
        m_sc[...] = jnp.zeros_like(m_sc)
        nm_sc[...] = jnp.zeros_like(nm_sc)


def _mlstm_emit(c_out_ref, n_out_ref, m_out_ref, c_sc, n_sc, m_sc, nm_sc, group, mxu_sums):
    if mxu_sums:
        for g in range(group):
            for h in range(MLSTM_HEADS):
                n_sc[g, h:h + 1, :] = jnp.transpose(nm_sc[g, h])[0:1, :]
    c_out_ref[...] = c_sc[...]
    n_out_ref[...] = n_sc[...]
    m_out_ref[...] = m_sc[...]


def _mlstm_chunk(raws, vbs, obs, gate_list, hbs, tails, wc_ref, bc_ref, g_ref,
                 xp_ref, act_ref, c_sc, n_sc, m_sc, nm_sc, *, chunk, mxu_sums):
    L = chunk
    group = len(raws)
    r_i = lax.broadcasted_iota(jnp.int32, (L, L), 0)
    c_i = lax.broadcasted_iota(jnp.int32, (L, L), 1)
    causal = r_i >= c_i
    tri_l = jnp.where(causal, 1.0, 0.0).astype(BF16)
    tri_u = jnp.where(r_i <= c_i, 1.0, 0.0).astype(BF16)
    first_tap = SUBLANES - (CONV_W - 1)
    bc = bc_ref[...]
    taps = [wc_ref[w:w + 1, :] for w in range(CONV_W)]
    ones_sq = jnp.ones((LANES, LANES), BF16)

    per_batch = []
    for g in range(group):
        tails[g][...] = raws[g][L - SUBLANES:L, :]
        for c in range(2 * MLSTM_HEADS):
            cl = slice(c * LANES, (c + 1) * LANES)
            raw = raws[g][:, cl]
            xp_ref[g, c, SUBLANES:SUBLANES + L, :] = raw
            if L % (SUBLANES * SUBLANES) == 0:
                n_rows = L // SUBLANES
                slabs = [xp_ref[g, c, pl.ds(first_tap + j, n_rows, stride=SUBLANES), :]
                         for j in range(SUBLANES + CONV_W - 1)]
                for s in range(SUBLANES):
                    conv = bc[:, cl]
                    for w in range(CONV_W):
                        conv = conv + slabs[s + w] * taps[w][:, cl]
                    act_ref[g, c, pl.ds(s, n_rows, stride=SUBLANES), :] = conv * jax.nn.sigmoid(conv)
            else:
                conv = bc[:, cl]
                for w in range(CONV_W):
                    conv = conv + xp_ref[g, c, first_tap + w:first_tap + w + L, :] * taps[w][:, cl]
                act_ref[g, c] = conv * jax.nn.sigmoid(conv)
            xp_ref[g, c, 0:SUBLANES, :] = raw[L - SUBLANES:L, :]

        gates = gate_list[g][...]
        lf = _log_sigmoid(gates)
        b_col = _split_dot(tri_l, lf)
        if L == LANES:
            gates_sq = gates
        else:
            gates_sq = jnp.concatenate([gates, jnp.zeros((LANES - L, LANES), F32)], axis=0)
        gates_t = jnp.transpose(gates_sq)[0:SUBLANES, 0:L]
        lf_t = _log_sigmoid(gates_t)
        b_row = _split_dot_rows(lf_t, tri_u)

        per_batch.append((gates, b_col, gates_t, b_row))

    items = [(g, h) for g in range(group) for h in range(MLSTM_HEADS)]
    st = {}
    for g, h in items:
        gates, b_col, gates_t, b_row = per_batch[g]
        ig_c = gates[:, h:h + 1]
        b_c = b_col[:, MLSTM_HEADS + h:MLSTM_HEADS + h + 1]
        ig_r = gates_t[h:h + 1, :]
        b_r = b_row[MLSTM_HEADS + h:MLSTM_HEADS + h + 1, :]
        if mxu_sums:
            b_c_wide = jnp.broadcast_to(b_c, (L, LANES))
            a_c = b_c_wide + m_sc[g, h:h + 1, 0:1]
            dm = jnp.where(causal, b_c_wide - b_r + ig_r, NEG_INF)
            mt = jnp.maximum(a_c, jnp.broadcast_to(jnp.max(dm, axis=1, keepdims=True), (L, LANES)))
        else:
            a_c = b_c + m_sc[g, h:h + 1, 0:1]
            dm = jnp.where(causal, b_c - b_r + ig_r, NEG_INF)
            mt = jnp.maximum(a_c, jnp.max(dm, axis=1, keepdims=True))
        st[g, h] = dict(ig_c=ig_c, b_c=b_c, a_c=a_c, dm=dm, mt=mt)
    for g, h in items:
        d = st[g, h]
        q = act_ref[g, h]
        k = act_ref[g, MLSTM_HEADS + h] * (MLSTM_HD ** -0.5)
        v = vbs[g][:, h * MLSTM_HD:(h + 1) * MLSTM_HD]
        qb, kb = q.astype(BF16), k.astype(BF16)
        w_inter = jnp.exp(d["a_c"] - d["mt"])
        wm = jnp.exp(d["dm"] - d["mt"])
        sc = lax.dot_general(qb, kb, (((1,), (1,)), ((), ())), preferred_element_type=F32) * wm
        d.update(q=q, k=k, v=v, qb=qb, vb=v.astype(BF16), w_inter=w_inter, sc=sc)
        del d["dm"]
    for g, h in items:
        d = st[g, h]
        c_old = c_sc[g, h]
        if mxu_sums:
            nm_old = nm_sc[g, h]
            v_ones = jnp.concatenate([d["vb"], ones_sq], axis=1)
            state = jnp.concatenate([c_old, nm_old], axis=1).astype(BF16)
            inter = jnp.dot(d["qb"], state, preferred_element_type=F32)
            intra = jnp.dot(d["sc"].astype(BF16), v_ones, preferred_element_type=F32)
            num = d["w_inter"] * inter[:, :LANES] + intra[:, :LANES]
            den = d["w_inter"] * inter[:, LANES:] + intra[:, LANES:]
            d.update(v_ones=v_ones, nm_old=nm_old)
        else:
            n_old = n_sc[g, h:h + 1, :]
            num = (d["w_inter"] * jnp.dot(d["qb"], c_old.astype(BF16), preferred_element_type=F32)
                   + jnp.dot(d["sc"].astype(BF16), d["vb"], preferred_element_type=F32))
            den = (d["w_inter"] * jnp.sum(d["q"] * n_old, axis=1, keepdims=True)
                   + jnp.sum(d["sc"], axis=1, keepdims=True))
            d.update(n_old=n_old)
        d.update(c_old=c_old, hh=num / jnp.maximum(jnp.abs(den), jnp.exp(-d["mt"])))
        del d["sc"]
    for g, h in items:
        d = st[g, h]
        mt, a_c, b_c = d["mt"], d["a_c"], d["b_c"]
        m_last = mt[L - 1:L, 0:1]
        w_last = jnp.exp(a_c[L - 1:L, 0:1] - m_last)
        w_t = jnp.exp(b_c[L - 1:L, :] - b_c + d["ig_c"] - m_last)
        kw = d["k"] * w_t
        if mxu_sums:
            upd = jnp.dot(jnp.transpose(kw).astype(BF16), d["v_ones"], preferred_element_type=F32)
            c_sc[g, h] = w_last * d["c_old"] + upd[:, :LANES]
            nm_sc[g, h] = w_last * d["nm_old"] + upd[:, LANES:]
        else:
            if L == LANES:
                kw_sq, v_sq = kw, d["v"]
            else:
                pad = jnp.zeros((LANES - L, MLSTM_HD), F32)
                kw_sq = jnp.concatenate([kw, pad], axis=0)
                v_sq = jnp.concatenate([d["v"], pad], axis=0)
            c_sc[g, h] = w_last * d["c_old"] + jnp.dot(jnp.transpose(kw_sq).astype(BF16), v_sq.astype(BF16),
                                                       preferred_element_type=F32)
            n_sc[g, h:h + 1, :] = w_last * d["n_old"] + jnp.sum(kw, axis=0, keepdims=True)
        m_sc[g, h:h + 1, :] = jnp.broadcast_to(m_last, (1, LANES))
    for g, h in items:
        hh = st[g, h]["hh"]
        sl = slice(h * MLSTM_HD, (h + 1) * MLSTM_HD)
        if mxu_sums:
            mean_sq = _split_dot_rows(hh * hh, ones_sq) * (1.0 / MLSTM_HD)
        else:
            mean_sq = jnp.mean(hh * hh, axis=1, keepdims=True)
        hn = hh * lax.rsqrt(mean_sq + EPS)
        hbs[g][:, sl] = hn * g_ref[:, sl] * jax.nn.sigmoid(obs[g][:, sl])


def _mlstm_kernel(*refs, chunk, group, has_state):
    if has_state:
        (raw_ref, vb_ref, ob_ref, gate_ref, wc_ref, bc_ref, g_ref, tail_ref, c0_ref, n0_ref, m0_ref,
         hb_ref, tail_out_ref, c_out_ref, n_out_ref, m_out_ref, xp_ref, act_ref, c_sc, n_sc, m_sc) = refs
        nm_sc, state_refs = None, (tail_ref, c0_ref, n0_ref, m0_ref)
    else:
        (raw_ref, vb_ref, ob_ref, gate_ref, wc_ref, bc_ref, g_ref,
         hb_ref, tail_out_ref, c_out_ref, n_out_ref, m_out_ref, xp_ref, act_ref, c_sc, n_sc, m_sc, nm_sc) = refs
        state_refs = None
    c_idx = pl.program_id(1)
    mxu_sums = nm_sc is not None and chunk == LANES

    @pl.when(c_idx == 0)
    def _():
        _mlstm_init(xp_ref, c_sc, n_sc, m_sc, nm_sc, group, state_refs)

    per_g = lambda ref: [ref.at[g] for g in range(group)]
    _mlstm_chunk(per_g(raw_ref), per_g(vb_ref), per_g(ob_ref), per_g(gate_ref), per_g(hb_ref), per_g(tail_out_ref),
                 wc_ref, bc_ref, g_ref, xp_ref, act_ref, c_sc, n_sc, m_sc, nm_sc, chunk=chunk, mxu_sums=mxu_sums)

    @pl.when(c_idx == pl.num_programs(1) - 1)
    def _():
        _mlstm_emit(c_out_ref, n_out_ref, m_out_ref, c_sc, n_sc, m_sc, nm_sc, group, mxu_sums)


def _mlstm(raw, vb, ob, gates, w_conv, b_conv, mh_g, chunk, group, state=None):
    b, t, _ = raw.shape
    nc = t // chunk
    assert nc * chunk == t and chunk % SUBLANES == 0 and b % group == 0
    seq = lambda w: pl.BlockSpec((group, chunk, w), lambda i, j: (i, j, 0))
    per_b = lambda *shape: pl.BlockSpec((group,) + shape, lambda i, j: (i,) + (0,) * len(shape))
    in_specs = [seq(2 * MLSTM_WIDTH), seq(MLSTM_WIDTH), seq(MLSTM_WIDTH), seq(GATE_PAD),
                _const_spec((CONV_W, 2 * MLSTM_WIDTH)), _const_spec((1, 2 * MLSTM_WIDTH)),
                _const_spec((1, MLSTM_WIDTH))]
    args = [raw, vb, ob, gates, w_conv, b_conv, mh_g]
    state_shapes = [(SUBLANES, 2 * MLSTM_WIDTH), (MLSTM_HEADS, MLSTM_HD, MLSTM_HD),
                    (MLSTM_HEADS, MLSTM_HD), (MLSTM_HEADS, LANES)]
    state_specs = [per_b(*sh) for sh in state_shapes]
    if state is not None:
        in_specs += state_specs
        args += list(state)
    return pl.pallas_call(
        functools.partial(_mlstm_kernel, chunk=chunk, group=group, has_state=state is not None),
        grid=(b // group, nc),
        in_specs=in_specs,
        out_specs=[seq(MLSTM_WIDTH)] + state_specs,
        out_shape=[jax.ShapeDtypeStruct((b, t, MLSTM_WIDTH), F32)]
        + [jax.ShapeDtypeStruct((b,) + sh, F32) for sh in state_shapes],
        scratch_shapes=[
            pltpu.VMEM((group, 2 * MLSTM_HEADS, SUBLANES + chunk, LANES), F32),
            pltpu.VMEM((group, 2 * MLSTM_HEADS, chunk, LANES), F32),
        ] + [pltpu.VMEM((group,) + sh, F32) for sh in state_shapes[1:]]
        + ([] if state is not None else [pltpu.VMEM((group,) + state_shapes[1], F32)]),
        compiler_params=_cparams(("parallel", "arbitrary")),
        name="mlstm_state" if state is not None else "mlstm",
    )(*args)


def _inproj_mlstm_kernel(x_ref, g_ref, w_ref, wg_ref, bg_ref, cos_ref, sa_ref, sb_ref, wc_ref, bc_ref, mhg_ref,
                         q_ref, k_ref, v_ref, kt_ref, vt_ref, hb_ref, tail_ref, c_out_ref, n_out_ref, m_out_ref,
                         *scratch, tiles_per_seq, window_tiles):
    stage_a, stage_b = scratch[0:4], scratch[4:8]
    xp_ref, act_ref, c_sc, n_sc, m_sc, nm_sc = scratch[8:]
    i = pl.program_id(0)
    n_seq, tm = hb_ref.shape[0], hb_ref.shape[1]

    @pl.when(i == 0)
    def _():
        for st in stage_b:
            st[...] = jnp.zeros(st.shape, F32)

    prev_tile_in_seq = jnp.maximum(i - 1, 0) % tiles_per_seq

    @pl.when(prev_tile_in_seq == 0)
    def _():
        _mlstm_init(xp_ref, c_sc, n_sc, m_sc, nm_sc, n_seq)

    def step(write, read):
        _inproj_kernel(x_ref, g_ref, w_ref, wg_ref, bg_ref, cos_ref, sa_ref, sb_ref, q_ref, k_ref, v_ref,
                       *write, kt_ref, vt_ref, tiles_per_seq=tiles_per_seq, window_tiles=window_tiles)
        raw, vb, ob, gate = read
        for cc in range(tm // CHUNK):
            rows = pl.ds(cc * CHUNK, CHUNK)
            each = lambda ref: [ref.at[s_, rows, :] for s_ in range(n_seq)]
            _mlstm_chunk(each(raw), each(vb), each(ob), each(gate), each(hb_ref),
                         [tail_ref.at[s_] for s_ in range(n_seq)], wc_ref, bc_ref, mhg_ref,
                         xp_ref, act_ref, c_sc, n_sc, m_sc, nm_sc, chunk=CHUNK, mxu_sums=True)

    @pl.when(i % 2 == 0)
    def _():
        step(stage_a, stage_b)

    @pl.when(i % 2 == 1)
    def _():
        step(stage_b, stage_a)

    @pl.when(prev_tile_in_seq == tiles_per_seq - 1)
    def _():
        _mlstm_emit(c_out_ref, n_out_ref, m_out_ref, c_sc, n_sc, m_sc, nm_sc, n_seq, True)


def _inproj_mlstm(x3d, g1, w_main_bf, w_gate_bf, bg_pad, tables, w_conv, b_conv, mh_g, tm, n_seq, window):
    b, t, _ = x3d.shape
    tiles_per_seq, window_tiles = t // tm, window // tm
    n_steps = (b // n_seq) * tiles_per_seq
    assert b % n_seq == 0 and t % tm == 0 and window_tiles >= 1 and tm % CHUNK == 0 and CHUNK == LANES
    cur = lambda i: jnp.minimum(i, n_steps - 1)
    prv = lambda i: jnp.maximum(i - 1, 0)
    row = lambda w: pl.BlockSpec((n_seq, tm, w), lambda i: (cur(i) // tiles_per_seq, cur(i) % tiles_per_seq, 0))
    tab = pl.BlockSpec((tm, LANES), lambda i: (cur(i) % tiles_per_seq, 0))
    first = tiles_per_seq - window_tiles
    win = pl.BlockSpec((n_seq, ATT_WIDTH, tm),
                       lambda i: (cur(i) // tiles_per_seq, 0, jnp.maximum(cur(i) % tiles_per_seq - first, 0)))
    seq_state = lambda *shape: pl.BlockSpec((n_seq,) + shape, lambda i: (prv(i) // tiles_per_seq,) + (0,) * len(shape))
    state_shapes = [(MLSTM_HEADS, MLSTM_HD, MLSTM_HD), (MLSTM_HEADS, MLSTM_HD), (MLSTM_HEADS, LANES)]
    return pl.pallas_call(
        functools.partial(_inproj_mlstm_kernel, tiles_per_seq=tiles_per_seq, window_tiles=window_tiles),
        grid=(n_steps + 1,),
        in_specs=[
            row(D_MODEL), _const_spec((1, D_MODEL)),
            _const_spec((D_MODEL, IN_COLS - 2 * MLSTM_HEADS)), _const_spec((D_MODEL, GATE_PAD)),
            _const_spec((1, GATE_PAD)), tab, tab, tab,
            _const_spec((CONV_W, 2 * MLSTM_WIDTH)), _const_spec((1, 2 * MLSTM_WIDTH)), _const_spec((1, MLSTM_WIDTH)),
        ],
        out_specs=[row(ATT_WIDTH), row(ATT_WIDTH), row(ATT_WIDTH), win, win,
                   pl.BlockSpec((n_seq, tm, MLSTM_WIDTH),
                                lambda i: (prv(i) // tiles_per_seq, prv(i) % tiles_per_seq, 0)),
                   seq_state(SUBLANES, 2 * MLSTM_WIDTH)]
        + [seq_state(*sh) for sh in state_shapes],
        out_shape=[jax.ShapeDtypeStruct((b, t, ATT_WIDTH), F32)] * 3
        + [jax.ShapeDtypeStruct((b, ATT_WIDTH, window), F32)] * 2
        + [jax.ShapeDtypeStruct((b, t, MLSTM_WIDTH), F32), jax.ShapeDtypeStruct((b, SUBLANES, 2 * MLSTM_WIDTH), F32)]
        + [jax.ShapeDtypeStruct((b,) + sh, F32) for sh in state_shapes],
        scratch_shapes=[pltpu.VMEM((n_seq, tm, w), F32)
                        for w in (2 * MLSTM_WIDTH, MLSTM_WIDTH, MLSTM_WIDTH, GATE_PAD)] * 2
        + [pltpu.VMEM((n_seq, 2 * MLSTM_HEADS, SUBLANES + CHUNK, LANES), F32),
           pltpu.VMEM((n_seq, 2 * MLSTM_HEADS, CHUNK, LANES), F32)]
        + [pltpu.VMEM((n_seq,) + sh, F32) for sh in state_shapes]
        + [pltpu.VMEM((n_seq,) + state_shapes[0], F32)],
        compiler_params=_cparams(("arbitrary",)),
        name="inproj_mlstm",
    )(x3d, g1, w_main_bf, w_gate_bf, bg_pad, *tables, w_conv, b_conv, mh_g)


def _outmlp_kernel(x_ref, att_ref, hb_ref, wo_ref, g2_ref, wu_ref, wd_ref, gf_ref, y_ref):
    mix = (jnp.dot(att_ref[...].astype(BF16), wo_ref[0:ATT_WIDTH, :], preferred_element_type=F32)
           + jnp.dot(hb_ref[...].astype(BF16), wo_ref[ATT_WIDTH:D_MODEL, :], preferred_element_type=F32))
    x1 = x_ref[...] + mix
    h2 = _rms(x1, g2_ref[...]).astype(BF16)
    acc = x1
    for c in range(FF_PARTS):
        u = jnp.dot(h2, wu_ref[c], preferred_element_type=F32)
        r = jnp.maximum(u, 0.0)
        acc = acc + jnp.dot((r * r).astype(BF16), wd_ref[c], preferred_element_type=F32)
    y_ref[...] = _rms(acc, gf_ref[...])


def _outmlp(x2d, att2d, hb2d, wo_bf, g2, wu_parts, wd_parts, gf, tm):
    n = x2d.shape[0]
    ffc = D_FF // FF_PARTS
    row = lambda w: pl.BlockSpec((tm, w), lambda i: (i, 0))
    return pl.pallas_call(
        _outmlp_kernel,
        grid=(n // tm,),
        in_specs=[row(D_MODEL), row(ATT_WIDTH), row(MLSTM_WIDTH),
                  _const_spec((D_MODEL, D_MODEL)), _const_spec((1, D_MODEL)),
                  _const_spec((FF_PARTS, D_MODEL, ffc)), _const_spec((FF_PARTS, ffc, D_MODEL)),
                  _const_spec((1, D_MODEL))],
        out_specs=row(D_MODEL),
        out_shape=jax.ShapeDtypeStruct((n, D_MODEL), F32),
        compiler_params=_cparams(("parallel",)),
        name="outmlp",
    )(x2d, att2d, hb2d, wo_bf, g2, wu_parts, wd_parts, gf)


def _project(x, pos, mix_params, tm, tile_tables, window=0):
    g1, w_main_bf, w_gate_bf, bg_pad = mix_params
    b, t, _ = x.shape
    x2d = x.reshape(b * t, D_MODEL)
    tm = min(tm, b * t)
    tables = _rotary_tables(pos)
    if tile_tables:
        tables = tuple(np.tile(a, (tm // t, 1)) for a in tables)
        table_blocks, tiles_per_seq = 1, 1
    else:
        table_blocks = tiles_per_seq = t // tm
    assert window % tm == 0
    outs = _inproj(x2d, g1, w_main_bf, w_gate_bf, bg_pad, tuple(jnp.asarray(a) for a in tables), tm,
                   table_blocks, tiles_per_seq, window // tm)
    return x2d, tm, outs


def kernel(x_prompt, x_sample, cache_win_k, cache_win_v, state_conv, state_C, state_n, state_m, norm1_g, w_in, b_gate, w_conv, b_conv, mh_norm_g, w_out, norm2_g, w_up, w_down, norm_f_g):
    depth = w_in.shape[0]
    assert depth == 1, "the final norm is fused into the (single) layer's MLP kernel"
    l = 0
    n_gate = 2 * MLSTM_HEADS
    w_l = w_in[l]
    w_main_bf = w_l[:, :IN_COLS - n_gate].astype(BF16)
    w_gate_bf = jnp.pad(w_l[:, IN_COLS - n_gate:], ((0, 0), (0, GATE_PAD - n_gate))).astype(BF16)
    bg_pad = jnp.pad(b_gate[l], (0, GATE_PAD - n_gate))[None, :]
    mix_params = (norm1_g[l][None, :], w_main_bf, w_gate_bf, bg_pad)
    conv_params = (w_conv[l], b_conv[l][None, :], mh_norm_g[l][None, :])
    ffc = D_FF // FF_PARTS
    wu_parts = w_up[l].reshape(D_MODEL, FF_PARTS, ffc).transpose(1, 0, 2).astype(BF16)
    wd_parts = w_down[l].astype(BF16).reshape(FF_PARTS, ffc, D_MODEL)
    mlp_params = (w_out[l].astype(BF16), norm2_g[l][None, :], wu_parts, wd_parts, norm_f_g[None, :])

    bp, tp, _ = x_prompt.shape
    n_keep = min(DILATIONS[-1] * SUB_WINDOW, tp)
    r3 = lambda a: a.reshape(bp, tp, a.shape[-1])
    n_seq = MLSTM_PROMPT_GROUP if bp % MLSTM_PROMPT_GROUP == 0 else 1
    tm_f = ROW_TILE // n_seq
    if tp % tm_f == 0 and n_keep % tm_f == 0 and tm_f % CHUNK == 0:
        xp2d, tm_p = x_prompt.reshape(bp * tp, D_MODEL), ROW_TILE
        tables = tuple(jnp.asarray(a) for a in _rotary_tables(np.arange(tp)))
        q, k, v, kt, vt, hb_p, tail_p, pc, pn, pm = _inproj_mlstm(
            x_prompt, *mix_params, tables, *conv_params, tm_f, n_seq, n_keep)
        hb_p = hb_p.reshape(bp * tp, MLSTM_WIDTH)
    else:
        xp2d, tm_p, (q, k, v, raw, vb, ob, gates, kt, vt) = _project(x_prompt, np.arange(tp), mix_params, ROW_TILE,
                                                                   False, window=n_keep)
        hb_p, tail_p, pc, pn, pm = _mlstm(r3(raw), r3(vb), r3(ob), r3(gates), *conv_params,
                                          CHUNK if tp % CHUNK == 0 else tp,
                                          MLSTM_PROMPT_GROUP if bp % MLSTM_PROMPT_GROUP == 0 else 1)
        hb_p = hb_p.reshape(bp * tp, MLSTM_WIDTH)
    att_p = _attn_prompt(r3(q), r3(k), r3(v)).reshape(bp * tp, ATT_WIDTH)
    heads = lambda a: a.reshape(bp, N_ATT_HEADS, HD, n_keep).transpose(0, 3, 1, 2)
    pk, pv = heads(kt), heads(vt)

    bs, ts, _ = x_sample.shape
    xs2d, tm_s, (q, k, v, raw, vb, ob, gates) = _project(x_sample, PAST_LEN + np.arange(ts), mix_params, ROW_TILE, True)
    r3 = lambda a: a.reshape(bs, ts, a.shape[-1])
    tail0 = jnp.pad(state_conv[l], ((0, 0), (SUBLANES - (CONV_W - 1), 0), (0, 0)))
    m0b = jnp.broadcast_to(state_m[l][:, :, None], (bs, MLSTM_HEADS, LANES))
    hb_s, tail_s, sc, sn, sm = _mlstm(r3(raw), r3(vb), r3(ob), r3(gates), *conv_params, ts,
                                      MLSTM_SAMPLE_GROUP if bs % MLSTM_SAMPLE_GROUP == 0 else 1,
                                      state=(tail0, state_C[l], state_n[l], m0b))
    hb_s = hb_s.reshape(bs * ts, MLSTM_WIDTH)

    kc_t = cache_win_k[l].transpose(0, 2, 3, 1)
    vc_t = cache_win_v[l].transpose(0, 2, 3, 1)
    if _attn_mlp_fits(bs, bp * tp):
        att_s, nk_t, nv_t, y_p = _attn_mlp(q, k, v, kc_t, vc_t, ts, xp2d, att_p, hb_p, *mlp_params)
    else:
        att_s, nk_t, nv_t = _attn_sample(q, k, v, kc_t, vc_t, ts)
        y_p = _outmlp(xp2d, att_p, hb_p, *mlp_params, tm_p)
    sk, sv = nk_t.transpose(0, 3, 1, 2), nv_t.transpose(0, 3, 1, 2)
    y_s = _outmlp(xs2d, att_s, hb_s, *mlp_params, tm_s)

    first_tail = SUBLANES - (CONV_W - 1)
    outs = (y_p.reshape(bp, tp, D_MODEL), y_s.reshape(bs, ts, D_MODEL),
            pk, pv, tail_p[:, first_tail:, :], pc, pn, pm[:, :, 0],
            sk, sv, tail_s[:, first_tail:, :], sc, sn, sm[:, :, 0])
    return outs[:2] + tuple(o[None] for o in outs[2:])
```

```python
import functools

import jax
import jax.numpy as jnp
import numpy as np
from jax import lax
from jax.experimental import pallas as pl
from jax.experimental.pallas import tpu as pltpu

D_MODEL = 1024
HD = 64
N_ATT_HEADS = 8
ATT_WIDTH = N_ATT_HEADS * HD
MLSTM_HEADS = 4
MLSTM_WIDTH = D_MODEL - ATT_WIDTH
MLSTM_HD = MLSTM_WIDTH // MLSTM_HEADS
ROT_DIM = HD // 4
ROT_HALF = ROT_DIM // 2
ROPE_THETA = 500000.0
DILATIONS = (1, 4, 16)
SUB_WINDOW = 128
ATT_BLOCK = 128
ROW_TILE = 512
ATT_UNROLL = 32
MLSTM_SAMPLE_GROUP = 8
MLSTM_PROMPT_GROUP = 2
FF_PARTS = 4
ATT_HEAD_SPLIT = 2
CONV_W = 4
CHUNK = 128
D_FF = 4 * D_MODEL
EPS = 1e-6
PAST_LEN = 8192
IN_SIZES = (ATT_WIDTH, ATT_WIDTH, ATT_WIDTH, 2 * MLSTM_WIDTH, MLSTM_WIDTH, MLSTM_WIDTH, 2 * MLSTM_HEADS)
IN_COLS = sum(IN_SIZES)

LANES = 128
SUBLANES = 8
GATE_PAD = LANES
VMEM_LIMIT = 56 * 1024 * 1024

F32 = jnp.float32
BF16 = jnp.bfloat16
NEG_INF = float("-inf")
LOG2E = 1.4426950408889634


def _cparams(sem):
    return pltpu.CompilerParams(dimension_semantics=sem, vmem_limit_bytes=VMEM_LIMIT)


def _const_spec(shape):
    nd = len(shape)
    return pl.BlockSpec(shape, lambda *_: (0,) * nd, pipeline_mode=pl.Buffered(1))


def _rms(x, g):
    return x * lax.rsqrt(jnp.mean(x * x, axis=-1, keepdims=True) + EPS) * g


def _inproj_kernel(x_ref, g_ref, w_ref, wg_ref, bg_ref, cos_ref, sa_ref, sb_ref,
                   q_ref, k_ref, v_ref, raw_ref, vb_ref, ob_ref, gate_ref, *win_refs,
                   tiles_per_seq, window_tiles):
    stacked = len(x_ref.shape) == 3
    n_seq = x_ref.shape[0] if stacked else 1
    tm = x_ref.shape[-2]
    flat = lambda a: a.reshape(n_seq * tm, a.shape[-1]) if stacked else a
    unflat = lambda a: a.reshape(n_seq, tm, a.shape[-1]) if stacked else a
    per_seq = lambda t: jnp.concatenate([t] * n_seq, axis=0) if stacked else t
    h = _rms(flat(x_ref[...]), g_ref[...]).astype(BF16)

    def proj(lo, width):
        return jnp.dot(h, w_ref[:, lo:lo + width], preferred_element_type=F32)

    cos, sa, sb = (per_seq(r[...]) for r in (cos_ref, sa_ref, sb_ref))

    def rotary_store(dst, y, scale):
        for c in range(ATT_WIDTH // LANES):
            yc = y[:, c * LANES:(c + 1) * LANES]
            up = pltpu.roll(yc, LANES - ROT_HALF, 1)
            dn = pltpu.roll(yc, ROT_HALF, 1)
            r = yc * cos + up * sa + dn * sb
            dst[..., c * LANES:(c + 1) * LANES] = unflat(r * scale if scale != 1.0 else r)

    off = 0
    rotary_store(q_ref, proj(off, ATT_WIDTH), HD ** -0.5)
    off += ATT_WIDTH
    rotary_store(k_ref, proj(off, ATT_WIDTH), 1.0)
    off += ATT_WIDTH
    v_ref[...] = unflat(proj(off, ATT_WIDTH))
    off += ATT_WIDTH
    raw_ref[...] = unflat(proj(off, 2 * MLSTM_WIDTH))
    off += 2 * MLSTM_WIDTH
    vb_ref[...] = unflat(proj(off, MLSTM_WIDTH))
    off += MLSTM_WIDTH
    ob_ref[...] = unflat(proj(off, MLSTM_WIDTH))
    gate_ref[...] = unflat(jnp.dot(h, wg_ref[...], preferred_element_type=F32) + bg_ref[...])

    if window_tiles:
        kt_ref, vt_ref = win_refs
        if stacked:
            for s in range(n_seq):
                kt_ref[s] = jnp.transpose(k_ref[s])
                vt_ref[s] = jnp.transpose(v_ref[s])
        else:
            kt_ref[...] = jnp.transpose(k_ref[...])
            vt_ref[...] = jnp.transpose(v_ref[...])


def _inproj(x2d, g1, w_main_bf, w_gate_bf, bg_pad, tables, tm, table_blocks, tiles_per_seq=1, window_tiles=0):
    n = x2d.shape[0]
    grid = (n // tm,)
    row = lambda i: (i, 0)
    tab = lambda i: (i % table_blocks, 0)
    widths = (ATT_WIDTH, ATT_WIDTH, ATT_WIDTH, 2 * MLSTM_WIDTH, MLSTM_WIDTH, MLSTM_WIDTH, GATE_PAD)
    out_specs = [pl.BlockSpec((tm, w), row) for w in widths]
    out_shape = [jax.ShapeDtypeStruct((n, w), F32) for w in widths]
    if window_tiles:
        first = tiles_per_seq - window_tiles
        win = pl.BlockSpec((None, ATT_WIDTH, tm),
                           lambda i: (i // tiles_per_seq, 0, jnp.maximum(i % tiles_per_seq - first, 0)))
        out_specs += [win, win]
        out_shape += [jax.ShapeDtypeStruct((n // (tm * tiles_per_seq), ATT_WIDTH, window_tiles * tm), F32)] * 2
    return pl.pallas_call(
        functools.partial(_inproj_kernel, tiles_per_seq=tiles_per_seq, window_tiles=window_tiles),
        grid=grid,
        in_specs=[
            pl.BlockSpec((tm, D_MODEL), row),
            _const_spec((1, D_MODEL)),
            _const_spec((D_MODEL, IN_COLS - 2 * MLSTM_HEADS)),
            _const_spec((D_MODEL, GATE_PAD)),
            _const_spec((1, GATE_PAD)),
            pl.BlockSpec((tm, LANES), tab),
            pl.BlockSpec((tm, LANES), tab),
            pl.BlockSpec((tm, LANES), tab),
        ],
        out_specs=out_specs,
        out_shape=out_shape,
        compiler_params=_cparams(("arbitrary",)),
        name="inproj",
    )(x2d, g1, w_main_bf, w_gate_bf, bg_pad, *tables)


def _rotary_tables(pos):
    half = np.arange(ROT_HALF, dtype=np.float64)
    ang = np.asarray(pos, np.float64)[:, None] * (ROPE_THETA ** (-half / ROT_HALF))[None, :]
    cos, sin = np.cos(ang), np.sin(ang)
    p = ang.shape[0]
    ones = np.ones((p, HD - ROT_DIM))
    zeros = np.zeros((p, HD - ROT_DIM))
    z8 = np.zeros((p, ROT_HALF))
    cos_h = np.concatenate([cos, cos, ones], axis=1)
    sa_h = np.concatenate([-sin, z8, zeros], axis=1)
    sb_h = np.concatenate([z8, sin, zeros], axis=1)
    two = lambda t: np.concatenate([t, t], axis=1).astype(np.float32)
    return two(cos_h), two(sa_h), two(sb_h)


def _attn_prompt_kernel(bias_ref, q_ref, k_ref, v_ref, o_ref,
                        q4_ref, k4_ref, v4_ref, num1_ref, m1_ref, den1_ref, num4_ref, m4_ref, den4_ref,
                        kb_ref, vb_ref, k4b_ref, v4b_ref, *, seq):
    d_mid, d_out = DILATIONS[1], DILATIONS[2]
    d_in = d_out // d_mid
    lane = lax.broadcasted_iota(jnp.int32, (ATT_BLOCK, LANES), 1)
    head0 = lane < HD
    nblk = seq // ATT_BLOCK
    sub_blocks = nblk // d_mid
    ones_cols = jnp.ones((2 * ATT_BLOCK, LANES), BF16)

    def scores(q, kk, first):
        q = q * LOG2E
        q2 = jnp.concatenate([jnp.where(head0, q, 0.0), jnp.where(head0, 0.0, q)], axis=0).astype(BF16)
        s = lax.dot_general(q2, kk.astype(BF16), (((1,), (1,)), ((), ())),
                            preferred_element_type=F32)
        return s + bias_ref[first]

    def weighted_values(p, vv):
        o2 = jnp.dot(p.astype(BF16), jnp.concatenate([vv.astype(BF16), ones_cols], axis=1),
                     preferred_element_type=F32)
        o = jnp.where(head0, o2[:ATT_BLOCK, :LANES], o2[ATT_BLOCK:, :LANES])
        den = jnp.where(head0, o2[:ATT_BLOCK, LANES:], o2[ATT_BLOCK:, LANES:])
        return o, den

    def aligned(start):
        return pl.ds(pl.multiple_of(start, ATT_BLOCK), ATT_BLOCK)

    def split(i, carry):
        r, c = i // sub_blocks, i % sub_blocks
        src = pl.ds(r + c * ATT_BLOCK * d_mid, ATT_BLOCK, stride=d_mid)
        dst = aligned(i * ATT_BLOCK)
        q4_ref[dst, :] = q_ref[src, :]
        k4, v4 = k_ref[src, :], v_ref[src, :]
        k4_ref[dst, :] = k4
        v4_ref[dst, :] = v4
        k4b_ref[dst, :] = k4.astype(BF16)
        v4b_ref[dst, :] = v4.astype(BF16)
        kb_ref[dst, :] = k_ref[dst, :].astype(BF16)
        vb_ref[dst, :] = v_ref[dst, :].astype(BF16)
        return carry

    lax.fori_loop(0, nblk, split, 0)

    def run_group(srcs, blocks_per_class, rows_of, merge, unroll=ATT_UNROLL):
        qs, ks, vs = srcs

        def body(i, carry):
            blocks = []
            for u in range(unroll):
                idx = i * unroll + u
                n = idx % blocks_per_class
                rows = rows_of(idx, n)
                prows = rows_of(idx - jnp.minimum(n, 1), jnp.maximum(n - 1, 0))
                first = jnp.where(n == 0, 1, 0)
                kk = jnp.concatenate([ks[prows, :], ks[rows, :]], axis=0)
                blocks.append([rows, prows, scores(qs[rows, :], kk, first)])
            for blk in blocks:
                s = blk[2]
                mx = jnp.max(s, axis=1, keepdims=True)
                blk[2] = jnp.exp2(s - mx)
                blk.append(jnp.where(head0, mx[:ATT_BLOCK], mx[ATT_BLOCK:]))
            for blk in blocks:
                rows, prows, p, _ = blk
                vv = jnp.concatenate([vs[prows, :], vs[rows, :]], axis=0)
                blk[2:3] = weighted_values(p, vv)
            for rows, _, o, den, mxb in blocks:
                merge(rows, o, mxb, den)
            return carry

        lax.fori_loop(0, nblk // unroll, body, 0)

    def init_state(num_ref, m_ref, den_ref):
        def merge(rows, o, mxb, den):
            num_ref[rows, :] = o
            m_ref[rows, :] = mxb
            den_ref[rows, :] = den
        return merge

    def merge4(rows, o, mxb, den):
        m_old = m4_ref[rows, :]
        m_new = jnp.maximum(m_old, mxb)
        a = jnp.exp2(m_old - m_new)
        b = jnp.exp2(mxb - m_new)
        num4_ref[rows, :] = num4_ref[rows, :] * a + o * b
        den4_ref[rows, :] = den4_ref[rows, :] * a + den * b
        m4_ref[rows, :] = m_new

    run_group((q_ref, kb_ref, vb_ref), nblk, lambda idx, n: aligned(idx * ATT_BLOCK),
              init_state(num1_ref, m1_ref, den1_ref))
    blocks16 = nblk // d_out

    def rows16(idx, n):
        cls = idx // blocks16
        r, j = cls // d_in, cls % d_in
        return pl.ds(r * (sub_blocks * ATT_BLOCK) + j + n * ATT_BLOCK * d_in, ATT_BLOCK, stride=d_in)

    run_group((q4_ref, k4_ref, v4_ref), blocks16, rows16, init_state(num4_ref, m4_ref, den4_ref))
    run_group((q4_ref, k4b_ref, v4b_ref), sub_blocks, lambda idx, n: aligned(idx * ATT_BLOCK), merge4)

    def finish(i, carry):
        r, c = i // sub_blocks, i % sub_blocks
        tok = pl.ds(r + c * ATT_BLOCK * d_mid, ATT_BLOCK, stride=d_mid)
        res = aligned(i * ATT_BLOCK)
        m_a, m_b = m4_ref[res, :], m1_ref[tok, :]
        m_new = jnp.maximum(m_a, m_b)
        wa = jnp.exp2(m_a - m_new)
        wb = jnp.exp2(m_b - m_new)
        num = num4_ref[res, :] * wa + num1_ref[tok, :] * wb
        den = den4_ref[res, :] * wa + den1_ref[tok, :] * wb
        o_ref[tok, :] = num / den
        return carry

    lax.fori_loop(0, nblk, finish, 0, unroll=4)


def _band_bias():
    row = np.arange(2 * ATT_BLOCK)[:, None] % ATT_BLOCK
    ki = np.arange(2 * ATT_BLOCK)[None, :] - ATT_BLOCK
    rel = row - ki
    valid = (rel >= 0) & (rel <= SUB_WINDOW)
    b0 = np.where(valid, 0.0, NEG_INF).astype(np.float32)
    b1 = np.where(valid & (ki >= 0), 0.0, NEG_INF).astype(np.float32)
    return jnp.asarray(np.stack([b0, b1]))


def _attn_prompt(q, k, v):
    b, s, _ = q.shape
    assert DILATIONS[0] == 1 and DILATIONS[2] % DILATIONS[1] == 0
    assert s % (ATT_BLOCK * DILATIONS[-1]) == 0 and (s // ATT_BLOCK) % ATT_UNROLL == 0
    blk = pl.BlockSpec((None, s, LANES), lambda i, j: (i, 0, j))
    return pl.pallas_call(
        functools.partial(_attn_prompt_kernel, seq=s),
        grid=(b, ATT_WIDTH // LANES),
        in_specs=[_const_spec((2, 2 * ATT_BLOCK, 2 * ATT_BLOCK)), blk, blk, blk],
        out_specs=blk,
        out_shape=jax.ShapeDtypeStruct((b, s, ATT_WIDTH), F32),
        scratch_shapes=[pltpu.VMEM((s, LANES), F32)] * 9 + [pltpu.VMEM((s, LANES), BF16)] * 4,
        compiler_params=_cparams(("parallel", "parallel")),
        name="attn_prompt",
    )(_band_bias(), q, k, v)


def _attn_sample_body(mw_ref, mn_ref, q_ref, kn_ref, vn_ref, kc_ref, vc_ref,
                      att_ref, nk_ref, nv_ref, *, win, t_new, heads):
    mult_w = mw_ref[...]
    mult_n = mn_ref[...]
    lane = lax.broadcasted_iota(jnp.int32, (HD, LANES), 1)
    keep = lane < LANES - t_new
    top = jnp.zeros((LANES - t_new, heads * HD), F32)
    knt_all = jnp.transpose(jnp.concatenate([top, kn_ref[...]], axis=0))
    vnt_all = jnp.transpose(jnp.concatenate([top, vn_ref[...]], axis=0))
    head_rows = lambda a, h: a[h * HD:(h + 1) * HD, :]
    for h in range(heads):
        for src_ref, new_all, dst in ((kc_ref, knt_all, nk_ref), (vc_ref, vnt_all, nv_ref)):
            rolled = pltpu.roll(src_ref[h], win - t_new, 1)
            dst[h, :, 0:win - LANES] = rolled[:, 0:win - LANES]
            dst[h, :, win - LANES:win] = jnp.where(keep, rolled[:, win - LANES:win], head_rows(new_all, h))
    sc = []
    for h in range(heads):
        qh = q_ref[:, h * HD:(h + 1) * HD].astype(BF16)
        s_w = jnp.dot(qh, kc_ref[h].astype(BF16), preferred_element_type=F32)
        s_n = jnp.dot(qh, head_rows(knt_all, h).astype(BF16), preferred_element_type=F32)
        sc.append((jnp.where(mult_w > 0, s_w, NEG_INF), jnp.where(mult_n > 0, s_n, NEG_INF)))
    pr = []
    for s_w, s_n in sc:
        mx = jnp.maximum(jnp.max(s_w, axis=1, keepdims=True), jnp.max(s_n, axis=1, keepdims=True))
        p_w = mult_w * jnp.exp(s_w - mx)
        p_n = mult_n * jnp.exp(s_n - mx)
        den = jnp.sum(p_w, axis=1, keepdims=True) + jnp.sum(p_n, axis=1, keepdims=True)
        pr.append((p_w, p_n, den))
    for h, (p_w, p_n, den) in enumerate(pr):
        o = lax.dot_general(p_w.astype(BF16), vc_ref[h].astype(BF16), (((1,), (1,)), ((), ())),
                            preferred_element_type=F32)
        o = o + lax.dot_general(p_n.astype(BF16), head_rows(vnt_all, h).astype(BF16),
                                (((1,), (1,)), ((), ())), preferred_element_type=F32)
        att_ref[:, h * HD:(h + 1) * HD] = o / den


def _attn_sample_kernel(*refs, win, t_new):
    _attn_sample_body(*refs, win=win, t_new=t_new, heads=N_ATT_HEADS)


def _attn_mlp_kernel(mw_ref, mn_ref, q_ref, kn_ref, vn_ref, kc_ref, vc_ref,
                     x_ref, attp_ref, hbp_ref, wo_ref, g2_ref, wu_ref, wd_ref, gf_ref,
                     att_ref, nk_ref, nv_ref, y_ref, h2_sc, acc_sc, *, win, t_new, heads):
    c = pl.program_id(0) % FF_PARTS

    @pl.when(c == 0)
    def _():
        mix = (jnp.dot(attp_ref[...].astype(BF16), wo_ref[0:ATT_WIDTH, :], preferred_element_type=F32)
               + jnp.dot(hbp_ref[...].astype(BF16), wo_ref[ATT_WIDTH:D_MODEL, :], preferred_element_type=F32))
        x1 = x_ref[...] + mix
        h2_sc[...] = _rms(x1, g2_ref[...]).astype(BF16)
        acc_sc[...] = x1

    u = jnp.dot(h2_sc[...], wu_ref[c], preferred_element_type=F32)
    r = jnp.maximum(u, 0.0)
    acc_sc[...] += jnp.dot((r * r).astype(BF16), wd_ref[c], preferred_element_type=F32)
    _attn_sample_body(mw_ref, mn_ref, q_ref, kn_ref, vn_ref, kc_ref, vc_ref, att_ref, nk_ref, nv_ref,
                      win=win, t_new=t_new, heads=heads)

    @pl.when(c == FF_PARTS - 1)
    def _():
        y_ref[...] = _rms(acc_sc[...], gf_ref[...])


def _sample_multiplicity(win, t_new):
    t = np.arange(t_new)[:, None]
    idx = np.arange(win + t_new)[None, :]
    back = win + t - idx
    mult = np.zeros((t_new, win + t_new), np.float32)
    for dil in DILATIONS:
        mult += ((back >= 0) & (back % dil == 0) & (back // dil <= SUB_WINDOW)).astype(np.float32)
    mw = mult[:, :win]
    mn = np.zeros((t_new, LANES), np.float32)
    mn[:, LANES - t_new:] = mult[:, win:]
    return jnp.asarray(mw), jnp.asarray(mn)


def _attn_sample(q, k_new, v_new, kc_t, vc_t, t_new):
    b, nh, _, win = kc_t.shape
    assert win == DILATIONS[-1] * SUB_WINDOW and t_new == SUBLANES and q.shape[0] == b * t_new
    mw, mn = _sample_multiplicity(win, t_new)
    rows = pl.BlockSpec((t_new, ATT_WIDTH), lambda i: (i, 0))
    per_b = lambda *shape: pl.BlockSpec((None,) + shape, lambda i: (i,) + (0,) * len(shape))
    return pl.pallas_call(
        functools.partial(_attn_sample_kernel, win=win, t_new=t_new),
        grid=(b,),
        in_specs=[
            _const_spec((t_new, win)), _const_spec((t_new, LANES)),
            rows, rows, rows, per_b(nh, HD, win), per_b(nh, HD, win),
        ],
        out_specs=[rows, per_b(nh, HD, win), per_b(nh, HD, win)],
        out_shape=[
            jax.ShapeDtypeStruct((b * t_new, ATT_WIDTH), F32),
            jax.ShapeDtypeStruct((b, nh, HD, win), F32),
            jax.ShapeDtypeStruct((b, nh, HD, win), F32),
        ],
        compiler_params=_cparams(("parallel",)),
        name="attn_sample",
    )(mw, mn, q, k_new, v_new, kc_t, vc_t)


def _attn_mlp_fits(b_s, n_p):
    steps = b_s * ATT_HEAD_SPLIT
    if steps % FF_PARTS:
        return False
    tiles = steps // FF_PARTS
    return n_p % tiles == 0 and (n_p // tiles) % SUBLANES == 0 and n_p // tiles <= ROW_TILE


def _attn_mlp(q, k_new, v_new, kc_t, vc_t, t_new, x2d, attp, hbp, wo_bf, g2, wu_parts, wd_parts, gf):
    b, nh, _, win = kc_t.shape
    n_p = x2d.shape[0]
    assert win == DILATIONS[-1] * SUB_WINDOW and t_new == SUBLANES and q.shape[0] == b * t_new
    heads = nh // ATT_HEAD_SPLIT
    steps = b * ATT_HEAD_SPLIT
    tm = n_p // (steps // FF_PARTS)
    ffc = D_FF // FF_PARTS
    mw, mn = _sample_multiplicity(win, t_new)
    rows = pl.BlockSpec((t_new, heads * HD), lambda i: (i // ATT_HEAD_SPLIT, i % ATT_HEAD_SPLIT))
    win_blk = pl.BlockSpec((None, heads, HD, win), lambda i: (i // ATT_HEAD_SPLIT, i % ATT_HEAD_SPLIT, 0, 0))
    tile = lambda w: pl.BlockSpec((tm, w), lambda i: (i // FF_PARTS, 0))
    return pl.pallas_call(
        functools.partial(_attn_mlp_kernel, win=win, t_new=t_new, heads=heads),
        grid=(steps,),
        in_specs=[
            _const_spec((t_new, win)), _const_spec((t_new, LANES)),
            rows, rows, rows, win_blk, win_blk,
            tile(D_MODEL), tile(ATT_WIDTH), tile(MLSTM_WIDTH),
            _const_spec((D_MODEL, D_MODEL)), _const_spec((1, D_MODEL)),
            _const_spec((FF_PARTS, D_MODEL, ffc)), _const_spec((FF_PARTS, ffc, D_MODEL)),
            _const_spec((1, D_MODEL)),
        ],
        out_specs=[rows, win_blk, win_blk, tile(D_MODEL)],
        out_shape=[
            jax.ShapeDtypeStruct((b * t_new, ATT_WIDTH), F32),
            jax.ShapeDtypeStruct((b, nh, HD, win), F32),
            jax.ShapeDtypeStruct((b, nh, HD, win), F32),
            jax.ShapeDtypeStruct((n_p, D_MODEL), F32),
        ],
        scratch_shapes=[pltpu.VMEM((tm, D_MODEL), BF16), pltpu.VMEM((tm, D_MODEL), F32)],
        compiler_params=_cparams(("arbitrary",)),
        name="attn_mlp",
    )(mw, mn, q, k_new, v_new, kc_t, vc_t, x2d, attp, hbp, wo_bf, g2, wu_parts, wd_parts, gf)


def _split_dot(tri, x):
    hi = x.astype(BF16)
    lo = (x - hi.astype(F32)).astype(BF16)
    return (jnp.dot(tri, hi, preferred_element_type=F32)
            + jnp.dot(tri, lo, preferred_element_type=F32))


def _split_dot_rows(x, tri):
    hi = x.astype(BF16)
    lo = (x - hi.astype(F32)).astype(BF16)
    return (jnp.dot(hi, tri, preferred_element_type=F32)
            + jnp.dot(lo, tri, preferred_element_type=F32))


def _log_sigmoid(x):
    return jnp.minimum(x, 0.0) - jnp.log1p(jnp.exp(-jnp.abs(x)))


def _mlstm_init(xp_ref, c_sc, n_sc, m_sc, nm_sc, group, state_refs=None):
    if state_refs is not None:
        tail_ref, c0_ref, n0_ref, m0_ref = state_refs
        for c in range(2 * MLSTM_HEADS):
            xp_ref[:, c, 0:SUBLANES, :] = tail_ref[:, :, c * LANES:(c + 1) * LANES]
        c_sc[...] = c0_ref[...]
        n_sc[...] = n0_ref[...]
        m_sc[...] = m0_ref[...]
    else:
        xp_ref[:, :, 0:SUBLANES, :] = jnp.zeros((group, 2 * MLSTM_HEADS, SUBLANES, LANES), F32)
        c_sc[...] = jnp.zeros_like(c_sc)
        n_sc[...] = jnp.zeros_like(n_sc)
        m_sc[...] = jnp.zeros_like(m_sc)
        nm_sc[...] = jnp.zeros_like(nm_sc)


def _mlstm_emit(c_out_ref, n_out_ref, m_out_ref, c_sc, n_sc, m_sc, nm_sc, group, mxu_sums):
    if mxu_sums:
        for g in range(group):
            for h in range(MLSTM_HEADS):
                n_sc[g, h:h + 1, :] = jnp.transpose(nm_sc[g, h])[0:1, :]
    c_out_ref[...] = c_sc[...]
    n_out_ref[...] = n_sc[...]
    m_out_ref[...] = m_sc[...]


def _mlstm_chunk(raws, vbs, obs, gate_list, hbs, tails, wc_ref, bc_ref, g_ref,
                 xp_ref, act_ref, c_sc, n_sc, m_sc, nm_sc, *, chunk, mxu_sums):
    L = chunk
    group = len(raws)
    r_i = lax.broadcasted_iota(jnp.int32, (L, L), 0)
    c_i = lax.broadcasted_iota(jnp.int32, (L, L), 1)
    causal = r_i >= c_i
    tri_l = jnp.where(causal, 1.0, 0.0).astype(BF16)
    tri_u = jnp.where(r_i <= c_i, 1.0, 0.0).astype(BF16)
    first_tap = SUBLANES - (CONV_W - 1)
    bc = bc_ref[...]
    taps = [wc_ref[w:w + 1, :] for w in range(CONV_W)]
    ones_sq = jnp.ones((LANES, LANES), BF16)

    per_batch = []
    for g in range(group):
        tails[g][...] = raws[g][L - SUBLANES:L, :]
        for c in range(2 * MLSTM_HEADS):
            cl = slice(c * LANES, (c + 1) * LANES)
            raw = raws[g][:, cl]
            xp_ref[g, c, SUBLANES:SUBLANES + L, :] = raw
            if L % (SUBLANES * SUBLANES) == 0:
                n_rows = L // SUBLANES
                slabs = [xp_ref[g, c, pl.ds(first_tap + j, n_rows, stride=SUBLANES), :]
                         for j in range(SUBLANES + CONV_W - 1)]
                for s in range(SUBLANES):
                    conv = bc[:, cl]
                    for w in range(CONV_W):
                        conv = conv + slabs[s + w] * taps[w][:, cl]
                    act_ref[g, c, pl.ds(s, n_rows, stride=SUBLANES), :] = conv * jax.nn.sigmoid(conv)
            else:
                conv = bc[:, cl]
                for w in range(CONV_W):
                    conv = conv + xp_ref[g, c, first_tap + w:first_tap + w + L, :] * taps[w][:, cl]
                act_ref[g, c] = conv * jax.nn.sigmoid(conv)
            xp_ref[g, c, 0:SUBLANES, :] = raw[L - SUBLANES:L, :]

        gates = gate_list[g][...]
        lf = _log_sigmoid(gates)
        b_col = _split_dot(tri_l, lf)
        if L == LANES:
            gates_sq = gates
        else:
            gates_sq = jnp.concatenate([gates, jnp.zeros((LANES - L, LANES), F32)], axis=0)
        gates_t = jnp.transpose(gates_sq)[0:SUBLANES, 0:L]
        lf_t = _log_sigmoid(gates_t)
        b_row = _split_dot_rows(lf_t, tri_u)

        per_batch.append((gates, b_col, gates_t, b_row))

    items = [(g, h) for g in range(group) for h in range(MLSTM_HEADS)]
    st = {}
    for g, h in items:
        gates, b_col, gates_t, b_row = per_batch[g]
        ig_c = gates[:, h:h + 1]
        b_c = b_col[:, MLSTM_HEADS + h:MLSTM_HEADS + h + 1]
        ig_r = gates_t[h:h + 1, :]
        b_r = b_row[MLSTM_HEADS + h:MLSTM_HEADS + h + 1, :]
        if mxu_sums:
            b_c_wide = jnp.broadcast_to(b_c, (L, LANES))
            a_c = b_c_wide + m_sc[g, h:h + 1, 0:1]
            dm = jnp.where(causal, b_c_wide - b_r + ig_r, NEG_INF)
            mt = jnp.maximum(a_c, jnp.broadcast_to(jnp.max(dm, axis=1, keepdims=True), (L, LANES)))
        else:
            a_c = b_c + m_sc[g, h:h + 1, 0:1]
            dm = jnp.where(causal, b_c - b_r + ig_r, NEG_INF)
            mt = jnp.maximum(a_c, jnp.max(dm, axis=1, keepdims=True))
        st[g, h] = dict(ig_c=ig_c, b_c=b_c, a_c=a_c, dm=dm, mt=mt)
    for g, h in items:
        d = st[g, h]
        q = act_ref[g, h]
        k = act_ref[g, MLSTM_HEADS + h] * (MLSTM_HD ** -0.5)
        v = vbs[g][:, h * MLSTM_HD:(h + 1) * MLSTM_HD]
        qb, kb = q.astype(BF16), k.astype(BF16)
        w_inter = jnp.exp(d["a_c"] - d["mt"])
        wm = jnp.exp(d["dm"] - d["mt"])
        sc = lax.dot_general(qb, kb, (((1,), (1,)), ((), ())), preferred_element_type=F32) * wm
        d.update(q=q, k=k, v=v, qb=qb, vb=v.astype(BF16), w_inter=w_inter, sc=sc)
        del d["dm"]
    for g, h in items:
        d = st[g, h]
        c_old = c_sc[g, h]
        if mxu_sums:
            nm_old = nm_sc[g, h]
            v_ones = jnp.concatenate([d["vb"], ones_sq], axis=1)
            state = jnp.concatenate([c_old, nm_old], axis=1).astype(BF16)
            inter = jnp.dot(d["qb"], state, preferred_element_type=F32)
            intra = jnp.dot(d["sc"].astype(BF16), v_ones, preferred_element_type=F32)
            num = d["w_inter"] * inter[:, :LANES] + intra[:, :LANES]
            den = d["w_inter"] * inter[:, LANES:] + intra[:, LANES:]
            d.update(v_ones=v_ones, nm_old=nm_old)
        else:
            n_old = n_sc[g, h:h + 1, :]
            num = (d["w_inter"] * jnp.dot(d["qb"], c_old.astype(BF16), preferred_element_type=F32)
                   + jnp.dot(d["sc"].astype(BF16), d["vb"], preferred_element_type=F32))
            den = (d["w_inter"] * jnp.sum(d["q"] * n_old, axis=1, keepdims=True)
                   + jnp.sum(d["sc"], axis=1, keepdims=True))
            d.update(n_old=n_old)
        d.update(c_old=c_old, hh=num / jnp.maximum(jnp.abs(den), jnp.exp(-d["mt"])))
        del d["sc"]
    for g, h in items:
        d = st[g, h]
        mt, a_c, b_c = d["mt"], d["a_c"], d["b_c"]
        m_last = mt[L - 1:L, 0:1]
        w_last = jnp.exp(a_c[L - 1:L, 0:1] - m_last)
        w_t = jnp.exp(b_c[L - 1:L, :] - b_c + d["ig_c"] - m_last)
        kw = d["k"] * w_t
        if mxu_sums:
            upd = jnp.dot(jnp.transpose(kw).astype(BF16), d["v_ones"], preferred_element_type=F32)
            c_sc[g, h] = w_last * d["c_old"] + upd[:, :LANES]
            nm_sc[g, h] = w_last * d["nm_old"] + upd[:, LANES:]
        else:
            if L == LANES:
                kw_sq, v_sq = kw, d["v"]
            else:
                pad = jnp.zeros((LANES - L, MLSTM_HD), F32)
                kw_sq = jnp.concatenate([kw, pad], axis=0)
                v_sq = jnp.concatenate([d["v"], pad], axis=0)
            c_sc[g, h] = w_last * d["c_old"] + jnp.dot(jnp.transpose(kw_sq).astype(BF16), v_sq.astype(BF16),
                                                       preferred_element_type=F32)
            n_sc[g, h:h + 1, :] = w_last * d["n_old"] + jnp.sum(kw, axis=0, keepdims=True)
        m_sc[g, h:h + 1, :] = jnp.broadcast_to(m_last, (1, LANES))
    for g, h in items:
        hh = st[g, h]["hh"]
        sl = slice(h * MLSTM_HD, (h + 1) * MLSTM_HD)
        if mxu_sums:
            mean_sq = _split_dot_rows(hh * hh, ones_sq) * (1.0 / MLSTM_HD)
        else:
            mean_sq = jnp.mean(hh * hh, axis=1, keepdims=True)
        hn = hh * lax.rsqrt(mean_sq + EPS)
        hbs[g][:, sl] = hn * g_ref[:, sl] * jax.nn.sigmoid(obs[g][:, sl])


def _mlstm_kernel(*refs, chunk, group, has_state):
    if has_state:
        (raw_ref, vb_ref, ob_ref, gate_ref, wc_ref, bc_ref, g_ref, tail_ref, c0_ref, n0_ref, m0_ref,
         hb_ref, tail_out_ref, c_out_ref, n_out_ref, m_out_ref, xp_ref, act_ref, c_sc, n_sc, m_sc) = refs
        nm_sc, state_refs = None, (tail_ref, c0_ref, n0_ref, m0_ref)
    else:
        (raw_ref, vb_ref, ob_ref, gate_ref, wc_ref, bc_ref, g_ref,
         hb_ref, tail_out_ref, c_out_ref, n_out_ref, m_out_ref, xp_ref, act_ref, c_sc, n_sc, m_sc, nm_sc) = refs
        state_refs = None
    c_idx = pl.program_id(1)
    mxu_sums = nm_sc is not None and chunk == LANES

    @pl.when(c_idx == 0)
    def _():
        _mlstm_init(xp_ref, c_sc, n_sc, m_sc, nm_sc, group, state_refs)

    per_g = lambda ref: [ref.at[g] for g in range(group)]
    _mlstm_chunk(per_g(raw_ref), per_g(vb_ref), per_g(ob_ref), per_g(gate_ref), per_g(hb_ref), per_g(tail_out_ref),
                 wc_ref, bc_ref, g_ref, xp_ref, act_ref, c_sc, n_sc, m_sc, nm_sc, chunk=chunk, mxu_sums=mxu_sums)

    @pl.when(c_idx == pl.num_programs(1) - 1)
    def _():
        _mlstm_emit(c_out_ref, n_out_ref, m_out_ref, c_sc, n_sc, m_sc, nm_sc, group, mxu_sums)


def _mlstm(raw, vb, ob, gates, w_conv, b_conv, mh_g, chunk, group, state=None):
    b, t, _ = raw.shape
    nc = t // chunk
    assert nc * chunk == t and chunk % SUBLANES == 0 and b % group == 0
    seq = lambda w: pl.BlockSpec((group, chunk, w), lambda i, j: (i, j, 0))
    per_b = lambda *shape: pl.BlockSpec((group,) + shape, lambda i, j: (i,) + (0,) * len(shape))
    in_specs = [seq(2 * MLSTM_WIDTH), seq(MLSTM_WIDTH), seq(MLSTM_WIDTH), seq(GATE_PAD),
                _const_spec((CONV_W, 2 * MLSTM_WIDTH)), _const_spec((1, 2 * MLSTM_WIDTH)),
                _const_spec((1, MLSTM_WIDTH))]
    args = [raw, vb, ob, gates, w_conv, b_conv, mh_g]
    state_shapes = [(SUBLANES, 2 * MLSTM_WIDTH), (MLSTM_HEADS, MLSTM_HD, MLSTM_HD),
                    (MLSTM_HEADS, MLSTM_HD), (MLSTM_HEADS, LANES)]
    state_specs = [per_b(*sh) for sh in state_shapes]
    if state is not None:
        in_specs += state_specs
        args += list(state)
    return pl.pallas_call(
        functools.partial(_mlstm_kernel, chunk=chunk, group=group, has_state=state is not None),
        grid=(b // group, nc),
        in_specs=in_specs,
        out_specs=[seq(MLSTM_WIDTH)] + state_specs,
        out_shape=[jax.ShapeDtypeStruct((b, t, MLSTM_WIDTH), F32)]
        + [jax.ShapeDtypeStruct((b,) + sh, F32) for sh in state_shapes],
        scratch_shapes=[
            pltpu.VMEM((group, 2 * MLSTM_HEADS, SUBLANES + chunk, LANES), F32),
            pltpu.VMEM((group, 2 * MLSTM_HEADS, chunk, LANES), F32),
        ] + [pltpu.VMEM((group,) + sh, F32) for sh in state_shapes[1:]]
        + ([] if state is not None else [pltpu.VMEM((group,) + state_shapes[1], F32)]),
        compiler_params=_cparams(("parallel", "arbitrary")),
        name="mlstm_state" if state is not None else "mlstm",
    )(*args)


def _inproj_mlstm_kernel(x_ref, g_ref, w_ref, wg_ref, bg_ref, cos_ref, sa_ref, sb_ref, wc_ref, bc_ref, mhg_ref,
                         q_ref, k_ref, v_ref, kt_ref, vt_ref, hb_ref, tail_ref, c_out_ref, n_out_ref, m_out_ref,
                         *scratch, tiles_per_seq, window_tiles):
    stage_a, stage_b = scratch[0:4], scratch[4:8]
    xp_ref, act_ref, c_sc, n_sc, m_sc, nm_sc = scratch[8:]
    i = pl.program_id(0)
    n_seq, tm = hb_ref.shape[0], hb_ref.shape[1]

    @pl.when(i == 0)
    def _():
        for st in stage_b:
            st[...] = jnp.zeros(st.shape, F32)

    prev_tile_in_seq = jnp.maximum(i - 1, 0) % tiles_per_seq

    @pl.when(prev_tile_in_seq == 0)
    def _():
        _mlstm_init(xp_ref, c_sc, n_sc, m_sc, nm_sc, n_seq)

    def step(write, read):
        _inproj_kernel(x_ref, g_ref, w_ref, wg_ref, bg_ref, cos_ref, sa_ref, sb_ref, q_ref, k_ref, v_ref,
                       *write, kt_ref, vt_ref, tiles_per_seq=tiles_per_seq, window_tiles=window_tiles)
        raw, vb, ob, gate = read
        for cc in range(tm // CHUNK):
            rows = pl.ds(cc * CHUNK, CHUNK)
            each = lambda ref: [ref.at[s_, rows, :] for s_ in range(n_seq)]
            _mlstm_chunk(each(raw), each(vb), each(ob), each(gate), each(hb_ref),
                         [tail_ref.at[s_] for s_ in range(n_seq)], wc_ref, bc_ref, mhg_ref,
                         xp_ref, act_ref, c_sc, n_sc, m_sc, nm_sc, chunk=CHUNK, mxu_sums=True)

    @pl.when(i % 2 == 0)
    def _():
        step(stage_a, stage_b)

    @pl.when(i % 2 == 1)
    def _():
        step(stage_b, stage_a)

    @pl.when(prev_tile_in_seq == tiles_per_seq - 1)
    def _():
        _mlstm_emit(c_out_ref, n_out_ref, m_out_ref, c_sc, n_sc, m_sc, nm_sc, n_seq, True)


def _inproj_mlstm(x3d, g1, w_main_bf, w_gate_bf, bg_pad, tables, w_conv, b_conv, mh_g, tm, n_seq, window):
    b, t, _ = x3d.shape
    tiles_per_seq, window_tiles = t // tm, window // tm
    n_steps = (b // n_seq) * tiles_per_seq
    assert b % n_seq == 0 and t % tm == 0 and window_tiles >= 1 and tm % CHUNK == 0 and CHUNK == LANES
    cur = lambda i: jnp.minimum(i, n_steps - 1)
    prv = lambda i: jnp.maximum(i - 1, 0)
    row = lambda w: pl.BlockSpec((n_seq, tm, w), lambda i: (cur(i) // tiles_per_seq, cur(i) % tiles_per_seq, 0))
    tab = pl.BlockSpec((tm, LANES), lambda i: (cur(i) % tiles_per_seq, 0))
    first = tiles_per_seq - window_tiles
    win = pl.BlockSpec((n_seq, ATT_WIDTH, tm),
                       lambda i: (cur(i) // tiles_per_seq, 0, jnp.maximum(cur(i) % tiles_per_seq - first, 0)))
    seq_state = lambda *shape: pl.BlockSpec((n_seq,) + shape, lambda i: (prv(i) // tiles_per_seq,) + (0,) * len(shape))
    state_shapes = [(MLSTM_HEADS, MLSTM_HD, MLSTM_HD), (MLSTM_HEADS, MLSTM_HD), (MLSTM_HEADS, LANES)]
    return pl.pallas_call(
        functools.partial(_inproj_mlstm_kernel, tiles_per_seq=tiles_per_seq, window_tiles=window_tiles),
        grid=(n_steps + 1,),
        in_specs=[
            row(D_MODEL), _const_spec((1, D_MODEL)),
            _const_spec((D_MODEL, IN_COLS - 2 * MLSTM_HEADS)), _const_spec((D_MODEL, GATE_PAD)),
            _const_spec((1, GATE_PAD)), tab, tab, tab,
            _const_spec((CONV_W, 2 * MLSTM_WIDTH)), _const_spec((1, 2 * MLSTM_WIDTH)), _const_spec((1, MLSTM_WIDTH)),
        ],
        out_specs=[row(ATT_WIDTH), row(ATT_WIDTH), row(ATT_WIDTH), win, win,
                   pl.BlockSpec((n_seq, tm, MLSTM_WIDTH),
                                lambda i: (prv(i) // tiles_per_seq, prv(i) % tiles_per_seq, 0)),
                   seq_state(SUBLANES, 2 * MLSTM_WIDTH)]
        + [seq_state(*sh) for sh in state_shapes],
        out_shape=[jax.ShapeDtypeStruct((b, t, ATT_WIDTH), F32)] * 3
        + [jax.ShapeDtypeStruct((b, ATT_WIDTH, window), F32)] * 2
        + [jax.ShapeDtypeStruct((b, t, MLSTM_WIDTH), F32), jax.ShapeDtypeStruct((b, SUBLANES, 2 * MLSTM_WIDTH), F32)]
        + [jax.ShapeDtypeStruct((b,) + sh, F32) for sh in state_shapes],
        scratch_shapes=[pltpu.VMEM((n_seq, tm, w), F32)
                        for w in (2 * MLSTM_WIDTH, MLSTM_WIDTH, MLSTM_WIDTH, GATE_PAD)] * 2
        + [pltpu.VMEM((n_seq, 2 * MLSTM_HEADS, SUBLANES + CHUNK, LANES), F32),
           pltpu.VMEM((n_seq, 2 * MLSTM_HEADS, CHUNK, LANES), F32)]
        + [pltpu.VMEM((n_seq,) + sh, F32) for sh in state_shapes]
        + [pltpu.VMEM((n_seq,) + state_shapes[0], F32)],
        compiler_params=_cparams(("arbitrary",)),
        name="inproj_mlstm",
    )(x3d, g1, w_main_bf, w_gate_bf, bg_pad, *tables, w_conv, b_conv, mh_g)


def _outmlp_kernel(x_ref, att_ref, hb_ref, wo_ref, g2_ref, wu_ref, wd_ref, gf_ref, y_ref):
    mix = (jnp.dot(att_ref[...].astype(BF16), wo_ref[0:ATT_WIDTH, :], preferred_element_type=F32)
           + jnp.dot(hb_ref[...].astype(BF16), wo_ref[ATT_WIDTH:D_MODEL, :], preferred_element_type=F32))
    x1 = x_ref[...] + mix
    h2 = _rms(x1, g2_ref[...]).astype(BF16)
    acc = x1
    for c in range(FF_PARTS):
        u = jnp.dot(h2, wu_ref[c], preferred_element_type=F32)
        r = jnp.maximum(u, 0.0)
        acc = acc + jnp.dot((r * r).astype(BF16), wd_ref[c], preferred_element_type=F32)
    y_ref[...] = _rms(acc, gf_ref[...])


def _outmlp(x2d, att2d, hb2d, wo_bf, g2, wu_parts, wd_parts, gf, tm):
    n = x2d.shape[0]
    ffc = D_FF // FF_PARTS
    row = lambda w: pl.BlockSpec((tm, w), lambda i: (i, 0))
    return pl.pallas_call(
        _outmlp_kernel,
        grid=(n // tm,),
        in_specs=[row(D_MODEL), row(ATT_WIDTH), row(MLSTM_WIDTH),
                  _const_spec((D_MODEL, D_MODEL)), _const_spec((1, D_MODEL)),
                  _const_spec((FF_PARTS, D_MODEL, ffc)), _const_spec((FF_PARTS, ffc, D_MODEL)),
                  _const_spec((1, D_MODEL))],
        out_specs=row(D_MODEL),
        out_shape=jax.ShapeDtypeStruct((n, D_MODEL), F32),
        compiler_params=_cparams(("parallel",)),
        name="outmlp",
    )(x2d, att2d, hb2d, wo_bf, g2, wu_parts, wd_parts, gf)


def _project(x, pos, mix_params, tm, tile_tables, window=0):
    g1, w_main_bf, w_gate_bf, bg_pad = mix_params
    b, t, _ = x.shape
    x2d = x.reshape(b * t, D_MODEL)
    tm = min(tm, b * t)
    tables = _rotary_tables(pos)
    if tile_tables:
        tables = tuple(np.tile(a, (tm // t, 1)) for a in tables)
        table_blocks, tiles_per_seq = 1, 1
    else:
        table_blocks = tiles_per_seq = t // tm
    assert window % tm == 0
    outs = _inproj(x2d, g1, w_main_bf, w_gate_bf, bg_pad, tuple(jnp.asarray(a) for a in tables), tm,
                   table_blocks, tiles_per_seq, window // tm)
    return x2d, tm, outs


def kernel(x_prompt, x_sample, cache_win_k, cache_win_v, state_conv, state_C, state_n, state_m, norm1_g, w_in, b_gate, w_conv, b_conv, mh_norm_g, w_out, norm2_g, w_up, w_down, norm_f_g):
    depth = w_in.shape[0]
    assert depth == 1, "the final norm is fused into the (single) layer's MLP kernel"
    l = 0
    n_gate = 2 * MLSTM_HEADS
    w_l = w_in[l]
    w_main_bf = w_l[:, :IN_COLS - n_gate].astype(BF16)
    w_gate_bf = jnp.pad(w_l[:, IN_COLS - n_gate:], ((0, 0), (0, GATE_PAD - n_gate))).astype(BF16)
    bg_pad = jnp.pad(b_gate[l], (0, GATE_PAD - n_gate))[None, :]
    mix_params = (norm1_g[l][None, :], w_main_bf, w_gate_bf, bg_pad)
    conv_params = (w_conv[l], b_conv[l][None, :], mh_norm_g[l][None, :])
    ffc = D_FF // FF_PARTS
    wu_parts = w_up[l].reshape(D_MODEL, FF_PARTS, ffc).transpose(1, 0, 2).astype(BF16)
    wd_parts = w_down[l].astype(BF16).reshape(FF_PARTS, ffc, D_MODEL)
    mlp_params = (w_out[l].astype(BF16), norm2_g[l][None, :], wu_parts, wd_parts, norm_f_g[None, :])

    bp, tp, _ = x_prompt.shape
    n_keep = min(DILATIONS[-1] * SUB_WINDOW, tp)
    r3 = lambda a: a.reshape(bp, tp, a.shape[-1])
    n_seq = MLSTM_PROMPT_GROUP if bp % MLSTM_PROMPT_GROUP == 0 else 1
    tm_f = ROW_TILE // n_seq
    if tp % tm_f == 0 and n_keep % tm_f == 0 and tm_f % CHUNK == 0:
        xp2d, tm_p = x_prompt.reshape(bp * tp, D_MODEL), ROW_TILE
        tables = tuple(jnp.asarray(a) for a in _rotary_tables(np.arange(tp)))
        q, k, v, kt, vt, hb_p, tail_p, pc, pn, pm = _inproj_mlstm(
            x_prompt, *mix_params, tables, *conv_params, tm_f, n_seq, n_keep)
        hb_p = hb_p.reshape(bp * tp, MLSTM_WIDTH)
    else:
        xp2d, tm_p, (q, k, v, raw, vb, ob, gates, kt, vt) = _project(x_prompt, np.arange(tp), mix_params, ROW_TILE,
                                                                   False, window=n_keep)
        hb_p, tail_p, pc, pn, pm = _mlstm(r3(raw), r3(vb), r3(ob), r3(gates), *conv_params,
                                          CHUNK if tp % CHUNK == 0 else tp,
                                          MLSTM_PROMPT_GROUP if bp % MLSTM_PROMPT_GROUP == 0 else 1)
        hb_p = hb_p.reshape(bp * tp, MLSTM_WIDTH)
    att_p = _attn_prompt(r3(q), r3(k), r3(v)).reshape(bp * tp, ATT_WIDTH)
    heads = lambda a: a.reshape(bp, N_ATT_HEADS, HD, n_keep).transpose(0, 3, 1, 2)
    pk, pv = heads(kt), heads(vt)

    bs, ts, _ = x_sample.shape
    xs2d, tm_s, (q, k, v, raw, vb, ob, gates) = _project(x_sample, PAST_LEN + np.arange(ts), mix_params, ROW_TILE, True)
    r3 = lambda a: a.reshape(bs, ts, a.shape[-1])
    tail0 = jnp.pad(state_conv[l], ((0, 0), (SUBLANES - (CONV_W - 1), 0), (0, 0)))
    m0b = jnp.broadcast_to(state_m[l][:, :, None], (bs, MLSTM_HEADS, LANES))
    hb_s, tail_s, sc, sn, sm = _mlstm(r3(raw), r3(vb), r3(ob), r3(gates), *conv_params, ts,
                                      MLSTM_SAMPLE_GROUP if bs % MLSTM_SAMPLE_GROUP == 0 else 1,
                                      state=(tail0, state_C[l], state_n[l], m0b))
    hb_s = hb_s.reshape(bs * ts, MLSTM_WIDTH)

    kc_t = cache_win_k[l].transpose(0, 2, 3, 1)
    vc_t = cache_win_v[l].transpose(0, 2, 3, 1)
    if _attn_mlp_fits(bs, bp * tp):
        att_s, nk_t, nv_t, y_p = _attn_mlp(q, k, v, kc_t, vc_t, ts, xp2d, att_p, hb_p, *mlp_params)
    else:
        att_s, nk_t, nv_t = _attn_sample(q, k, v, kc_t, vc_t, ts)
        y_p = _outmlp(xp2d, att_p, hb_p, *mlp_params, tm_p)
    sk, sv = nk_t.transpose(0, 3, 1, 2), nv_t.transpose(0, 3, 1, 2)
    y_s = _outmlp(xs2d, att_s, hb_s, *mlp_params, tm_s)

    first_tail = SUBLANES - (CONV_W - 1)
    outs = (y_p.reshape(bp, tp, D_MODEL), y_s.reshape(bs, ts, D_MODEL),
            pk, pv, tail_p[:, first_tail:, :], pc, pn, pm[:, :, 0],
            sk, sv, tail_s[:, first_tail:, :], sc, sn, sm[:, :, 0])
    return outs[:2] + tuple(o[None] for o in outs[2:])
```

```python
import functools

import jax
import jax.numpy as jnp
import numpy as np
from jax import lax
from jax.experimental import pallas as pl
from jax.experimental.pallas import tpu as pltpu

D_MODEL = 1024
HD = 64
N_ATT_HEADS = 8
ATT_WIDTH = N_ATT_HEADS * HD
MLSTM_HEADS = 4
MLSTM_WIDTH = D_MODEL - ATT_WIDTH
MLSTM_HD = MLSTM_WIDTH // MLSTM_HEADS
ROT_DIM = HD // 4
ROT_HALF = ROT_DIM // 2
ROPE_THETA = 500000.0
DILATIONS = (1, 4, 16)
SUB_WINDOW = 128
ATT_BLOCK = 128
ROW_TILE = 512
ATT_UNROLL = 32
MLSTM_SAMPLE_GROUP = 8
MLSTM_PROMPT_GROUP = 2
FF_PARTS = 4
ATT_HEAD_SPLIT = 2
WINDOW_RING = 3
CONV_W = 4
CHUNK = 128
D_FF = 4 * D_MODEL
EPS = 1e-6
PAST_LEN = 8192
IN_SIZES = (ATT_WIDTH, ATT_WIDTH, ATT_WIDTH, 2 * MLSTM_WIDTH, MLSTM_WIDTH, MLSTM_WIDTH, 2 * MLSTM_HEADS)
IN_COLS = sum(IN_SIZES)

LANES = 128
SUBLANES = 8
GATE_PAD = LANES
VMEM_LIMIT = 56 * 1024 * 1024

F32 = jnp.float32
BF16 = jnp.bfloat16
NEG_INF = float("-inf")
LOG2E = 1.4426950408889634


def _cparams(sem):
    return pltpu.CompilerParams(dimension_semantics=sem, vmem_limit_bytes=VMEM_LIMIT)


def _const_spec(shape):
    nd = len(shape)
    return pl.BlockSpec(shape, lambda *_: (0,) * nd, pipeline_mode=pl.Buffered(1))


def _rms(x, g):
    return x * lax.rsqrt(jnp.mean(x * x, axis=-1, keepdims=True) + EPS) * g


def _inproj_kernel(x_ref, g_ref, w_ref, wg_ref, bg_ref, cos_ref, sa_ref, sb_ref,
                   q_ref, k_ref, v_ref, raw_ref, vb_ref, ob_ref, gate_ref, *win_refs,
                   tiles_per_seq, window_tiles):
    stacked = len(x_ref.shape) == 3
    n_seq = x_ref.shape[0] if stacked else 1
    tm = x_ref.shape[-2]
    flat = lambda a: a.reshape(n_seq * tm, a.shape[-1]) if stacked else a
    unflat = lambda a: a.reshape(n_seq, tm, a.shape[-1]) if stacked else a
    per_seq = lambda t: jnp.concatenate([t] * n_seq, axis=0) if stacked else t
    h = _rms(flat(x_ref[...]), g_ref[...]).astype(BF16)

    def proj(lo, width):
        return jnp.dot(h, w_ref[:, lo:lo + width], preferred_element_type=F32)

    cos, sa, sb = (per_seq(r[...]) for r in (cos_ref, sa_ref, sb_ref))

    def rotary_store(dst, y, scale):
        for c in range(ATT_WIDTH // LANES):
            yc = y[:, c * LANES:(c + 1) * LANES]
            up = pltpu.roll(yc, LANES - ROT_HALF, 1)
            dn = pltpu.roll(yc, ROT_HALF, 1)
            r = yc * cos + up * sa + dn * sb
            dst[..., c * LANES:(c + 1) * LANES] = unflat(r * scale if scale != 1.0 else r)

    off = 0
    rotary_store(q_ref, proj(off, ATT_WIDTH), HD ** -0.5)
    off += ATT_WIDTH
    rotary_store(k_ref, proj(off, ATT_WIDTH), 1.0)
    off += ATT_WIDTH
    v_ref[...] = unflat(proj(off, ATT_WIDTH))
    off += ATT_WIDTH
    raw_ref[...] = unflat(proj(off, 2 * MLSTM_WIDTH))
    off += 2 * MLSTM_WIDTH
    vb_ref[...] = unflat(proj(off, MLSTM_WIDTH))
    off += MLSTM_WIDTH
    ob_ref[...] = unflat(proj(off, MLSTM_WIDTH))
    gate_ref[...] = unflat(jnp.dot(h, wg_ref[...], preferred_element_type=F32) + bg_ref[...])

    if window_tiles:
        kt_ref, vt_ref = win_refs
        if stacked:
            for s in range(n_seq):
                kt_ref[s] = jnp.transpose(k_ref[s])
                vt_ref[s] = jnp.transpose(v_ref[s])
        else:
            kt_ref[...] = jnp.transpose(k_ref[...])
            vt_ref[...] = jnp.transpose(v_ref[...])


def _inproj(x2d, g1, w_main_bf, w_gate_bf, bg_pad, tables, tm, table_blocks, tiles_per_seq=1, window_tiles=0):
    n = x2d.shape[0]
    grid = (n // tm,)
    row = lambda i: (i, 0)
    tab = lambda i: (i % table_blocks, 0)
    widths = (ATT_WIDTH, ATT_WIDTH, ATT_WIDTH, 2 * MLSTM_WIDTH, MLSTM_WIDTH, MLSTM_WIDTH, GATE_PAD)
    out_specs = [pl.BlockSpec((tm, w), row) for w in widths]
    out_shape = [jax.ShapeDtypeStruct((n, w), F32) for w in widths]
    if window_tiles:
        first = tiles_per_seq - window_tiles
        win = pl.BlockSpec((None, ATT_WIDTH, tm),
                           lambda i: (i // tiles_per_seq, 0, jnp.maximum(i % tiles_per_seq - first, 0)))
        out_specs += [win, win]
        out_shape += [jax.ShapeDtypeStruct((n // (tm * tiles_per_seq), ATT_WIDTH, window_tiles * tm), F32)] * 2
    return pl.pallas_call(
        functools.partial(_inproj_kernel, tiles_per_seq=tiles_per_seq, window_tiles=window_tiles),
        grid=grid,
        in_specs=[
            pl.BlockSpec((tm, D_MODEL), row),
            _const_spec((1, D_MODEL)),
            _const_spec((D_MODEL, IN_COLS - 2 * MLSTM_HEADS)),
            _const_spec((D_MODEL, GATE_PAD)),
            _const_spec((1, GATE_PAD)),
            pl.BlockSpec((tm, LANES), tab),
            pl.BlockSpec((tm, LANES), tab),
            pl.BlockSpec((tm, LANES), tab),
        ],
        out_specs=out_specs,
        out_shape=out_shape,
        compiler_params=_cparams(("arbitrary",)),
        name="inproj",
    )(x2d, g1, w_main_bf, w_gate_bf, bg_pad, *tables)


def _rotary_tables(pos):
    half = np.arange(ROT_HALF, dtype=np.float64)
    ang = np.asarray(pos, np.float64)[:, None] * (ROPE_THETA ** (-half / ROT_HALF))[None, :]
    cos, sin = np.cos(ang), np.sin(ang)
    p = ang.shape[0]
    ones = np.ones((p, HD - ROT_DIM))
    zeros = np.zeros((p, HD - ROT_DIM))
    z8 = np.zeros((p, ROT_HALF))
    cos_h = np.concatenate([cos, cos, ones], axis=1)
    sa_h = np.concatenate([-sin, z8, zeros], axis=1)
    sb_h = np.concatenate([z8, sin, zeros], axis=1)
    two = lambda t: np.concatenate([t, t], axis=1).astype(np.float32)
    return two(cos_h), two(sa_h), two(sb_h)


def _attn_prompt_kernel(bias_ref, q_ref, k_ref, v_ref, o_ref,
                        q4_ref, k4_ref, v4_ref, num1_ref, m1_ref, den1_ref, num4_ref, m4_ref, den4_ref,
                        kb_ref, vb_ref, k4b_ref, v4b_ref, *, seq):
    d_mid, d_out = DILATIONS[1], DILATIONS[2]
    d_in = d_out // d_mid
    lane = lax.broadcasted_iota(jnp.int32, (ATT_BLOCK, LANES), 1)
    head0 = lane < HD
    nblk = seq // ATT_BLOCK
    sub_blocks = nblk // d_mid
    ones_cols = jnp.ones((2 * ATT_BLOCK, LANES), BF16)

    def scores(q, kk, first):
        q = q * LOG2E
        q2 = jnp.concatenate([jnp.where(head0, q, 0.0), jnp.where(head0, 0.0, q)], axis=0).astype(BF16)
        s = lax.dot_general(q2, kk.astype(BF16), (((1,), (1,)), ((), ())),
                            preferred_element_type=F32)
        return s + bias_ref[first]

    def weighted_values(p, vv):
        o2 = jnp.dot(p.astype(BF16), jnp.concatenate([vv.astype(BF16), ones_cols], axis=1),
                     preferred_element_type=F32)
        o = jnp.where(head0, o2[:ATT_BLOCK, :LANES], o2[ATT_BLOCK:, :LANES])
        den = jnp.where(head0, o2[:ATT_BLOCK, LANES:], o2[ATT_BLOCK:, LANES:])
        return o, den

    def aligned(start):
        return pl.ds(pl.multiple_of(start, ATT_BLOCK), ATT_BLOCK)

    def split(i, carry):
        r, c = i // sub_blocks, i % sub_blocks
        src = pl.ds(r + c * ATT_BLOCK * d_mid, ATT_BLOCK, stride=d_mid)
        dst = aligned(i * ATT_BLOCK)
        q4_ref[dst, :] = q_ref[src, :]
        k4, v4 = k_ref[src, :], v_ref[src, :]
        k4_ref[dst, :] = k4
        v4_ref[dst, :] = v4
        k4b_ref[dst, :] = k4.astype(BF16)
        v4b_ref[dst, :] = v4.astype(BF16)
        kb_ref[dst, :] = k_ref[dst, :].astype(BF16)
        vb_ref[dst, :] = v_ref[dst, :].astype(BF16)
        return carry

    lax.fori_loop(0, nblk, split, 0)

    def run_group(srcs, blocks_per_class, rows_of, merge, unroll=ATT_UNROLL):
        qs, ks, vs = srcs

        def body(i, carry):
            blocks = []
            for u in range(unroll):
                idx = i * unroll + u
                n = idx % blocks_per_class
                rows = rows_of(idx, n)
                prows = rows_of(idx - jnp.minimum(n, 1), jnp.maximum(n - 1, 0))
                first = jnp.where(n == 0, 1, 0)
                kk = jnp.concatenate([ks[prows, :], ks[rows, :]], axis=0)
                blocks.append([rows, prows, scores(qs[rows, :], kk, first)])
            for blk in blocks:
                s = blk[2]
                mx = jnp.max(s, axis=1, keepdims=True)
                blk[2] = jnp.exp2(s - mx)
                blk.append(jnp.where(head0, mx[:ATT_BLOCK], mx[ATT_BLOCK:]))
            for blk in blocks:
                rows, prows, p, _ = blk
                vv = jnp.concatenate([vs[prows, :], vs[rows, :]], axis=0)
                blk[2:3] = weighted_values(p, vv)
            for rows, _, o, den, mxb in blocks:
                merge(rows, o, mxb, den)
            return carry

        lax.fori_loop(0, nblk // unroll, body, 0)

    def init_state(num_ref, m_ref, den_ref):
        def merge(rows, o, mxb, den):
            num_ref[rows, :] = o
            m_ref[rows, :] = mxb
            den_ref[rows, :] = den
        return merge

    def merge4(rows, o, mxb, den):
        m_old = m4_ref[rows, :]
        m_new = jnp.maximum(m_old, mxb)
        a = jnp.exp2(m_old - m_new)
        b = jnp.exp2(mxb - m_new)
        num4_ref[rows, :] = num4_ref[rows, :] * a + o * b
        den4_ref[rows, :] = den4_ref[rows, :] * a + den * b
        m4_ref[rows, :] = m_new

    run_group((q_ref, kb_ref, vb_ref), nblk, lambda idx, n: aligned(idx * ATT_BLOCK),
              init_state(num1_ref, m1_ref, den1_ref))
    blocks16 = nblk // d_out

    def rows16(idx, n):
        cls = idx // blocks16
        r, j = cls // d_in, cls % d_in
        return pl.ds(r * (sub_blocks * ATT_BLOCK) + j + n * ATT_BLOCK * d_in, ATT_BLOCK, stride=d_in)

    run_group((q4_ref, k4_ref, v4_ref), blocks16, rows16, init_state(num4_ref, m4_ref, den4_ref))
    run_group((q4_ref, k4b_ref, v4b_ref), sub_blocks, lambda idx, n: aligned(idx * ATT_BLOCK), merge4)

    def finish(i, carry):
        r, c = i // sub_blocks, i % sub_blocks
        tok = pl.ds(r + c * ATT_BLOCK * d_mid, ATT_BLOCK, stride=d_mid)
        res = aligned(i * ATT_BLOCK)
        m_a, m_b = m4_ref[res, :], m1_ref[tok, :]
        m_new = jnp.maximum(m_a, m_b)
        wa = jnp.exp2(m_a - m_new)
        wb = jnp.exp2(m_b - m_new)
        num = num4_ref[res, :] * wa + num1_ref[tok, :] * wb
        den = den4_ref[res, :] * wa + den1_ref[tok, :] * wb
        o_ref[tok, :] = num / den
        return carry

    lax.fori_loop(0, nblk, finish, 0, unroll=4)


def _band_bias():
    row = np.arange(2 * ATT_BLOCK)[:, None] % ATT_BLOCK
    ki = np.arange(2 * ATT_BLOCK)[None, :] - ATT_BLOCK
    rel = row - ki
    valid = (rel >= 0) & (rel <= SUB_WINDOW)
    b0 = np.where(valid, 0.0, NEG_INF).astype(np.float32)
    b1 = np.where(valid & (ki >= 0), 0.0, NEG_INF).astype(np.float32)
    return jnp.asarray(np.stack([b0, b1]))


def _attn_prompt(q, k, v):
    b, s, _ = q.shape
    assert DILATIONS[0] == 1 and DILATIONS[2] % DILATIONS[1] == 0
    assert s % (ATT_BLOCK * DILATIONS[-1]) == 0 and (s // ATT_BLOCK) % ATT_UNROLL == 0
    blk = pl.BlockSpec((None, s, LANES), lambda i, j: (i, 0, j))
    return pl.pallas_call(
        functools.partial(_attn_prompt_kernel, seq=s),
        grid=(b, ATT_WIDTH // LANES),
        in_specs=[_const_spec((2, 2 * ATT_BLOCK, 2 * ATT_BLOCK)), blk, blk, blk],
        out_specs=blk,
        out_shape=jax.ShapeDtypeStruct((b, s, ATT_WIDTH), F32),
        scratch_shapes=[pltpu.VMEM((s, LANES), F32)] * 9 + [pltpu.VMEM((s, LANES), BF16)] * 4,
        compiler_params=_cparams(("parallel", "parallel")),
        name="attn_prompt",
    )(_band_bias(), q, k, v)


def _attn_sample_body(mw_ref, mn_ref, q_ref, kn_ref, vn_ref, kc_ref, vc_ref,
                      att_ref, nk_ref, nv_ref, *, win, t_new, heads):
    mult_w = mw_ref[...]
    mult_n = mn_ref[...]
    lane = lax.broadcasted_iota(jnp.int32, (HD, LANES), 1)
    keep = lane < LANES - t_new
    top = jnp.zeros((LANES - t_new, heads * HD), F32)
    knt_all = jnp.transpose(jnp.concatenate([top, kn_ref[...]], axis=0))
    vnt_all = jnp.transpose(jnp.concatenate([top, vn_ref[...]], axis=0))
    head_rows = lambda a, h: a[h * HD:(h + 1) * HD, :]
    for h in range(heads):
        for src_ref, new_all, dst in ((kc_ref, knt_all, nk_ref), (vc_ref, vnt_all, nv_ref)):
            rolled = pltpu.roll(src_ref[h], win - t_new, 1)
            dst[h, :, 0:win - LANES] = rolled[:, 0:win - LANES]
            dst[h, :, win - LANES:win] = jnp.where(keep, rolled[:, win - LANES:win], head_rows(new_all, h))
    sc = []
    for h in range(heads):
        qh = q_ref[:, h * HD:(h + 1) * HD].astype(BF16)
        s_w = jnp.dot(qh, kc_ref[h].astype(BF16), preferred_element_type=F32)
        s_n = jnp.dot(qh, head_rows(knt_all, h).astype(BF16), preferred_element_type=F32)
        sc.append((jnp.where(mult_w > 0, s_w, NEG_INF), jnp.where(mult_n > 0, s_n, NEG_INF)))
    pr = []
    for s_w, s_n in sc:
        mx = jnp.maximum(jnp.max(s_w, axis=1, keepdims=True), jnp.max(s_n, axis=1, keepdims=True))
        p_w = mult_w * jnp.exp(s_w - mx)
        p_n = mult_n * jnp.exp(s_n - mx)
        den = jnp.sum(p_w, axis=1, keepdims=True) + jnp.sum(p_n, axis=1, keepdims=True)
        pr.append((p_w, p_n, den))
    for h, (p_w, p_n, den) in enumerate(pr):
        o = lax.dot_general(p_w.astype(BF16), vc_ref[h].astype(BF16), (((1,), (1,)), ((), ())),
                            preferred_element_type=F32)
        o = o + lax.dot_general(p_n.astype(BF16), head_rows(vnt_all, h).astype(BF16),
                                (((1,), (1,)), ((), ())), preferred_element_type=F32)
        att_ref[:, h * HD:(h + 1) * HD] = o / den


def _attn_sample_kernel(*refs, win, t_new):
    _attn_sample_body(*refs, win=win, t_new=t_new, heads=N_ATT_HEADS)


def _attn_mlp_kernel(mw_ref, mn_ref, q_ref, kn_ref, vn_ref, kc_ref, vc_ref,
                     x_ref, attp_ref, hbp_ref, wo_ref, g2_ref, wu_ref, wd_ref, gf_ref,
                     att_ref, nk_ref, nv_ref, y_ref, h2_sc, acc_sc, kbuf, vbuf, sem, *, win, t_new, heads):
    step = pl.program_id(0)
    n_steps = pl.num_programs(0)
    c = step % FF_PARTS

    def window_copies(t):
        slot = t % WINDOW_RING
        src = lambda ref: ref.at[t // ATT_HEAD_SPLIT, pl.ds((t % ATT_HEAD_SPLIT) * heads, heads)]
        return (pltpu.make_async_copy(src(kc_ref), kbuf.at[slot], sem.at[0, slot]),
                pltpu.make_async_copy(src(vc_ref), vbuf.at[slot], sem.at[1, slot]))

    @pl.when(step == 0)
    def _():
        for t in range(WINDOW_RING - 1):
            for cp in window_copies(t):
                cp.start()

    @pl.when(step + WINDOW_RING - 1 < n_steps)
    def _():
        for cp in window_copies(step + WINDOW_RING - 1):
            cp.start()

    for cp in window_copies(step):
        cp.wait()
    kc_ref, vc_ref = kbuf.at[step % WINDOW_RING], vbuf.at[step % WINDOW_RING]

    @pl.when(c == 0)
    def _():
        mix = (jnp.dot(attp_ref[...].astype(BF16), wo_ref[0:ATT_WIDTH, :], preferred_element_type=F32)
               + jnp.dot(hbp_ref[...].astype(BF16), wo_ref[ATT_WIDTH:D_MODEL, :], preferred_element_type=F32))
        x1 = x_ref[...] + mix
        h2_sc[...] = _rms(x1, g2_ref[...]).astype(BF16)
        acc_sc[...] = x1

    u = jnp.dot(h2_sc[...], wu_ref[c], preferred_element_type=F32)
    r = jnp.maximum(u, 0.0)
    acc_sc[...] += jnp.dot((r * r).astype(BF16), wd_ref[c], preferred_element_type=F32)
    _attn_sample_body(mw_ref, mn_ref, q_ref, kn_ref, vn_ref, kc_ref, vc_ref, att_ref, nk_ref, nv_ref,
                      win=win, t_new=t_new, heads=heads)

    @pl.when(c == FF_PARTS - 1)
    def _():
        y_ref[...] = _rms(acc_sc[...], gf_ref[...])


def _sample_multiplicity(win, t_new):
    t = np.arange(t_new)[:, None]
    idx = np.arange(win + t_new)[None, :]
    back = win + t - idx
    mult = np.zeros((t_new, win + t_new), np.float32)
    for dil in DILATIONS:
        mult += ((back >= 0) & (back % dil == 0) & (back // dil <= SUB_WINDOW)).astype(np.float32)
    mw = mult[:, :win]
    mn = np.zeros((t_new, LANES), np.float32)
    mn[:, LANES - t_new:] = mult[:, win:]
    return jnp.asarray(mw), jnp.asarray(mn)


def _attn_sample(q, k_new, v_new, kc_t, vc_t, t_new):
    b, nh, _, win = kc_t.shape
    assert win == DILATIONS[-1] * SUB_WINDOW and t_new == SUBLANES and q.shape[0] == b * t_new
    mw, mn = _sample_multiplicity(win, t_new)
    rows = pl.BlockSpec((t_new, ATT_WIDTH), lambda i: (i, 0))
    per_b = lambda *shape: pl.BlockSpec((None,) + shape, lambda i: (i,) + (0,) * len(shape))
    return pl.pallas_call(
        functools.partial(_attn_sample_kernel, win=win, t_new=t_new),
        grid=(b,),
        in_specs=[
            _const_spec((t_new, win)), _const_spec((t_new, LANES)),
            rows, rows, rows, per_b(nh, HD, win), per_b(nh, HD, win),
        ],
        out_specs=[rows, per_b(nh, HD, win), per_b(nh, HD, win)],
        out_shape=[
            jax.ShapeDtypeStruct((b * t_new, ATT_WIDTH), F32),
            jax.ShapeDtypeStruct((b, nh, HD, win), F32),
            jax.ShapeDtypeStruct((b, nh, HD, win), F32),
        ],
        compiler_params=_cparams(("parallel",)),
        name="attn_sample",
    )(mw, mn, q, k_new, v_new, kc_t, vc_t)


def _attn_mlp_fits(b_s, n_p):
    steps = b_s * ATT_HEAD_SPLIT
    if steps % FF_PARTS:
        return False
    tiles = steps // FF_PARTS
    return n_p % tiles == 0 and (n_p // tiles) % SUBLANES == 0 and n_p // tiles <= ROW_TILE


def _attn_mlp(q, k_new, v_new, kc_t, vc_t, t_new, x2d, attp, hbp, wo_bf, g2, wu_parts, wd_parts, gf):
    b, nh, _, win = kc_t.shape
    n_p = x2d.shape[0]
    assert win == DILATIONS[-1] * SUB_WINDOW and t_new == SUBLANES and q.shape[0] == b * t_new
    heads = nh // ATT_HEAD_SPLIT
    steps = b * ATT_HEAD_SPLIT
    tm = n_p // (steps // FF_PARTS)
    ffc = D_FF // FF_PARTS
    mw, mn = _sample_multiplicity(win, t_new)
    rows = pl.BlockSpec((t_new, heads * HD), lambda i: (i // ATT_HEAD_SPLIT, i % ATT_HEAD_SPLIT))
    win_blk = pl.BlockSpec((None, heads, HD, win), lambda i: (i // ATT_HEAD_SPLIT, i % ATT_HEAD_SPLIT, 0, 0))
    tile = lambda w: pl.BlockSpec((tm, w), lambda i: (i // FF_PARTS, 0))
    return pl.pallas_call(
        functools.partial(_attn_mlp_kernel, win=win, t_new=t_new, heads=heads),
        grid=(steps,),
        in_specs=[
            _const_spec((t_new, win)), _const_spec((t_new, LANES)),
            rows, rows, rows, pl.BlockSpec(memory_space=pl.ANY), pl.BlockSpec(memory_space=pl.ANY),
            tile(D_MODEL), tile(ATT_WIDTH), tile(MLSTM_WIDTH),
            _const_spec((D_MODEL, D_MODEL)), _const_spec((1, D_MODEL)),
            _const_spec((FF_PARTS, D_MODEL, ffc)), _const_spec((FF_PARTS, ffc, D_MODEL)),
            _const_spec((1, D_MODEL)),
        ],
        out_specs=[rows, win_blk, win_blk, tile(D_MODEL)],
        out_shape=[
            jax.ShapeDtypeStruct((b * t_new, ATT_WIDTH), F32),
            jax.ShapeDtypeStruct((b, nh, HD, win), F32),
            jax.ShapeDtypeStruct((b, nh, HD, win), F32),
            jax.ShapeDtypeStruct((n_p, D_MODEL), F32),
        ],
        scratch_shapes=[pltpu.VMEM((tm, D_MODEL), BF16), pltpu.VMEM((tm, D_MODEL), F32),
                        pltpu.VMEM((WINDOW_RING, heads, HD, win), F32), pltpu.VMEM((WINDOW_RING, heads, HD, win), F32),
                        pltpu.SemaphoreType.DMA((2, WINDOW_RING))],
        compiler_params=_cparams(("arbitrary",)),
        name="attn_mlp",
    )(mw, mn, q, k_new, v_new, kc_t, vc_t, x2d, attp, hbp, wo_bf, g2, wu_parts, wd_parts, gf)


def _split_dot(tri, x):
    hi = x.astype(BF16)
    lo = (x - hi.astype(F32)).astype(BF16)
    return (jnp.dot(tri, hi, preferred_element_type=F32)
            + jnp.dot(tri, lo, preferred_element_type=F32))


def _split_dot_rows(x, tri):
    hi = x.astype(BF16)
    lo = (x - hi.astype(F32)).astype(BF16)
    return (jnp.dot(hi, tri, preferred_element_type=F32)
            + jnp.dot(lo, tri, preferred_element_type=F32))


def _log_sigmoid(x):
    return jnp.minimum(x, 0.0) - jnp.log1p(jnp.exp(-jnp.abs(x)))


def _mlstm_init(xp_ref, c_sc, n_sc, m_sc, nm_sc, group, state_refs=None):
    if state_refs is not None:
        tail_ref, c0_ref, n0_ref, m0_ref = state_refs
        for c in range(2 * MLSTM_HEADS):
            xp_ref[:, c, 0:SUBLANES, :] = tail_ref[:, :, c * LANES:(c + 1) * LANES]
        c_sc[...] = c0_ref[...]
        n_sc[...] = n0_ref[...]
        m_sc[...] = m0_ref[...]
    else:
        xp_ref[:, :, 0:SUBLANES, :] = jnp.zeros((group, 2 * MLSTM_HEADS, SUBLANES, LANES), F32)
        c_sc[...] = jnp.zeros_like(c_sc)
        n_sc[...] = jnp.zeros_like(n_sc)
        m_sc[...] = jnp.zeros_like(m_sc)
        nm_sc[...] = jnp.zeros_like(nm_sc)


def _mlstm_emit(c_out_ref, n_out_ref, m_out_ref, c_sc, n_sc, m_sc, nm_sc, group, mxu_sums):
    if mxu_sums:
        for g in range(group):
            for h in range(MLSTM_HEADS):
                n_sc[g, h:h + 1, :] = jnp.transpose(nm_sc[g, h])[0:1, :]
    c_out_ref[...] = c_sc[...]
    n_out_ref[...] = n_sc[...]
    m_out_ref[...] = m_sc[...]


def _mlstm_chunk(raws, vbs, obs, gate_list, hbs, tails, wc_ref, bc_ref, g_ref,
                 xp_ref, act_ref, c_sc, n_sc, m_sc, nm_sc, *, chunk, mxu_sums):
    L = chunk
    group = len(raws)
    r_i = lax.broadcasted_iota(jnp.int32, (L, L), 0)
    c_i = lax.broadcasted_iota(jnp.int32, (L, L), 1)
    causal = r_i >= c_i
    tri_l = jnp.where(causal, 1.0, 0.0).astype(BF16)
    tri_u = jnp.where(r_i <= c_i, 1.0, 0.0).astype(BF16)
    first_tap = SUBLANES - (CONV_W - 1)
    bc = bc_ref[...]
    taps = [wc_ref[w:w + 1, :] for w in range(CONV_W)]
    ones_sq = jnp.ones((LANES, LANES), BF16)

    per_batch = []
    for g in range(group):
        tails[g][...] = raws[g][L - SUBLANES:L, :]
        for c in range(2 * MLSTM_HEADS):
            cl = slice(c * LANES, (c + 1) * LANES)
            raw = raws[g][:, cl]
            xp_ref[g, c, SUBLANES:SUBLANES + L, :] = raw
            if L % (SUBLANES * SUBLANES) == 0:
                n_rows = L // SUBLANES
                slabs = [xp_ref[g, c, pl.ds(first_tap + j, n_rows, stride=SUBLANES), :]
                         for j in range(SUBLANES + CONV_W - 1)]
                for s in range(SUBLANES):
                    conv = bc[:, cl]
                    for w in range(CONV_W):
                        conv = conv + slabs[s + w] * taps[w][:, cl]
                    act_ref[g, c, pl.ds(s, n_rows, stride=SUBLANES), :] = conv * jax.nn.sigmoid(conv)
            else:
                conv = bc[:, cl]
                for w in range(CONV_W):
                    conv = conv + xp_ref[g, c, first_tap + w:first_tap + w + L, :] * taps[w][:, cl]
                act_ref[g, c] = conv * jax.nn.sigmoid(conv)
            xp_ref[g, c, 0:SUBLANES, :] = raw[L - SUBLANES:L, :]

        gates = gate_list[g][...]
        lf = _log_sigmoid(gates)
        b_col = _split_dot(tri_l, lf)
        if L == LANES:
            gates_sq = gates
        else:
            gates_sq = jnp.concatenate([gates, jnp.zeros((LANES - L, LANES), F32)], axis=0)
        gates_t = jnp.transpose(gates_sq)[0:SUBLANES, 0:L]
        lf_t = _log_sigmoid(gates_t)
        b_row = _split_dot_rows(lf_t, tri_u)

        per_batch.append((gates, b_col, gates_t, b_row))

    items = [(g, h) for g in range(group) for h in range(MLSTM_HEADS)]
    st = {}
    for g, h in items:
        gates, b_col, gates_t, b_row = per_batch[g]
        ig_c = gates[:, h:h + 1]
        b_c = b_col[:, MLSTM_HEADS + h:MLSTM_HEADS + h + 1]
        ig_r = gates_t[h:h + 1, :]
        b_r = b_row[MLSTM_HEADS + h:MLSTM_HEADS + h + 1, :]
        if mxu_sums:
            b_c_wide = jnp.broadcast_to(b_c, (L, LANES))
            a_c = b_c_wide + m_sc[g, h:h + 1, 0:1]
            dm = jnp.where(causal, b_c_wide - b_r + ig_r, NEG_INF)
            mt = jnp.maximum(a_c, jnp.broadcast_to(jnp.max(dm, axis=1, keepdims=True), (L, LANES)))
        else:
            a_c = b_c + m_sc[g, h:h + 1, 0:1]
            dm = jnp.where(causal, b_c - b_r + ig_r, NEG_INF)
            mt = jnp.maximum(a_c, jnp.max(dm, axis=1, keepdims=True))
        st[g, h] = dict(ig_c=ig_c, b_c=b_c, a_c=a_c, dm=dm, mt=mt)
    for g, h in items:
        d = st[g, h]
        q = act_ref[g, h]
        k = act_ref[g, MLSTM_HEADS + h] * (MLSTM_HD ** -0.5)
        v = vbs[g][:, h * MLSTM_HD:(h + 1) * MLSTM_HD]
        qb, kb = q.astype(BF16), k.astype(BF16)
        w_inter = jnp.exp(d["a_c"] - d["mt"])
        wm = jnp.exp(d["dm"] - d["mt"])
        sc = lax.dot_general(qb, kb, (((1,), (1,)), ((), ())), preferred_element_type=F32) * wm
        d.update(q=q, k=k, v=v, qb=qb, vb=v.astype(BF16), w_inter=w_inter, sc=sc)
        del d["dm"]
    for g, h in items:
        d = st[g, h]
        c_old = c_sc[g, h]
        if mxu_sums:
            nm_old = nm_sc[g, h]
            v_ones = jnp.concatenate([d["vb"], ones_sq], axis=1)
            state = jnp.concatenate([c_old, nm_old], axis=1).astype(BF16)
            inter = jnp.dot(d["qb"], state, preferred_element_type=F32)
            intra = jnp.dot(d["sc"].astype(BF16), v_ones, preferred_element_type=F32)
            num = d["w_inter"] * inter[:, :LANES] + intra[:, :LANES]
            den = d["w_inter"] * inter[:, LANES:] + intra[:, LANES:]
            d.update(v_ones=v_ones, nm_old=nm_old)
        else:
            n_old = n_sc[g, h:h + 1, :]
            num = (d["w_inter"] * jnp.dot(d["qb"], c_old.astype(BF16), preferred_element_type=F32)
                   + jnp.dot(d["sc"].astype(BF16), d["vb"], preferred_element_type=F32))
            den = (d["w_inter"] * jnp.sum(d["q"] * n_old, axis=1, keepdims=True)
                   + jnp.sum(d["sc"], axis=1, keepdims=True))
            d.update(n_old=n_old)
        d.update(c_old=c_old, hh=num / jnp.maximum(jnp.abs(den), jnp.exp(-d["mt"])))
        del d["sc"]
    for g, h in items:
        d = st[g, h]
        mt, a_c, b_c = d["mt"], d["a_c"], d["b_c"]
        m_last = mt[L - 1:L, 0:1]
        w_last = jnp.exp(a_c[L - 1:L, 0:1] - m_last)
        w_t = jnp.exp(b_c[L - 1:L, :] - b_c + d["ig_c"] - m_last)
        kw = d["k"] * w_t
        if mxu_sums:
            upd = jnp.dot(jnp.transpose(kw).astype(BF16), d["v_ones"], preferred_element_type=F32)
            c_sc[g, h] = w_last * d["c_old"] + upd[:, :LANES]
            nm_sc[g, h] = w_last * d["nm_old"] + upd[:, LANES:]
        else:
            if L == LANES:
                kw_sq, v_sq = kw, d["v"]
            else:
                pad = jnp.zeros((LANES - L, MLSTM_HD), F32)
                kw_sq = jnp.concatenate([kw, pad], axis=0)
                v_sq = jnp.concatenate([d["v"], pad], axis=0)
            c_sc[g, h] = w_last * d["c_old"] + jnp.dot(jnp.transpose(kw_sq).astype(BF16), v_sq.astype(BF16),
                                                       preferred_element_type=F32)
            n_sc[g, h:h + 1, :] = w_last * d["n_old"] + jnp.sum(kw, axis=0, keepdims=True)
        m_sc[g, h:h + 1, :] = jnp.broadcast_to(m_last, (1, LANES))
    for g, h in items:
        hh = st[g, h]["hh"]
        sl = slice(h * MLSTM_HD, (h + 1) * MLSTM_HD)
        if mxu_sums:
            mean_sq = _split_dot_rows(hh * hh, ones_sq) * (1.0 / MLSTM_HD)
        else:
            mean_sq = jnp.mean(hh * hh, axis=1, keepdims=True)
        hn = hh * lax.rsqrt(mean_sq + EPS)
        hbs[g][:, sl] = hn * g_ref[:, sl] * jax.nn.sigmoid(obs[g][:, sl])


def _mlstm_kernel(*refs, chunk, group, has_state):
    if has_state:
        (raw_ref, vb_ref, ob_ref, gate_ref, wc_ref, bc_ref, g_ref, tail_ref, c0_ref, n0_ref, m0_ref,
         hb_ref, tail_out_ref, c_out_ref, n_out_ref, m_out_ref, xp_ref, act_ref, c_sc, n_sc, m_sc) = refs
        nm_sc, state_refs = None, (tail_ref, c0_ref, n0_ref, m0_ref)
    else:
        (raw_ref, vb_ref, ob_ref, gate_ref, wc_ref, bc_ref, g_ref,
         hb_ref, tail_out_ref, c_out_ref, n_out_ref, m_out_ref, xp_ref, act_ref, c_sc, n_sc, m_sc, nm_sc) = refs
        state_refs = None
    c_idx = pl.program_id(1)
    mxu_sums = nm_sc is not None and chunk == LANES

    @pl.when(c_idx == 0)
    def _():
        _mlstm_init(xp_ref, c_sc, n_sc, m_sc, nm_sc, group, state_refs)

    per_g = lambda ref: [ref.at[g] for g in range(group)]
    _mlstm_chunk(per_g(raw_ref), per_g(vb_ref), per_g(ob_ref), per_g(gate_ref), per_g(hb_ref), per_g(tail_out_ref),
                 wc_ref, bc_ref, g_ref, xp_ref, act_ref, c_sc, n_sc, m_sc, nm_sc, chunk=chunk, mxu_sums=mxu_sums)

    @pl.when(c_idx == pl.num_programs(1) - 1)
    def _():
        _mlstm_emit(c_out_ref, n_out_ref, m_out_ref, c_sc, n_sc, m_sc, nm_sc, group, mxu_sums)


def _mlstm(raw, vb, ob, gates, w_conv, b_conv, mh_g, chunk, group, state=None):
    b, t, _ = raw.shape
    nc = t // chunk
    assert nc * chunk == t and chunk % SUBLANES == 0 and b % group == 0
    seq = lambda w: pl.BlockSpec((group, chunk, w), lambda i, j: (i, j, 0))
    per_b = lambda *shape: pl.BlockSpec((group,) + shape, lambda i, j: (i,) + (0,) * len(shape))
    in_specs = [seq(2 * MLSTM_WIDTH), seq(MLSTM_WIDTH), seq(MLSTM_WIDTH), seq(GATE_PAD),
                _const_spec((CONV_W, 2 * MLSTM_WIDTH)), _const_spec((1, 2 * MLSTM_WIDTH)),
                _const_spec((1, MLSTM_WIDTH))]
    args = [raw, vb, ob, gates, w_conv, b_conv, mh_g]
    state_shapes = [(SUBLANES, 2 * MLSTM_WIDTH), (MLSTM_HEADS, MLSTM_HD, MLSTM_HD),
                    (MLSTM_HEADS, MLSTM_HD), (MLSTM_HEADS, LANES)]
    state_specs = [per_b(*sh) for sh in state_shapes]
    if state is not None:
        in_specs += state_specs
        args += list(state)
    return pl.pallas_call(
        functools.partial(_mlstm_kernel, chunk=chunk, group=group, has_state=state is not None),
        grid=(b // group, nc),
        in_specs=in_specs,
        out_specs=[seq(MLSTM_WIDTH)] + state_specs,
        out_shape=[jax.ShapeDtypeStruct((b, t, MLSTM_WIDTH), F32)]
        + [jax.ShapeDtypeStruct((b,) + sh, F32) for sh in state_shapes],
        scratch_shapes=[
            pltpu.VMEM((group, 2 * MLSTM_HEADS, SUBLANES + chunk, LANES), F32),
            pltpu.VMEM((group, 2 * MLSTM_HEADS, chunk, LANES), F32),
        ] + [pltpu.VMEM((group,) + sh, F32) for sh in state_shapes[1:]]
        + ([] if state is not None else [pltpu.VMEM((group,) + state_shapes[1], F32)]),
        compiler_params=_cparams(("parallel", "arbitrary")),
        name="mlstm_state" if state is not None else "mlstm",
    )(*args)


def _inproj_mlstm_kernel(x_ref, g_ref, w_ref, wg_ref, bg_ref, cos_ref, sa_ref, sb_ref, wc_ref, bc_ref, mhg_ref,
                         q_ref, k_ref, v_ref, kt_ref, vt_ref, hb_ref, tail_ref, c_out_ref, n_out_ref, m_out_ref,
                         *scratch, tiles_per_seq, window_tiles):
    stage_a, stage_b = scratch[0:4], scratch[4:8]
    xp_ref, act_ref, c_sc, n_sc, m_sc, nm_sc = scratch[8:]
    i = pl.program_id(0)
    n_seq, tm = hb_ref.shape[0], hb_ref.shape[1]

    @pl.when(i == 0)
    def _():
        for st in stage_b:
            st[...] = jnp.zeros(st.shape, F32)

    prev_tile_in_seq = jnp.maximum(i - 1, 0) % tiles_per_seq

    @pl.when(prev_tile_in_seq == 0)
    def _():
        _mlstm_init(xp_ref, c_sc, n_sc, m_sc, nm_sc, n_seq)

    def step(write, read):
        _inproj_kernel(x_ref, g_ref, w_ref, wg_ref, bg_ref, cos_ref, sa_ref, sb_ref, q_ref, k_ref, v_ref,
                       *write, kt_ref, vt_ref, tiles_per_seq=tiles_per_seq, window_tiles=window_tiles)
        raw, vb, ob, gate = read
        for cc in range(tm // CHUNK):
            rows = pl.ds(cc * CHUNK, CHUNK)
            each = lambda ref: [ref.at[s_, rows, :] for s_ in range(n_seq)]
            _mlstm_chunk(each(raw), each(vb), each(ob), each(gate), each(hb_ref),
                         [tail_ref.at[s_] for s_ in range(n_seq)], wc_ref, bc_ref, mhg_ref,
                         xp_ref, act_ref, c_sc, n_sc, m_sc, nm_sc, chunk=CHUNK, mxu_sums=True)

    @pl.when(i % 2 == 0)
    def _():
        step(stage_a, stage_b)

    @pl.when(i % 2 == 1)
    def _():
        step(stage_b, stage_a)

    @pl.when(prev_tile_in_seq == tiles_per_seq - 1)
    def _():
        _mlstm_emit(c_out_ref, n_out_ref, m_out_ref, c_sc, n_sc, m_sc, nm_sc, n_seq, True)


def _inproj_mlstm(x3d, g1, w_main_bf, w_gate_bf, bg_pad, tables, w_conv, b_conv, mh_g, tm, n_seq, window):
    b, t, _ = x3d.shape
    tiles_per_seq, window_tiles = t // tm, window // tm
    n_steps = (b // n_seq) * tiles_per_seq
    assert b % n_seq == 0 and t % tm == 0 and window_tiles >= 1 and tm % CHUNK == 0 and CHUNK == LANES
    cur = lambda i: jnp.minimum(i, n_steps - 1)
    prv = lambda i: jnp.maximum(i - 1, 0)
    row = lambda w: pl.BlockSpec((n_seq, tm, w), lambda i: (cur(i) // tiles_per_seq, cur(i) % tiles_per_seq, 0))
    tab = pl.BlockSpec((tm, LANES), lambda i: (cur(i) % tiles_per_seq, 0))
    first = tiles_per_seq - window_tiles
    win = pl.BlockSpec((n_seq, ATT_WIDTH, tm),
                       lambda i: (cur(i) // tiles_per_seq, 0, jnp.maximum(cur(i) % tiles_per_seq - first, 0)))
    seq_state = lambda *shape: pl.BlockSpec((n_seq,) + shape, lambda i: (prv(i) // tiles_per_seq,) + (0,) * len(shape))
    state_shapes = [(MLSTM_HEADS, MLSTM_HD, MLSTM_HD), (MLSTM_HEADS, MLSTM_HD), (MLSTM_HEADS, LANES)]
    return pl.pallas_call(
        functools.partial(_inproj_mlstm_kernel, tiles_per_seq=tiles_per_seq, window_tiles=window_tiles),
        grid=(n_steps + 1,),
        in_specs=[
            row(D_MODEL), _const_spec((1, D_MODEL)),
            _const_spec((D_MODEL, IN_COLS - 2 * MLSTM_HEADS)), _const_spec((D_MODEL, GATE_PAD)),
            _const_spec((1, GATE_PAD)), tab, tab, tab,
            _const_spec((CONV_W, 2 * MLSTM_WIDTH)), _const_spec((1, 2 * MLSTM_WIDTH)), _const_spec((1, MLSTM_WIDTH)),
        ],
        out_specs=[row(ATT_WIDTH), row(ATT_WIDTH), row(ATT_WIDTH), win, win,
                   pl.BlockSpec((n_seq, tm, MLSTM_WIDTH),
                                lambda i: (prv(i) // tiles_per_seq, prv(i) % tiles_per_seq, 0)),
                   seq_state(SUBLANES, 2 * MLSTM_WIDTH)]
        + [seq_state(*sh) for sh in state_shapes],
        out_shape=[jax.ShapeDtypeStruct((b, t, ATT_WIDTH), F32)] * 3
        + [jax.ShapeDtypeStruct((b, ATT_WIDTH, window), F32)] * 2
        + [jax.ShapeDtypeStruct((b, t, MLSTM_WIDTH), F32), jax.ShapeDtypeStruct((b, SUBLANES, 2 * MLSTM_WIDTH), F32)]
        + [jax.ShapeDtypeStruct((b,) + sh, F32) for sh in state_shapes],
        scratch_shapes=[pltpu.VMEM((n_seq, tm, w), F32)
                        for w in (2 * MLSTM_WIDTH, MLSTM_WIDTH, MLSTM_WIDTH, GATE_PAD)] * 2
        + [pltpu.VMEM((n_seq, 2 * MLSTM_HEADS, SUBLANES + CHUNK, LANES), F32),
           pltpu.VMEM((n_seq, 2 * MLSTM_HEADS, CHUNK, LANES), F32)]
        + [pltpu.VMEM((n_seq,) + sh, F32) for sh in state_shapes]
        + [pltpu.VMEM((n_seq,) + state_shapes[0], F32)],
        compiler_params=_cparams(("arbitrary",)),
        name="inproj_mlstm",
    )(x3d, g1, w_main_bf, w_gate_bf, bg_pad, *tables, w_conv, b_conv, mh_g)


def _outmlp_kernel(x_ref, att_ref, hb_ref, wo_ref, g2_ref, wu_ref, wd_ref, gf_ref, y_ref):
    mix = (jnp.dot(att_ref[...].astype(BF16), wo_ref[0:ATT_WIDTH, :], preferred_element_type=F32)
           + jnp.dot(hb_ref[...].astype(BF16), wo_ref[ATT_WIDTH:D_MODEL, :], preferred_element_type=F32))
    x1 = x_ref[...] + mix
    h2 = _rms(x1, g2_ref[...]).astype(BF16)
    acc = x1
    for c in range(FF_PARTS):
        u = jnp.dot(h2, wu_ref[c], preferred_element_type=F32)
        r = jnp.maximum(u, 0.0)
        acc = acc + jnp.dot((r * r).astype(BF16), wd_ref[c], preferred_element_type=F32)
    y_ref[...] = _rms(acc, gf_ref[...])


def _outmlp(x2d, att2d, hb2d, wo_bf, g2, wu_parts, wd_parts, gf, tm):
    n = x2d.shape[0]
    ffc = D_FF // FF_PARTS
    row = lambda w: pl.BlockSpec((tm, w), lambda i: (i, 0))
    return pl.pallas_call(
        _outmlp_kernel,
        grid=(n // tm,),
        in_specs=[row(D_MODEL), row(ATT_WIDTH), row(MLSTM_WIDTH),
                  _const_spec((D_MODEL, D_MODEL)), _const_spec((1, D_MODEL)),
                  _const_spec((FF_PARTS, D_MODEL, ffc)), _const_spec((FF_PARTS, ffc, D_MODEL)),
                  _const_spec((1, D_MODEL))],
        out_specs=row(D_MODEL),
        out_shape=jax.ShapeDtypeStruct((n, D_MODEL), F32),
        compiler_params=_cparams(("parallel",)),
        name="outmlp",
    )(x2d, att2d, hb2d, wo_bf, g2, wu_parts, wd_parts, gf)


def _project(x, pos, mix_params, tm, tile_tables, window=0):
    g1, w_main_bf, w_gate_bf, bg_pad = mix_params
    b, t, _ = x.shape
    x2d = x.reshape(b * t, D_MODEL)
    tm = min(tm, b * t)
    tables = _rotary_tables(pos)
    if tile_tables:
        tables = tuple(np.tile(a, (tm // t, 1)) for a in tables)
        table_blocks, tiles_per_seq = 1, 1
    else:
        table_blocks = tiles_per_seq = t // tm
    assert window % tm == 0
    outs = _inproj(x2d, g1, w_main_bf, w_gate_bf, bg_pad, tuple(jnp.asarray(a) for a in tables), tm,
                   table_blocks, tiles_per_seq, window // tm)
    return x2d, tm, outs


def kernel(x_prompt, x_sample, cache_win_k, cache_win_v, state_conv, state_C, state_n, state_m, norm1_g, w_in, b_gate, w_conv, b_conv, mh_norm_g, w_out, norm2_g, w_up, w_down, norm_f_g):
    depth = w_in.shape[0]
    assert depth == 1, "the final norm is fused into the (single) layer's MLP kernel"
    l = 0
    n_gate = 2 * MLSTM_HEADS
    w_l = w_in[l]
    w_main_bf = w_l[:, :IN_COLS - n_gate].astype(BF16)
    w_gate_bf = jnp.pad(w_l[:, IN_COLS - n_gate:], ((0, 0), (0, GATE_PAD - n_gate))).astype(BF16)
    bg_pad = jnp.pad(b_gate[l], (0, GATE_PAD - n_gate))[None, :]
    mix_params = (norm1_g[l][None, :], w_main_bf, w_gate_bf, bg_pad)
    conv_params = (w_conv[l], b_conv[l][None, :], mh_norm_g[l][None, :])
    ffc = D_FF // FF_PARTS
    wu_parts = w_up[l].reshape(D_MODEL, FF_PARTS, ffc).transpose(1, 0, 2).astype(BF16)
    wd_parts = w_down[l].astype(BF16).reshape(FF_PARTS, ffc, D_MODEL)
    mlp_params = (w_out[l].astype(BF16), norm2_g[l][None, :], wu_parts, wd_parts, norm_f_g[None, :])

    bp, tp, _ = x_prompt.shape
    n_keep = min(DILATIONS[-1] * SUB_WINDOW, tp)
    r3 = lambda a: a.reshape(bp, tp, a.shape[-1])
    n_seq = MLSTM_PROMPT_GROUP if bp % MLSTM_PROMPT_GROUP == 0 else 1
    tm_f = ROW_TILE // n_seq
    if tp % tm_f == 0 and n_keep % tm_f == 0 and tm_f % CHUNK == 0:
        xp2d, tm_p = x_prompt.reshape(bp * tp, D_MODEL), ROW_TILE
        tables = tuple(jnp.asarray(a) for a in _rotary_tables(np.arange(tp)))
        q, k, v, kt, vt, hb_p, tail_p, pc, pn, pm = _inproj_mlstm(
            x_prompt, *mix_params, tables, *conv_params, tm_f, n_seq, n_keep)
        hb_p = hb_p.reshape(bp * tp, MLSTM_WIDTH)
    else:
        xp2d, tm_p, (q, k, v, raw, vb, ob, gates, kt, vt) = _project(x_prompt, np.arange(tp), mix_params, ROW_TILE,
                                                                   False, window=n_keep)
        hb_p, tail_p, pc, pn, pm = _mlstm(r3(raw), r3(vb), r3(ob), r3(gates), *conv_params,
                                          CHUNK if tp % CHUNK == 0 else tp,
                                          MLSTM_PROMPT_GROUP if bp % MLSTM_PROMPT_GROUP == 0 else 1)
        hb_p = hb_p.reshape(bp * tp, MLSTM_WIDTH)
    att_p = _attn_prompt(r3(q), r3(k), r3(v)).reshape(bp * tp, ATT_WIDTH)
    heads = lambda a: a.reshape(bp, N_ATT_HEADS, HD, n_keep).transpose(0, 3, 1, 2)
    pk, pv = heads(kt), heads(vt)

    bs, ts, _ = x_sample.shape
    xs2d, tm_s, (q, k, v, raw, vb, ob, gates) = _project(x_sample, PAST_LEN + np.arange(ts), mix_params, ROW_TILE, True)
    r3 = lambda a: a.reshape(bs, ts, a.shape[-1])
    tail0 = jnp.pad(state_conv[l], ((0, 0), (SUBLANES - (CONV_W - 1), 0), (0, 0)))
    m0b = jnp.broadcast_to(state_m[l][:, :, None], (bs, MLSTM_HEADS, LANES))
    hb_s, tail_s, sc, sn, sm = _mlstm(r3(raw), r3(vb), r3(ob), r3(gates), *conv_params, ts,
                                      MLSTM_SAMPLE_GROUP if bs % MLSTM_SAMPLE_GROUP == 0 else 1,
                                      state=(tail0, state_C[l], state_n[l], m0b))
    hb_s = hb_s.reshape(bs * ts, MLSTM_WIDTH)

    kc_t = cache_win_k[l].transpose(0, 2, 3, 1)
    vc_t = cache_win_v[l].transpose(0, 2, 3, 1)
    if _attn_mlp_fits(bs, bp * tp):
        att_s, nk_t, nv_t, y_p = _attn_mlp(q, k, v, kc_t, vc_t, ts, xp2d, att_p, hb_p, *mlp_params)
    else:
        att_s, nk_t, nv_t = _attn_sample(q, k, v, kc_t, vc_t, ts)
        y_p = _outmlp(xp2d, att_p, hb_p, *mlp_params, tm_p)
    sk, sv = nk_t.transpose(0, 3, 1, 2), nv_t.transpose(0, 3, 1, 2)
    y_s = _outmlp(xs2d, att_s, hb_s, *mlp_params, tm_s)

    first_tail = SUBLANES - (CONV_W - 1)
    outs = (y_p.reshape(bp, tp, D_MODEL), y_s.reshape(bs, ts, D_MODEL),
            pk, pv, tail_p[:, first_tail:, :], pc, pn, pm[:, :, 0],
            sk, sv, tail_s[:, first_tail:, :], sc, sn, sm[:, :, 0])
    return outs[:2] + tuple(o[None] for o in outs[2:])
```
